```python
import jax, jax.numpy as jnp
from jax import lax
import numpy as np

D_MODEL = 1024
BATCH = 2
SEQ = 16384
DEPTH = 1
DEC_BATCH = 4
DEC_SEQ = 8192
PAST_LEN = 128

D_MIX = D_MODEL
RET_HEADS = 4
RET_HEAD_DIM = 128
RET_WIDTH = RET_HEADS * RET_HEAD_DIM
MLA_HEADS = 4
MLA_NOPE_DIM = 128
MLA_ROPE_DIM = 64
MLA_QK_DIM = MLA_NOPE_DIM + MLA_ROPE_DIM
MLA_V_DIM = 128
MLA_WIDTH = MLA_HEADS * MLA_V_DIM
Q_LORA = 384
KV_LORA = 256
D_FF = 4 * D_MODEL
D_IN = 4 * RET_WIDTH + Q_LORA + KV_LORA + MLA_ROPE_DIM
IN_SPLITS = (RET_WIDTH, 2 * RET_WIDTH, 3 * RET_WIDTH, 4 * RET_WIDTH,
             4 * RET_WIDTH + Q_LORA, 4 * RET_WIDTH + Q_LORA + KV_LORA)
CHUNK = 128
Q_BLOCK = 128
ROPE_BASE = 10000.0
LN_EPS = 1e-5
RMS_EPS = 1e-6
DEEPNORM_ALPHA = float((2 * DEPTH) ** 0.25)
DEEPNORM_BETA = float((8 * DEPTH) ** -0.25)
N_MOD = 6
MLA_SCALE = float(MLA_QK_DIM ** -0.5)

kernel_name = "hymba_retention_mla_deepnorm_adaln_encoder"


def layer_norm_plain(x):
    xf = x.astype(jnp.float32)
    mu = jnp.mean(xf, -1, keepdims=True)
    var = jnp.mean(jnp.square(xf - mu), -1, keepdims=True)
    return ((xf - mu) * lax.rsqrt(var + LN_EPS)).astype(x.dtype)


def layer_norm_affine(x, w, b):
    xf = x.astype(jnp.float32)
    mu = jnp.mean(xf, -1, keepdims=True)
    var = jnp.mean(jnp.square(xf - mu), -1, keepdims=True)
    return ((xf - mu) * lax.rsqrt(var + LN_EPS) * w + b).astype(x.dtype)


def rms_norm(x, w):
    xf = x.astype(jnp.float32)
    y = xf * lax.rsqrt(jnp.mean(xf * xf, -1, keepdims=True) + RMS_EPS)
    return (y * w).astype(x.dtype)


def apply_rotary(x):
    S, d = x.shape[1], x.shape[-1]
    inv = ROPE_BASE ** (-jnp.arange(0, d, 2, dtype=jnp.float32) / d)
    ang = jnp.arange(S, dtype=jnp.float32)[:, None] * inv[None, :]
    cos = jnp.cos(ang)[:, None, :]
    sin = jnp.sin(ang)[:, None, :]
    xf = x.astype(jnp.float32)
    x1, x2 = xf[..., : d // 2], xf[..., d // 2:]
    return jnp.concatenate([x1 * cos - x2 * sin, x1 * sin + x2 * cos], -1).astype(x.dtype)


def retention_chunkwise(q, k, v, log_gamma, include_diag):
    B, S, H, dk = q.shape
    dv = v.shape[-1]
    n = S // CHUNK
    idx = jnp.arange(CHUNK, dtype=jnp.float32)
    rel = idx[:, None] - idx[None, :]
    mask = (rel >= 0) if include_diag else (rel > 0)
    decay_in = jnp.where(mask[None], jnp.exp(log_gamma[:, None, None] * jnp.maximum(rel, 0.0)[None]), 0.0)
    decay_q = jnp.exp(log_gamma[:, None] * (idx[None, :] + 1.0))
    decay_k = jnp.exp(log_gamma[:, None] * (CHUNK - 1.0 - idx[None, :]))
    decay_chunk = jnp.exp(log_gamma * CHUNK)

    def chunks(t):
        return t.astype(jnp.float32).reshape(B, n, CHUNK, H, t.shape[-1]).transpose(1, 0, 3, 2, 4)

    def step(state, inp):
        qi, ki, vi = inp
        s = jnp.einsum('bhid,bhjd->bhij', qi, ki) * decay_in[None]
        inner = jnp.einsum('bhij,bhje->bhie', s, vi)
        cross = jnp.einsum('bhid,bhde->bhie', qi, state) * decay_q[None, :, :, None]
        new_state = state * decay_chunk[None, :, None, None] + jnp.einsum(
            'bhjd,bhje->bhde', ki * decay_k[None, :, :, None], vi)
        return new_state, inner + cross

    state0 = jnp.zeros((B, H, dk, dv), jnp.float32)
    _, out = lax.scan(step, state0, (chunks(q), chunks(k), chunks(v)))
    return out.transpose(1, 0, 3, 2, 4).reshape(B, S, H, dv)


def bidirectional_retention(q, k, v, log_gamma_f, log_gamma_b):
    fwd = retention_chunkwise(q, k, v, log_gamma_f, True)
    bwd = retention_chunkwise(q[:, ::-1], k[:, ::-1], v[:, ::-1], log_gamma_b, False)[:, ::-1]
    return fwd + bwd


def head_group_norm(y, w):
    B, S, H, dv = y.shape
    mu = jnp.mean(y, -1, keepdims=True)
    var = jnp.mean(jnp.square(y - mu), -1, keepdims=True)
    return ((y - mu) * lax.rsqrt(var + LN_EPS)).reshape(B, S, H * dv) * w


def latent_attention(c_q, c_kv, k_rope, q_norm_w, w_uq, kv_norm_w, w_ukv):
    B, S, _ = c_q.shape
    q = (rms_norm(c_q, q_norm_w) @ w_uq).reshape(B, S, MLA_HEADS, MLA_QK_DIM)
    q = jnp.concatenate([q[..., :MLA_NOPE_DIM], apply_rotary(q[..., MLA_NOPE_DIM:])], -1) * MLA_SCALE
    kv = (rms_norm(c_kv, kv_norm_w) @ w_ukv).reshape(B, S, MLA_HEADS, MLA_NOPE_DIM + MLA_V_DIM)
    k_nope, v = kv[..., :MLA_NOPE_DIM], kv[..., MLA_NOPE_DIM:]
    k_pe = jnp.broadcast_to(apply_rotary(k_rope[:, :, None, :]), (B, S, MLA_HEADS, MLA_ROPE_DIM))
    k = jnp.concatenate([k_nope, k_pe], -1)
    nq = S // Q_BLOCK
    qb = q.reshape(B, nq, Q_BLOCK, MLA_HEADS, MLA_QK_DIM).transpose(1, 0, 2, 3, 4)

    def attend(q_blk):
        s = jnp.einsum('bqhd,bkhd->bhqk', q_blk, k).astype(jnp.float32)
        p = jax.nn.softmax(s, axis=-1).astype(v.dtype)
        return jnp.einsum('bhqk,bkhd->bqhd', p, v)

    o = lax.map(attend, qb)
    return o.transpose(1, 0, 2, 3, 4).reshape(B, S, MLA_WIDTH)


def encoder_layer(x, c, w_ada, b_ada, w_in, ret_decay_f, ret_decay_b, ret_gn_w,
                  q_norm_w, w_uq, kv_norm_w, w_ukv, w_o, ln1_w, ln1_b,
                  w_up, w_down, ln2_w, ln2_b):
    B, S, _ = x.shape
    mod = (jax.nn.silu(c) @ w_ada + b_ada)[:, None, :]
    sh1, sc1, g1, sh2, sc2, g2 = jnp.split(mod, N_MOD, axis=-1)

    h = layer_norm_plain(x) * (1.0 + sc1) + sh1
    r_q, r_k, r_v, r_g, c_q, c_kv, k_rope = jnp.split(h @ w_in, IN_SPLITS, axis=-1)
    heads = lambda t: t.reshape(B, S, RET_HEADS, RET_HEAD_DIM)
    rq = apply_rotary(heads(r_q))
    rk = apply_rotary(heads(r_k)) * (RET_HEAD_DIM ** -0.5)
    lg_f = jax.nn.log_sigmoid(ret_decay_f.astype(jnp.float32))
    lg_b = jax.nn.log_sigmoid(ret_decay_b.astype(jnp.float32))
    ret = bidirectional_retention(rq, rk, heads(r_v), lg_f, lg_b)
    ret = (jax.nn.silu(r_g) * head_group_norm(ret, ret_gn_w)).astype(x.dtype)
    att = latent_attention(c_q, c_kv, k_rope, q_norm_w, w_uq, kv_norm_w, w_ukv)
    mix = jnp.concatenate([ret, att], -1) @ w_o
    x = layer_norm_affine(DEEPNORM_ALPHA * x + g1 * mix, ln1_w, ln1_b)

    h = layer_norm_plain(x) * (1.0 + sc2) + sh2
    ff = jnp.square(jax.nn.relu(h @ w_up)) @ w_down
    return layer_norm_affine(DEEPNORM_ALPHA * x + g2 * ff, ln2_w, ln2_b)


def encoder_trunk(x, c, w_ada, b_ada, w_in, ret_decay_f, ret_decay_b, ret_gn_w,
                  q_norm_w, w_uq, kv_norm_w, w_ukv, w_o, ln1_w, ln1_b,
                  w_up, w_down, ln2_w, ln2_b):
    for l in range(DEPTH):
        x = encoder_layer(x, c, w_ada[l], b_ada[l], w_in[l], ret_decay_f[l], ret_decay_b[l],
                          ret_gn_w[l], q_norm_w[l], w_uq[l], kv_norm_w[l], w_ukv[l], w_o[l],
                          ln1_w[l], ln1_b[l], w_up[l], w_down[l], ln2_w[l], ln2_b[l])
    return x


def setup_inputs(seed: int = 0) -> dict:
    key = jax.random.key(seed)
    ks = jax.random.split(key, 24)
    nrm = lambda k, shape, s: jax.random.normal(k, shape, jnp.float32) * s
    L = DEPTH
    hidx = jnp.arange(RET_HEADS, dtype=jnp.float32)
    dec_f = jnp.log(2.0 ** (5.0 + hidx) - 1.0)
    dec_b = jnp.log(2.0 ** (5.5 + hidx) - 1.0)
    return {
        "x_prompt": nrm(ks[0], (BATCH, SEQ, D_MODEL), 1.0),
        "x_sample": nrm(ks[1], (DEC_BATCH, DEC_SEQ, D_MODEL), 1.0),
        "c_prompt": nrm(ks[2], (BATCH, D_MODEL), 1.0),
        "c_sample": nrm(ks[3], (DEC_BATCH, D_MODEL), 1.0),
        "w_ada": nrm(ks[4], (L, D_MODEL, N_MOD * D_MODEL), D_MODEL ** -0.5),
        "b_ada": nrm(ks[5], (L, N_MOD * D_MODEL), 0.01),
        "w_in": nrm(ks[6], (L, D_MODEL, D_IN), D_MODEL ** -0.5),
        "ret_decay_f": dec_f + nrm(ks[7], (L, RET_HEADS), 0.01),
        "ret_decay_b": dec_b + nrm(ks[8], (L, RET_HEADS), 0.01),
        "ret_gn_w": 1.0 + nrm(ks[9], (L, RET_WIDTH), 0.01),
        "q_norm_w": 1.0 + nrm(ks[10], (L, Q_LORA), 0.01),
        "w_uq": nrm(ks[11], (L, Q_LORA, MLA_HEADS * MLA_QK_DIM), Q_LORA ** -0.5),
        "kv_norm_w": 1.0 + nrm(ks[12], (L, KV_LORA), 0.01),
        "w_ukv": nrm(ks[13], (L, KV_LORA, MLA_HEADS * (MLA_NOPE_DIM + MLA_V_DIM)), KV_LORA ** -0.5),
        "w_o": nrm(ks[14], (L, D_MIX, D_MODEL), (D_MIX ** -0.5) * DEEPNORM_BETA),
        "ln1_w": 1.0 + nrm(ks[15], (L, D_MODEL), 0.01),
        "ln1_b": nrm(ks[16], (L, D_MODEL), 0.01),
        "w_up": nrm(ks[17], (L, D_MODEL, D_FF), D_MODEL ** -0.5),
        "w_down": nrm(ks[18], (L, D_FF, D_MODEL), (D_FF ** -0.5) * DEEPNORM_BETA),
        "ln2_w": 1.0 + nrm(ks[19], (L, D_MODEL), 0.01),
        "ln2_b": nrm(ks[20], (L, D_MODEL), 0.01),
    }


def reference(x_prompt, x_sample, c_prompt, c_sample, w_ada, b_ada, w_in, ret_decay_f,
              ret_decay_b, ret_gn_w, q_norm_w, w_uq, kv_norm_w, w_ukv, w_o, ln1_w, ln1_b,
              w_up, w_down, ln2_w, ln2_b):
    y_prompt = encoder_trunk(x_prompt, c_prompt, w_ada, b_ada, w_in, ret_decay_f, ret_decay_b,
                             ret_gn_w, q_norm_w, w_uq, kv_norm_w, w_ukv, w_o, ln1_w, ln1_b,
                             w_up, w_down, ln2_w, ln2_b)
    y_sample = encoder_trunk(x_sample, c_sample, w_ada, b_ada, w_in, ret_decay_f, ret_decay_b,
                             ret_gn_w, q_norm_w, w_uq, kv_norm_w, w_ukv, w_o, ln1_w, ln1_b,
                             w_up, w_down, ln2_w, ln2_b)
    return (y_prompt, y_sample)
```

```python
import functools
import math

import jax
import jax.numpy as jnp
from jax import lax
from jax.experimental import pallas as pl
from jax.experimental.pallas import tpu as pltpu

F32 = jnp.float32
BF16 = jnp.bfloat16

D_MODEL = 1024
RET_HEADS = 4
RET_HEAD_DIM = 128
RET_WIDTH = RET_HEADS * RET_HEAD_DIM
MLA_HEADS = 4
MLA_NOPE_DIM = 128
MLA_ROPE_DIM = 64
MLA_V_DIM = 128
MLA_WIDTH = MLA_HEADS * MLA_V_DIM
MLA_QK_DIM = MLA_NOPE_DIM + MLA_ROPE_DIM
Q_LORA = 384
KV_LORA = 256
D_FF = 4 * D_MODEL
ROPE_BASE = 10000.0
LN_EPS = 1e-5
RMS_EPS = 1e-6
N_MOD = 6
MLA_SCALE = float(MLA_QK_DIM ** -0.5)
LOG2E = 1.4426950408889634

V7X_LANES = 128
V7X_SUBLANES = 8
V7X_VMEM_BYTES = 64 * 1024 * 1024
VMEM_LIMIT = V7X_VMEM_BYTES - 8 * 1024 * 1024

QK_PAD = 2 * V7X_LANES
_OFF_RQ, _OFF_RV, _OFF_RG = 0, RET_WIDTH, 2 * RET_WIDTH
_OFF_CQ = 3 * RET_WIDTH
_OFF_CKV = _OFF_CQ + Q_LORA
_OFF_KR = _OFF_CKV + KV_LORA
_W_IN_COLS = _OFF_KR + V7X_LANES
ROWS_PAD = V7X_SUBLANES
RET_CHUNK = 128


def _tile(n, pref):
    t = min(n, pref)
    assert n % t == 0, (n, t)
    return t


def _const_spec(shape):
    nd = len(shape)
    return pl.BlockSpec(shape, lambda *_: (0,) * nd, pipeline_mode=pl.Buffered(1))


def _params(*sem):
    return pltpu.CompilerParams(dimension_semantics=sem, vmem_limit_bytes=VMEM_LIMIT)


def _nt_dot(a, b):
    return lax.dot_general(a, b, (((1,), (1,)), ((), ())), preferred_element_type=F32)


def _dot(a, b):
    return jnp.dot(a, b, preferred_element_type=F32)


def _ada_kernel(c_ref, w_ref, b_ref, o_ref):
    c = c_ref[...]
    s = c / (1.0 + jnp.exp(-c))
    o_ref[...] = _dot(s.astype(BF16), w_ref[...].astype(BF16)) + b_ref[...]


def _ada(c_pad, w_ada, b_ada):
    n = w_ada.shape[1]
    tn = _tile(n, D_MODEL)
    return pl.pallas_call(
        _ada_kernel,
        grid=(n // tn,),
        in_specs=[
            pl.BlockSpec((ROWS_PAD, D_MODEL), lambda j: (0, 0)),
            pl.BlockSpec((D_MODEL, tn), lambda j: (0, j)),
            pl.BlockSpec((1, tn), lambda j: (0, j)),
        ],
        out_specs=pl.BlockSpec((ROWS_PAD, tn), lambda j: (0, j)),
        out_shape=jax.ShapeDtypeStruct((ROWS_PAD, n), F32),
        compiler_params=_params("arbitrary"),
    )(c_pad, w_ada, b_ada.reshape(1, n))


def _tables_kernel(inv_r_ref, sgn_r_ref, inv_m_ref, sgn_m_ref, inv_rt_ref, sgn_rt_ref,
                   cos_r_ref, sin_r_ref, cos_m_ref, sin_m_ref, cos_rt_ref, sin_rt_ref):
    tr = cos_r_ref.shape[0]
    base = pl.program_id(0) * tr
    pos = (lax.broadcasted_iota(jnp.int32, (tr, V7X_LANES), 0) + base).astype(F32)
    ang_r = pos * inv_r_ref[...]
    cos_r_ref[...] = jnp.cos(ang_r)
    sin_r_ref[...] = jnp.sin(ang_r) * sgn_r_ref[...]
    ang_m = pos * inv_m_ref[...]
    cos_m_ref[...] = jnp.cos(ang_m)
    sin_m_ref[...] = jnp.sin(ang_m) * sgn_m_ref[...]
    for c0 in range(0, tr, V7X_LANES):
        pos_t = (lax.broadcasted_iota(jnp.int32, (RET_HEAD_DIM, V7X_LANES), 1) + (base + c0)).astype(F32)
        ang_t = pos_t * inv_rt_ref[...]
        cos_rt_ref[:, c0:c0 + V7X_LANES] = jnp.cos(ang_t)
        sin_rt_ref[:, c0:c0 + V7X_LANES] = jnp.sin(ang_t) * sgn_rt_ref[...]


def _rotary_tables(seq):
    half_r = RET_HEAD_DIM // 2
    half_m = MLA_ROPE_DIM // 2
    inv_r = ROPE_BASE ** (-jnp.arange(0, RET_HEAD_DIM, 2, dtype=F32) / RET_HEAD_DIM)
    inv_m = ROPE_BASE ** (-jnp.arange(0, MLA_ROPE_DIM, 2, dtype=F32) / MLA_ROPE_DIM)
    inv_r_full = jnp.tile(inv_r, 2)
    sgn_r_full = jnp.concatenate([-jnp.ones(half_r, F32), jnp.ones(half_r, F32)])
    inv_m_full = jnp.tile(inv_m, V7X_LANES // half_m)
    sgn_m_full = jnp.tile(jnp.concatenate([-jnp.ones(half_m, F32), jnp.ones(half_m, F32)]),
                          V7X_LANES // MLA_ROPE_DIM)
    tr = _tile(seq, 1024)
    row = lambda v: v.reshape(1, V7X_LANES)
    col = lambda v: jnp.broadcast_to(v.reshape(RET_HEAD_DIM, 1), (RET_HEAD_DIM, V7X_LANES))
    tok = jax.ShapeDtypeStruct((seq, V7X_LANES), F32)
    feat = jax.ShapeDtypeStruct((RET_HEAD_DIM, seq), F32)
    vec = pl.BlockSpec((1, V7X_LANES), lambda i: (0, 0))
    cvec = pl.BlockSpec((RET_HEAD_DIM, V7X_LANES), lambda i: (0, 0))
    tok_spec = pl.BlockSpec((tr, V7X_LANES), lambda i: (i, 0))
    feat_spec = pl.BlockSpec((RET_HEAD_DIM, tr), lambda i: (0, i))
    return pl.pallas_call(
        _tables_kernel,
        grid=(seq // tr,),
        in_specs=[vec, vec, vec, vec, cvec, cvec],
        out_specs=[tok_spec, tok_spec, tok_spec, tok_spec, feat_spec, feat_spec],
        out_shape=[tok, tok, tok, tok, feat, feat],
        compiler_params=_params("arbitrary"),
    )(row(inv_r_full), row(sgn_r_full), row(inv_m_full), row(sgn_m_full),
      col(inv_r_full), col(sgn_r_full))


def _layer_norm(x):
    mu = jnp.mean(x, axis=-1, keepdims=True)
    xc = x - mu
    var = jnp.mean(xc * xc, axis=-1, keepdims=True)
    return xc * lax.rsqrt(var + LN_EPS)


def _rms_norm(x, w):
    return x * lax.rsqrt(jnp.mean(x * x, axis=-1, keepdims=True) + RMS_EPS) * w


def _rope64(x, cos_m, sin_m, first_half):
    swapped = jnp.where(first_half,
                        pltpu.roll(x, V7X_LANES - MLA_ROPE_DIM // 2, 1),
                        pltpu.roll(x, MLA_ROPE_DIM // 2, 1))
    return x * cos_m + swapped * sin_m


def _proj_kernel(x_ref, mod_ref, cos_r_ref, sin_r_ref, cos_m_ref, sin_m_ref, cos_rt_ref, sin_rt_ref,
                 w_in_ref, w_kt_ref, qn_w_ref, w_uqn_ref, w_uqr_ref, kvn_w_ref, w_uk_ref, w_uvt_ref,
                 rq_ref, rkt_ref, rv_ref, rg_ref, q_ref, k_ref, vt_ref):
    x = x_ref[0]
    sh1 = mod_ref[0, 0:1, :]
    sc1 = mod_ref[0, 1:2, :]
    h = (_layer_norm(x) * (1.0 + sc1) + sh1).astype(BF16)
    proj = lambda lo, n: _dot(h, w_in_ref[:, lo:lo + n])
    hd = RET_HEAD_DIM

    cos_r = cos_r_ref[...]
    sin_r = sin_r_ref[...]
    rq = proj(_OFF_RQ, RET_WIDTH)
    for hh in range(RET_HEADS):
        t = rq[:, hh * hd:(hh + 1) * hd]
        rq_ref[0, :, hh * hd:(hh + 1) * hd] = (t * cos_r + pltpu.roll(t, hd // 2, 1) * sin_r).astype(BF16)

    cos_rt = cos_rt_ref[...]
    sin_rt = sin_rt_ref[...]
    rkt = _nt_dot(w_kt_ref[...], h)
    k_scale = RET_HEAD_DIM ** -0.5
    for hh in range(RET_HEADS):
        t = rkt[hh * hd:(hh + 1) * hd, :]
        rot = ((t * cos_rt + pltpu.roll(t, hd // 2, 0) * sin_rt) * k_scale).astype(BF16)
        for j in range(rkt_ref.shape[1]):
            rkt_ref[0, j, hh * hd:(hh + 1) * hd, :] = rot[:, j * RET_CHUNK:(j + 1) * RET_CHUNK]

    rv_ref[0] = proj(_OFF_RV, RET_WIDTH).astype(BF16)
    g = proj(_OFF_RG, RET_WIDTH)
    rg_ref[0] = (g / (1.0 + jnp.exp(-g))).astype(BF16)

    cos_m = cos_m_ref[...]
    sin_m = sin_m_ref[...]
    lane = lax.broadcasted_iota(jnp.int32, cos_m.shape, 1)
    first_half = (lane % MLA_ROPE_DIM) < (MLA_ROPE_DIM // 2)
    low_lanes = lane < MLA_ROPE_DIM
    q_scale = MLA_SCALE * LOG2E
    cq = _rms_norm(proj(_OFF_CQ, Q_LORA), qn_w_ref[...]).astype(BF16)
    q_nope = _dot(cq, w_uqn_ref[...]) * q_scale
    q_rope = _dot(cq, w_uqr_ref[...]) * q_scale
    for pair in range(MLA_HEADS // 2):
        r = _rope64(q_rope[:, pair * V7X_LANES:(pair + 1) * V7X_LANES], cos_m, sin_m, first_half)
        for sub in range(2):
            hh = 2 * pair + sub
            keep = low_lanes if sub == 0 else jnp.logical_not(low_lanes)
            q_ref[0, hh, :, 0:MLA_NOPE_DIM] = q_nope[:, hh * MLA_NOPE_DIM:(hh + 1) * MLA_NOPE_DIM].astype(BF16)
            q_ref[0, hh, :, MLA_NOPE_DIM:QK_PAD] = jnp.where(keep, r, 0.0).astype(BF16)

    ckv = _rms_norm(proj(_OFF_CKV, KV_LORA), kvn_w_ref[...]).astype(BF16)
    k_nope = _dot(ckv, w_uk_ref[...])
    v_t = _nt_dot(w_uvt_ref[...], ckv)
    k_pe = _rope64(proj(_OFF_KR, V7X_LANES), cos_m, sin_m, first_half)
    k_pe_hi = pltpu.roll(k_pe, MLA_ROPE_DIM, 1)
    for hh in range(MLA_HEADS):
        k_ref[0, hh, :, 0:MLA_NOPE_DIM] = k_nope[:, hh * MLA_NOPE_DIM:(hh + 1) * MLA_NOPE_DIM].astype(BF16)
        k_ref[0, hh, :, MLA_NOPE_DIM:QK_PAD] = (k_pe if hh % 2 == 0 else k_pe_hi).astype(BF16)
        vt_ref[0, hh] = v_t[hh * MLA_V_DIM:(hh + 1) * MLA_V_DIM, :].astype(BF16)


def _proj(x, mod, mod_row0, tables, w):
    b, s, _ = x.shape
    tm = _tile(s, 512)
    cos_r, sin_r, cos_m, sin_m, cos_rt, sin_rt = tables
    tok_tab = pl.BlockSpec((tm, V7X_LANES), lambda bi, i: (i, 0))
    feat_tab = pl.BlockSpec((RET_HEAD_DIM, tm), lambda bi, i: (0, i))
    tok_out = lambda: pl.BlockSpec((1, tm, RET_WIDTH), lambda bi, i: (bi, i, 0))
    out_shape = [
        jax.ShapeDtypeStruct((b, s, RET_WIDTH), BF16),
        jax.ShapeDtypeStruct((b, s // RET_CHUNK, RET_WIDTH, RET_CHUNK), BF16),
        jax.ShapeDtypeStruct((b, s, RET_WIDTH), BF16),
        jax.ShapeDtypeStruct((b, s, RET_WIDTH), BF16),
        jax.ShapeDtypeStruct((b, MLA_HEADS, s, QK_PAD), BF16),
        jax.ShapeDtypeStruct((b, MLA_HEADS, s, QK_PAD), BF16),
        jax.ShapeDtypeStruct((b, MLA_HEADS, MLA_V_DIM, s), BF16),
    ]
    out_specs = [
        tok_out(),
        pl.BlockSpec((1, tm // RET_CHUNK, RET_WIDTH, RET_CHUNK), lambda bi, i: (bi, i, 0, 0)),
        tok_out(),
        tok_out(),
        pl.BlockSpec((1, MLA_HEADS, tm, QK_PAD), lambda bi, i: (bi, 0, i, 0)),
        pl.BlockSpec((1, MLA_HEADS, tm, QK_PAD), lambda bi, i: (bi, 0, i, 0)),
        pl.BlockSpec((1, MLA_HEADS, MLA_V_DIM, tm), lambda bi, i: (bi, 0, 0, i)),
    ]
    in_specs = [
        pl.BlockSpec((1, tm, D_MODEL), lambda bi, i: (bi, i, 0)),
        pl.BlockSpec((1, N_MOD, D_MODEL), lambda bi, i: (bi + mod_row0, 0, 0)),
        tok_tab, tok_tab, tok_tab, tok_tab, feat_tab, feat_tab,
        _const_spec(w["w_in"].shape), _const_spec(w["w_kt"].shape),
        _const_spec(w["q_norm_w"].shape), _const_spec(w["w_uqn"].shape), _const_spec(w["w_uqr"].shape),
        _const_spec(w["kv_norm_w"].shape), _const_spec(w["w_uk"].shape), _const_spec(w["w_uvt"].shape),
    ]
    return pl.pallas_call(
        _proj_kernel,
        grid=(b, s // tm),
        in_specs=in_specs,
        out_specs=out_specs,
        out_shape=out_shape,
        compiler_params=_params("arbitrary", "arbitrary"),
    )(x, mod, cos_r, sin_r, cos_m, sin_m, cos_rt, sin_rt,
      w["w_in"], w["w_kt"], w["q_norm_w"], w["w_uqn"], w["w_uqr"], w["kv_norm_w"], w["w_uk"], w["w_uvt"])


def _log_sigmoid(x):
    return jnp.minimum(x, 0.0) - jnp.log(1.0 + jnp.exp(-jnp.abs(x)))


def _ret_decay_tables(dec_row, reverse, d_in_ref, d_q_ref, d_k_ref, d_c_ref, hh):
    c = RET_CHUNK
    lg = _log_sigmoid(dec_row)[:, 0:1]
    row = lax.broadcasted_iota(jnp.int32, (c, c), 0)
    col = lax.broadcasted_iota(jnp.int32, (c, c), 1)
    rel = (col - row) if reverse else (row - col)
    mask = (rel > 0) if reverse else (rel >= 0)
    relf = jnp.maximum(rel, 0).astype(F32)
    d_in_ref[hh] = jnp.where(mask, jnp.exp(lg * relf), 0.0)
    qi = lax.broadcasted_iota(jnp.int32, (c, V7X_LANES), 0).astype(F32)
    kj = lax.broadcasted_iota(jnp.int32, (RET_HEAD_DIM, c), 1).astype(F32)
    q_steps = (c - qi) if reverse else (qi + 1.0)
    k_steps = kj if reverse else (c - 1.0 - kj)
    d_q_ref[hh] = jnp.exp(lg * q_steps)
    d_k_ref[hh] = jnp.exp(lg * k_steps)
    d_c_ref[hh] = jnp.exp(lg * float(c)) + jnp.zeros((V7X_SUBLANES, V7X_LANES), F32)


def _ret_kernel(reverse, *refs):
    if reverse:
        (dec_ref, rq_ref, rkt_ref, rv_ref, fwd_ref, rg_ref, gnw_ref, out_ref,
         state_ref, d_in_ref, d_q_ref, d_k_ref, d_c_ref) = refs
    else:
        (dec_ref, rq_ref, rkt_ref, rv_ref, out_ref,
         state_ref, d_in_ref, d_q_ref, d_k_ref, d_c_ref) = refs
    c = RET_CHUNK
    hd = RET_HEAD_DIM
    n_chunks = rq_ref.shape[1] // c

    @pl.when(pl.program_id(1) == 0)
    def _():
        state_ref[...] = jnp.zeros_like(state_ref)
        for hh in range(RET_HEADS):
            _ret_decay_tables(dec_ref[hh:hh + 1, :], reverse, d_in_ref, d_q_ref, d_k_ref, d_c_ref, hh)

    def chunk_body(ci, carry):
        cidx = (n_chunks - 1 - ci) if reverse else ci
        t0 = pl.multiple_of(cidx * c, c)
        for hh in range(RET_HEADS):
            cols = slice(hh * hd, (hh + 1) * hd)
            q = rq_ref[0, pl.ds(t0, c), cols]
            kt = rkt_ref[0, cidx, cols, :]
            v = rv_ref[0, pl.ds(t0, c), cols]
            state = state_ref[hh]
            s = _dot(q, kt) * d_in_ref[hh]
            inner = _dot(s.astype(BF16), v)
            cross = _dot(q, state.astype(BF16)) * d_q_ref[hh]
            kt_dec = (kt.astype(F32) * d_k_ref[hh]).astype(BF16)
            state_ref[hh] = state * d_c_ref[hh][0:1, :] + _dot(kt_dec, v)
            y = inner + cross
            if reverse:
                y = y + fwd_ref[0, pl.ds(t0, c), cols]
                mu = jnp.mean(y, axis=-1, keepdims=True)
                yc = y - mu
                var = jnp.mean(yc * yc, axis=-1, keepdims=True)
                yn = yc * lax.rsqrt(var + LN_EPS) * gnw_ref[:, cols]
                gate = rg_ref[0, pl.ds(t0, c), cols].astype(F32)
                out_ref[0, pl.ds(t0, c), cols] = (gate * yn).astype(out_ref.dtype)
            else:
                out_ref[0, pl.ds(t0, c), cols] = y
        return carry

    lax.fori_loop(0, n_chunks, chunk_body, 0)


def _retention(dec_rows, rq, rkt, rv, reverse, fwd=None, rg=None, gn_w=None):
    b, s, _ = rq.shape
    tb = _tile(s, 1024)
    nb = s // tb
    blk = (lambda i: nb - 1 - i) if reverse else (lambda i: i)
    tok = pl.BlockSpec((1, tb, RET_WIDTH), lambda bi, i: (bi, blk(i), 0))
    feat = pl.BlockSpec((1, tb // RET_CHUNK, RET_WIDTH, RET_CHUNK), lambda bi, i: (bi, blk(i), 0, 0))
    dec_spec = pl.BlockSpec((RET_HEADS, V7X_LANES), lambda bi, i: (0, 0))
    in_specs = [dec_spec, tok, feat, tok]
    args = [dec_rows, rq, rkt, rv]
    if reverse:
        in_specs += [tok, tok, pl.BlockSpec((1, RET_WIDTH), lambda bi, i: (0, 0))]
        args += [fwd, rg, gn_w]
    c = RET_CHUNK
    scratch = [
        pltpu.VMEM((RET_HEADS, RET_HEAD_DIM, RET_HEAD_DIM), F32),
        pltpu.VMEM((RET_HEADS, c, c), F32),
        pltpu.VMEM((RET_HEADS, c, V7X_LANES), F32),
        pltpu.VMEM((RET_HEADS, RET_HEAD_DIM, c), F32),
        pltpu.VMEM((RET_HEADS, V7X_SUBLANES, V7X_LANES), F32),
    ]
    return pl.pallas_call(
        functools.partial(_ret_kernel, reverse),
        grid=(b, nb),
        in_specs=in_specs,
        out_specs=tok,
        out_shape=jax.ShapeDtypeStruct((b, s, RET_WIDTH), BF16 if reverse else F32),
        scratch_shapes=scratch,
        compiler_params=_params("arbitrary", "arbitrary"),
    )(*args)


NEG_BIG = -1e30


def _attn_kernel(q_ref, k_ref, vt_ref, o_ref, m_ref, l_ref, acc_ref):
    ki = pl.program_id(3)

    @pl.when(ki == 0)
    def _():
        m_ref[...] = jnp.full_like(m_ref, NEG_BIG)
        l_ref[...] = jnp.zeros_like(l_ref)
        acc_ref[...] = jnp.zeros_like(acc_ref)

    s = _nt_dot(k_ref[0, 0], q_ref[0, 0])
    m_prev = m_ref[...]
    m_new = jnp.maximum(m_prev, jnp.max(s, axis=0, keepdims=True))
    alpha = jnp.exp2(m_prev - m_new)
    p = jnp.exp2(s - m_new)
    l_ref[...] = alpha * l_ref[...] + jnp.sum(p, axis=0, keepdims=True)
    acc_ref[...] = alpha * acc_ref[...] + _dot(vt_ref[0, 0], p.astype(BF16))
    m_ref[...] = m_new

    @pl.when(ki == pl.num_programs(3) - 1)
    def _():
        o = acc_ref[...] / l_ref[...]
        o_ref[0] = o.T.astype(o_ref.dtype)


def _attention(q, k, vt):
    b, nh, s, _ = q.shape
    tq = _tile(s, 1024)
    tk = _tile(s, 512)
    return pl.pallas_call(
        _attn_kernel,
        grid=(b, nh, s // tq, s // tk),
        in_specs=[
            pl.BlockSpec((1, 1, tq, QK_PAD), lambda bi, hi, qi, ki: (bi, hi, qi, 0)),
            pl.BlockSpec((1, 1, tk, QK_PAD), lambda bi, hi, qi, ki: (bi, hi, ki, 0)),
            pl.BlockSpec((1, 1, MLA_V_DIM, tk), lambda bi, hi, qi, ki: (bi, hi, 0, ki)),
        ],
        out_specs=pl.BlockSpec((1, tq, MLA_V_DIM), lambda bi, hi, qi, ki: (bi, qi, hi)),
        out_shape=jax.ShapeDtypeStruct((b, s, MLA_WIDTH), BF16),
        scratch_shapes=[
            pltpu.VMEM((1, tq), F32),
            pltpu.VMEM((1, tq), F32),
            pltpu.VMEM((MLA_V_DIM, tq), F32),
        ],
        compiler_params=_params("arbitrary", "arbitrary", "arbitrary", "arbitrary"),
    )(q, k, vt)


FF_CHUNK = 1024


def _mlp_kernel(alpha, x_ref, ret_ref, att_ref, mod_ref, w_o_ref, ln1_w_ref, ln1_b_ref,
                w_up_ref, w_down_ref, ln2_w_ref, ln2_b_ref, o_ref):
    x = x_ref[0]
    g1 = mod_ref[0, 2:3, :]
    sh2 = mod_ref[0, 3:4, :]
    sc2 = mod_ref[0, 4:5, :]
    g2 = mod_ref[0, 5:6, :]
    mix = _dot(ret_ref[0], w_o_ref[0:RET_WIDTH, :]) + _dot(att_ref[0], w_o_ref[RET_WIDTH:, :])
    x1 = _layer_norm(alpha * x + g1 * mix) * ln1_w_ref[...] + ln1_b_ref[...]
    h = (_layer_norm(x1) * (1.0 + sc2) + sh2).astype(BF16)
    ff = jnp.zeros_like(x1)
    for c0 in range(0, D_FF, FF_CHUNK):
        u = jnp.maximum(_dot(h, w_up_ref[:, c0:c0 + FF_CHUNK]), 0.0)
        ff = ff + _dot((u * u).astype(BF16), w_down_ref[c0:c0 + FF_CHUNK, :])
    o_ref[0] = _layer_norm(alpha * x1 + g2 * ff) * ln2_w_ref[...] + ln2_b_ref[...]


def _mlp(alpha, x, ret, att, mod, mod_row0, w):
    b, s, _ = x.shape
    tm = _tile(s, 512)
    xs = pl.BlockSpec((1, tm, D_MODEL), lambda bi, i: (bi, i, 0))
    half = pl.BlockSpec((1, tm, RET_WIDTH), lambda bi, i: (bi, i, 0))
    vec = lambda: pl.BlockSpec((1, D_MODEL), lambda bi, i: (0, 0))
    return pl.pallas_call(
        functools.partial(_mlp_kernel, alpha),
        grid=(b, s // tm),
        in_specs=[
            xs, half, half,
            pl.BlockSpec((1, N_MOD, D_MODEL), lambda bi, i: (bi + mod_row0, 0, 0)),
            _const_spec(w["w_o"].shape), vec(), vec(),
            _const_spec(w["w_up"].shape), _const_spec(w["w_down"].shape), vec(), vec(),
        ],
        out_specs=xs,
        out_shape=jax.ShapeDtypeStruct((b, s, D_MODEL), F32),
        compiler_params=_params("arbitrary", "arbitrary"),
    )(x, ret, att, mod, w["w_o"], w["ln1_w"], w["ln1_b"], w["w_up"], w["w_down"], w["ln2_w"], w["ln2_b"])


def _prep_layer_weights(w_in, ret_decay_f, ret_decay_b, ret_gn_w, q_norm_w, w_uq, kv_norm_w, w_ukv,
                        w_o, ln1_w, ln1_b, w_up, w_down, ln2_w, ln2_b):
    rw = RET_WIDTH
    w_rq, w_rk, w_rest = w_in[:, :rw], w_in[:, rw:2 * rw], w_in[:, 2 * rw:]
    pad = jnp.zeros((D_MODEL, _W_IN_COLS - (w_in.shape[1] - rw)), w_in.dtype)
    w_in_p = jnp.concatenate([w_rq, w_rest, pad], axis=1).astype(BF16)
    uq = w_uq.reshape(Q_LORA, MLA_HEADS, MLA_QK_DIM)
    ukv = w_ukv.reshape(KV_LORA, MLA_HEADS, MLA_NOPE_DIM + MLA_V_DIM)
    lanes = lambda v: jnp.broadcast_to(v.reshape(RET_HEADS, 1), (RET_HEADS, V7X_LANES)).astype(F32)
    return {
        "w_in": w_in_p,
        "w_kt": w_rk.T.astype(BF16),
        "q_norm_w": q_norm_w.reshape(1, Q_LORA),
        "w_uqn": uq[:, :, :MLA_NOPE_DIM].reshape(Q_LORA, MLA_HEADS * MLA_NOPE_DIM).astype(BF16),
        "w_uqr": uq[:, :, MLA_NOPE_DIM:].reshape(Q_LORA, MLA_HEADS * MLA_ROPE_DIM).astype(BF16),
        "kv_norm_w": kv_norm_w.reshape(1, KV_LORA),
        "w_uk": ukv[:, :, :MLA_NOPE_DIM].reshape(KV_LORA, MLA_HEADS * MLA_NOPE_DIM).astype(BF16),
        "w_uvt": ukv[:, :, MLA_NOPE_DIM:].reshape(KV_LORA, MLA_HEADS * MLA_V_DIM).T.astype(BF16),
        "dec_f": lanes(ret_decay_f),
        "dec_b": lanes(ret_decay_b),
        "gn_w": ret_gn_w.reshape(1, RET_WIDTH),
        "w_o": w_o.astype(BF16),
        "ln1_w": ln1_w.reshape(1, D_MODEL), "ln1_b": ln1_b.reshape(1, D_MODEL),
        "w_up": w_up.astype(BF16), "w_down": w_down.astype(BF16),
        "ln2_w": ln2_w.reshape(1, D_MODEL), "ln2_b": ln2_b.reshape(1, D_MODEL),
    }


def _layer(alpha, x, mod, mod_row0, tables, w):
    rq, rkt, rv, rg, q, k, vt = _proj(x, mod, mod_row0, tables, w)
    fwd = _retention(w["dec_f"], rq, rkt, rv, reverse=False)
    ret = _retention(w["dec_b"], rq, rkt, rv, reverse=True, fwd=fwd, rg=rg, gn_w=w["gn_w"])
    att = _attention(q, k, vt)
    return _mlp(alpha, x, ret, att, mod, mod_row0, w)


def _trunks(xs, cs, w_ada, b_ada, layer_weights):
    depth = w_ada.shape[0]
    alpha = float((2 * depth) ** 0.25)
    n_rows = sum(c.shape[0] for c in cs)
    assert n_rows <= ROWS_PAD
    c_pad = jnp.concatenate(list(cs) + [jnp.zeros((ROWS_PAD - n_rows, D_MODEL), F32)], axis=0)
    tables = _rotary_tables(max(x.shape[1] for x in xs))
    outs = list(xs)
    for l in range(depth):
        mod = _ada(c_pad, w_ada[l], b_ada[l]).reshape(ROWS_PAD, N_MOD, D_MODEL)
        w = _prep_layer_weights(*[t[l] for t in layer_weights])
        row0 = 0
        for i, c in enumerate(cs):
            outs[i] = _layer(alpha, outs[i], mod, row0, tables, w)
            row0 += c.shape[0]
    return tuple(outs)


def kernel(x_prompt, x_sample, c_prompt, c_sample, w_ada, b_ada, w_in, ret_decay_f, ret_decay_b, ret_gn_w,
           q_norm_w, w_uq, kv_norm_w, w_ukv, w_o, ln1_w, ln1_b, w_up, w_down, ln2_w, ln2_b):
    layer_weights = (w_in, ret_decay_f, ret_decay_b, ret_gn_w, q_norm_w, w_uq, kv_norm_w, w_ukv,
                     w_o, ln1_w, ln1_b, w_up, w_down, ln2_w, ln2_b)
    return _trunks((x_prompt, x_sample), (c_prompt, c_sample), w_ada, b_ada, layer_weights)
```

```python
import functools
import math

import jax
import jax.numpy as jnp
from jax import lax
from jax.experimental import pallas as pl
from jax.experimental.pallas import tpu as pltpu

F32 = jnp.float32
BF16 = jnp.bfloat16

D_MODEL = 1024
RET_HEADS = 4
RET_HEAD_DIM = 128
RET_WIDTH = RET_HEADS * RET_HEAD_DIM
MLA_HEADS = 4
MLA_NOPE_DIM = 128
MLA_ROPE_DIM = 64
MLA_V_DIM = 128
MLA_WIDTH = MLA_HEADS * MLA_V_DIM
MLA_QK_DIM = MLA_NOPE_DIM + MLA_ROPE_DIM
Q_LORA = 384
KV_LORA = 256
D_FF = 4 * D_MODEL
ROPE_BASE = 10000.0
LN_EPS = 1e-5
RMS_EPS = 1e-6
N_MOD = 6
MLA_SCALE = float(MLA_QK_DIM ** -0.5)
LOG2E = 1.4426950408889634

V7X_LANES = 128
V7X_SUBLANES = 8
V7X_VMEM_BYTES = 64 * 1024 * 1024
VMEM_LIMIT = V7X_VMEM_BYTES - 8 * 1024 * 1024

QK_PAD = 2 * V7X_LANES
_OFF_RQ, _OFF_RV, _OFF_RG = 0, RET_WIDTH, 2 * RET_WIDTH
_OFF_CQ = 3 * RET_WIDTH
_OFF_CKV = _OFF_CQ + Q_LORA
_OFF_KR = _OFF_CKV + KV_LORA
_W_IN_COLS = _OFF_KR + V7X_LANES
ROWS_PAD = V7X_SUBLANES
RET_CHUNK = 128


def _tile(n, pref):
    t = min(n, pref)
    assert n % t == 0, (n, t)
    return t


def _const_spec(shape):
    nd = len(shape)
    return pl.BlockSpec(shape, lambda *_: (0,) * nd, pipeline_mode=pl.Buffered(1))


def _params(*sem):
    return pltpu.CompilerParams(dimension_semantics=sem, vmem_limit_bytes=VMEM_LIMIT)


def _nt_dot(a, b):
    return lax.dot_general(a, b, (((1,), (1,)), ((), ())), preferred_element_type=F32)


def _dot(a, b):
    return jnp.dot(a, b, preferred_element_type=F32)


def _ada_kernel(c_ref, w_ref, b_ref, o_ref):
    c = c_ref[...]
    s = c / (1.0 + jnp.exp(-c))
    o_ref[...] = _dot(s.astype(BF16), w_ref[...].astype(BF16)) + b_ref[...]


def _ada(c_pad, w_ada, b_ada):
    n = w_ada.shape[1]
    tn = _tile(n, D_MODEL)
    return pl.pallas_call(
        _ada_kernel,
        grid=(n // tn,),
        in_specs=[
            pl.BlockSpec((ROWS_PAD, D_MODEL), lambda j: (0, 0)),
            pl.BlockSpec((D_MODEL, tn), lambda j: (0, j)),
            pl.BlockSpec((1, tn), lambda j: (0, j)),
        ],
        out_specs=pl.BlockSpec((ROWS_PAD, tn), lambda j: (0, j)),
        out_shape=jax.ShapeDtypeStruct((ROWS_PAD, n), F32),
        compiler_params=_params("arbitrary"),
        name="ada",
    )(c_pad, w_ada, b_ada.reshape(1, n))


def _tables_kernel(inv_r_ref, sgn_r_ref, inv_m_ref, sgn_m_ref, inv_rt_ref, sgn_rt_ref,
                   cos_r_ref, sin_r_ref, cos_m_ref, sin_m_ref, cos_rt_ref, sin_rt_ref):
    tr = cos_r_ref.shape[0]
    base = pl.program_id(0) * tr
    pos = (lax.broadcasted_iota(jnp.int32, (tr, V7X_LANES), 0) + base).astype(F32)
    ang_r = pos * inv_r_ref[...]
    cos_r_ref[...] = jnp.cos(ang_r)
    sin_r_ref[...] = jnp.sin(ang_r) * sgn_r_ref[...]
    ang_m = pos * inv_m_ref[...]
    cos_m_ref[...] = jnp.cos(ang_m)
    sin_m_ref[...] = jnp.sin(ang_m) * sgn_m_ref[...]
    for c0 in range(0, tr, V7X_LANES):
        pos_t = (lax.broadcasted_iota(jnp.int32, (RET_HEAD_DIM, V7X_LANES), 1) + (base + c0)).astype(F32)
        ang_t = pos_t * inv_rt_ref[...]
        cos_rt_ref[:, c0:c0 + V7X_LANES] = jnp.cos(ang_t)
        sin_rt_ref[:, c0:c0 + V7X_LANES] = jnp.sin(ang_t) * sgn_rt_ref[...]


def _rotary_tables(seq):
    half_r = RET_HEAD_DIM // 2
    half_m = MLA_ROPE_DIM // 2
    inv_r = ROPE_BASE ** (-jnp.arange(0, RET_HEAD_DIM, 2, dtype=F32) / RET_HEAD_DIM)
    inv_m = ROPE_BASE ** (-jnp.arange(0, MLA_ROPE_DIM, 2, dtype=F32) / MLA_ROPE_DIM)
    inv_r_full = jnp.tile(inv_r, 2)
    sgn_r_full = jnp.concatenate([-jnp.ones(half_r, F32), jnp.ones(half_r, F32)])
    inv_m_full = jnp.tile(inv_m, V7X_LANES // half_m)
    sgn_m_full = jnp.tile(jnp.concatenate([-jnp.ones(half_m, F32), jnp.ones(half_m, F32)]),
                          V7X_LANES // MLA_ROPE_DIM)
    tr = _tile(seq, 1024)
    row = lambda v: v.reshape(1, V7X_LANES)
    col = lambda v: jnp.broadcast_to(v.reshape(RET_HEAD_DIM, 1), (RET_HEAD_DIM, V7X_LANES))
    tok = jax.ShapeDtypeStruct((seq, V7X_LANES), F32)
    feat = jax.ShapeDtypeStruct((RET_HEAD_DIM, seq), F32)
    vec = pl.BlockSpec((1, V7X_LANES), lambda i: (0, 0))
    cvec = pl.BlockSpec((RET_HEAD_DIM, V7X_LANES), lambda i: (0, 0))
    tok_spec = pl.BlockSpec((tr, V7X_LANES), lambda i: (i, 0))
    feat_spec = pl.BlockSpec((RET_HEAD_DIM, tr), lambda i: (0, i))
    return pl.pallas_call(
        _tables_kernel,
        grid=(seq // tr,),
        in_specs=[vec, vec, vec, vec, cvec, cvec],
        out_specs=[tok_spec, tok_spec, tok_spec, tok_spec, feat_spec, feat_spec],
        out_shape=[tok, tok, tok, tok, feat, feat],
        compiler_params=_params("arbitrary"),
        name="tables",
    )(row(inv_r_full), row(sgn_r_full), row(inv_m_full), row(sgn_m_full),
      col(inv_r_full), col(sgn_r_full))


def _layer_norm(x):
    mu = jnp.mean(x, axis=-1, keepdims=True)
    xc = x - mu
    var = jnp.mean(xc * xc, axis=-1, keepdims=True)
    return xc * lax.rsqrt(var + LN_EPS)


def _rms_norm(x, w):
    return x * lax.rsqrt(jnp.mean(x * x, axis=-1, keepdims=True) + RMS_EPS) * w


def _rope64(x, cos_m, sin_m, first_half):
    swapped = jnp.where(first_half,
                        pltpu.roll(x, V7X_LANES - MLA_ROPE_DIM // 2, 1),
                        pltpu.roll(x, MLA_ROPE_DIM // 2, 1))
    return x * cos_m + swapped * sin_m


def _proj_kernel(x_ref, mod_ref, cos_r_ref, sin_r_ref, cos_m_ref, sin_m_ref, cos_rt_ref, sin_rt_ref,
                 w_in_ref, w_kt_ref, qn_w_ref, w_uqn_ref, w_uqr_ref, kvn_w_ref, w_uk_ref, w_uvt_ref,
                 rq_ref, rkt_ref, rv_ref, rg_ref, q_ref, k_ref, vt_ref):
    x = x_ref[0]
    sh1 = mod_ref[0, 0:1, :]
    sc1 = mod_ref[0, 1:2, :]
    h = (_layer_norm(x) * (1.0 + sc1) + sh1).astype(BF16)
    proj = lambda lo, n: _dot(h, w_in_ref[:, lo:lo + n])
    hd = RET_HEAD_DIM

    cos_r = cos_r_ref[...]
    sin_r = sin_r_ref[...]
    rq = proj(_OFF_RQ, RET_WIDTH)
    for hh in range(RET_HEADS):
        t = rq[:, hh * hd:(hh + 1) * hd]
        rq_ref[0, :, hh * hd:(hh + 1) * hd] = (t * cos_r + pltpu.roll(t, hd // 2, 1) * sin_r).astype(BF16)

    cos_rt = cos_rt_ref[...]
    sin_rt = sin_rt_ref[...]
    rkt = _nt_dot(w_kt_ref[...], h)
    k_scale = RET_HEAD_DIM ** -0.5
    for hh in range(RET_HEADS):
        t = rkt[hh * hd:(hh + 1) * hd, :]
        rot = ((t * cos_rt + pltpu.roll(t, hd // 2, 0) * sin_rt) * k_scale).astype(BF16)
        for j in range(rkt_ref.shape[1]):
            rkt_ref[0, j, hh * hd:(hh + 1) * hd, :] = rot[:, j * RET_CHUNK:(j + 1) * RET_CHUNK]

    rv_ref[0] = proj(_OFF_RV, RET_WIDTH).astype(BF16)
    g = proj(_OFF_RG, RET_WIDTH)
    rg_ref[0] = (g / (1.0 + jnp.exp(-g))).astype(BF16)

    cos_m = cos_m_ref[...]
    sin_m = sin_m_ref[...]
    lane = lax.broadcasted_iota(jnp.int32, cos_m.shape, 1)
    first_half = (lane % MLA_ROPE_DIM) < (MLA_ROPE_DIM // 2)
    low_lanes = lane < MLA_ROPE_DIM
    q_scale = MLA_SCALE * LOG2E
    cq = _rms_norm(proj(_OFF_CQ, Q_LORA), qn_w_ref[...]).astype(BF16)
    q_nope = _dot(cq, w_uqn_ref[...]) * q_scale
    q_rope = _dot(cq, w_uqr_ref[...]) * q_scale
    for pair in range(MLA_HEADS // 2):
        r = _rope64(q_rope[:, pair * V7X_LANES:(pair + 1) * V7X_LANES], cos_m, sin_m, first_half)
        for sub in range(2):
            hh = 2 * pair + sub
            keep = low_lanes if sub == 0 else jnp.logical_not(low_lanes)
            q_ref[0, hh, :, 0:MLA_NOPE_DIM] = q_nope[:, hh * MLA_NOPE_DIM:(hh + 1) * MLA_NOPE_DIM].astype(BF16)
            q_ref[0, hh, :, MLA_NOPE_DIM:QK_PAD] = jnp.where(keep, r, 0.0).astype(BF16)

    ckv = _rms_norm(proj(_OFF_CKV, KV_LORA), kvn_w_ref[...]).astype(BF16)
    k_nope = _dot(ckv, w_uk_ref[...])
    v_t = _nt_dot(w_uvt_ref[...], ckv)
    k_pe = _rope64(proj(_OFF_KR, V7X_LANES), cos_m, sin_m, first_half)
    k_pe_hi = pltpu.roll(k_pe, MLA_ROPE_DIM, 1)
    for hh in range(MLA_HEADS):
        k_ref[0, hh, :, 0:MLA_NOPE_DIM] = k_nope[:, hh * MLA_NOPE_DIM:(hh + 1) * MLA_NOPE_DIM].astype(BF16)
        k_ref[0, hh, :, MLA_NOPE_DIM:QK_PAD] = (k_pe if hh % 2 == 0 else k_pe_hi).astype(BF16)
        vt_ref[0, hh] = v_t[hh * MLA_V_DIM:(hh + 1) * MLA_V_DIM, :].astype(BF16)


def _proj(x, mod, mod_row0, tables, w):
    b, s, _ = x.shape
    tm = _tile(s, 512)
    cos_r, sin_r, cos_m, sin_m, cos_rt, sin_rt = tables
    tok_tab = pl.BlockSpec((tm, V7X_LANES), lambda bi, i: (i, 0))
    feat_tab = pl.BlockSpec((RET_HEAD_DIM, tm), lambda bi, i: (0, i))
    tok_out = lambda: pl.BlockSpec((1, tm, RET_WIDTH), lambda bi, i: (bi, i, 0))
    out_shape = [
        jax.ShapeDtypeStruct((b, s, RET_WIDTH), BF16),
        jax.ShapeDtypeStruct((b, s // RET_CHUNK, RET_WIDTH, RET_CHUNK), BF16),
        jax.ShapeDtypeStruct((b, s, RET_WIDTH), BF16),
        jax.ShapeDtypeStruct((b, s, RET_WIDTH), BF16),
        jax.ShapeDtypeStruct((b, MLA_HEADS, s, QK_PAD), BF16),
        jax.ShapeDtypeStruct((b, MLA_HEADS, s, QK_PAD), BF16),
        jax.ShapeDtypeStruct((b, MLA_HEADS, MLA_V_DIM, s), BF16),
    ]
    out_specs = [
        tok_out(),
        pl.BlockSpec((1, tm // RET_CHUNK, RET_WIDTH, RET_CHUNK), lambda bi, i: (bi, i, 0, 0)),
        tok_out(),
        tok_out(),
        pl.BlockSpec((1, MLA_HEADS, tm, QK_PAD), lambda bi, i: (bi, 0, i, 0)),
        pl.BlockSpec((1, MLA_HEADS, tm, QK_PAD), lambda bi, i: (bi, 0, i, 0)),
        pl.BlockSpec((1, MLA_HEADS, MLA_V_DIM, tm), lambda bi, i: (bi, 0, 0, i)),
    ]
    in_specs = [
        pl.BlockSpec((1, tm, D_MODEL), lambda bi, i: (bi, i, 0)),
        pl.BlockSpec((1, N_MOD, D_MODEL), lambda bi, i: (bi + mod_row0, 0, 0)),
        tok_tab, tok_tab, tok_tab, tok_tab, feat_tab, feat_tab,
        _const_spec(w["w_in"].shape), _const_spec(w["w_kt"].shape),
        _const_spec(w["q_norm_w"].shape), _const_spec(w["w_uqn"].shape), _const_spec(w["w_uqr"].shape),
        _const_spec(w["kv_norm_w"].shape), _const_spec(w["w_uk"].shape), _const_spec(w["w_uvt"].shape),
    ]
    return pl.pallas_call(
        _proj_kernel,
        grid=(b, s // tm),
        in_specs=in_specs,
        out_specs=out_specs,
        out_shape=out_shape,
        compiler_params=_params("arbitrary", "arbitrary"),
        name="proj",
    )(x, mod, cos_r, sin_r, cos_m, sin_m, cos_rt, sin_rt,
      w["w_in"], w["w_kt"], w["q_norm_w"], w["w_uqn"], w["w_uqr"], w["kv_norm_w"], w["w_uk"], w["w_uvt"])


def _log_sigmoid(x):
    return jnp.minimum(x, 0.0) - jnp.log(1.0 + jnp.exp(-jnp.abs(x)))


def _ret_decay_tables(dec_row, reverse, d_in_ref, d_q_ref, d_k_ref, d_c_ref, hh):
    c = RET_CHUNK
    lg = _log_sigmoid(dec_row)[:, 0:1]
    row = lax.broadcasted_iota(jnp.int32, (c, c), 0)
    col = lax.broadcasted_iota(jnp.int32, (c, c), 1)
    rel = (col - row) if reverse else (row - col)
    mask = (rel > 0) if reverse else (rel >= 0)
    relf = jnp.maximum(rel, 0).astype(F32)
    d_in_ref[hh] = jnp.where(mask, jnp.exp(lg * relf), 0.0)
    qi = lax.broadcasted_iota(jnp.int32, (c, V7X_LANES), 0).astype(F32)
    kj = lax.broadcasted_iota(jnp.int32, (RET_HEAD_DIM, c), 1).astype(F32)
    q_steps = (c - qi) if reverse else (qi + 1.0)
    k_steps = kj if reverse else (c - 1.0 - kj)
    d_q_ref[hh] = jnp.exp(lg * q_steps)
    d_k_ref[hh] = jnp.exp(lg * k_steps)
    d_c_ref[hh] = jnp.exp(lg * float(c)) + jnp.zeros((V7X_SUBLANES, V7X_LANES), F32)


def _ret_kernel(reverse, *refs):
    if reverse:
        (dec_ref, rq_ref, rkt_ref, rv_ref, fwd_ref, rg_ref, gnw_ref, out_ref,
         state_ref, d_in_ref, d_q_ref, d_k_ref, d_c_ref) = refs
    else:
        (dec_ref, rq_ref, rkt_ref, rv_ref, out_ref,
         state_ref, d_in_ref, d_q_ref, d_k_ref, d_c_ref) = refs
    c = RET_CHUNK
    hd = RET_HEAD_DIM
    n_chunks = rq_ref.shape[1] // c

    @pl.when(pl.program_id(1) == 0)
    def _():
        state_ref[...] = jnp.zeros_like(state_ref)
        for hh in range(RET_HEADS):
            _ret_decay_tables(dec_ref[hh:hh + 1, :], reverse, d_in_ref, d_q_ref, d_k_ref, d_c_ref, hh)

    def chunk_body(ci, carry):
        cidx = (n_chunks - 1 - ci) if reverse else ci
        t0 = pl.multiple_of(cidx * c, c)
        for hh in range(RET_HEADS):
            cols = slice(hh * hd, (hh + 1) * hd)
            q = rq_ref[0, pl.ds(t0, c), cols]
            kt = rkt_ref[0, cidx, cols, :]
            v = rv_ref[0, pl.ds(t0, c), cols]
            state = state_ref[hh]
            s = _dot(q, kt) * d_in_ref[hh]
            inner = _dot(s.astype(BF16), v)
            cross = _dot(q, state.astype(BF16)) * d_q_ref[hh]
            kt_dec = (kt.astype(F32) * d_k_ref[hh]).astype(BF16)
            state_ref[hh] = state * d_c_ref[hh][0:1, :] + _dot(kt_dec, v)
            y = inner + cross
            if reverse:
                y = y + fwd_ref[0, pl.ds(t0, c), cols]
                mu = jnp.mean(y, axis=-1, keepdims=True)
                yc = y - mu
                var = jnp.mean(yc * yc, axis=-1, keepdims=True)
                yn = yc * lax.rsqrt(var + LN_EPS) * gnw_ref[:, cols]
                gate = rg_ref[0, pl.ds(t0, c), cols].astype(F32)
                out_ref[0, pl.ds(t0, c), cols] = (gate * yn).astype(out_ref.dtype)
            else:
                out_ref[0, pl.ds(t0, c), cols] = y
        return carry

    lax.fori_loop(0, n_chunks, chunk_body, 0)


def _retention(dec_rows, rq, rkt, rv, reverse, fwd=None, rg=None, gn_w=None):
    b, s, _ = rq.shape
    tb = _tile(s, 1024)
    nb = s // tb
    blk = (lambda i: nb - 1 - i) if reverse else (lambda i: i)
    tok = pl.BlockSpec((1, tb, RET_WIDTH), lambda bi, i: (bi, blk(i), 0))
    feat = pl.BlockSpec((1, tb // RET_CHUNK, RET_WIDTH, RET_CHUNK), lambda bi, i: (bi, blk(i), 0, 0))
    dec_spec = pl.BlockSpec((RET_HEADS, V7X_LANES), lambda bi, i: (0, 0))
    in_specs = [dec_spec, tok, feat, tok]
    args = [dec_rows, rq, rkt, rv]
    if reverse:
        in_specs += [tok, tok, pl.BlockSpec((1, RET_WIDTH), lambda bi, i: (0, 0))]
        args += [fwd, rg, gn_w]
    c = RET_CHUNK
    scratch = [
        pltpu.VMEM((RET_HEADS, RET_HEAD_DIM, RET_HEAD_DIM), F32),
        pltpu.VMEM((RET_HEADS, c, c), F32),
        pltpu.VMEM((RET_HEADS, c, V7X_LANES), F32),
        pltpu.VMEM((RET_HEADS, RET_HEAD_DIM, c), F32),
        pltpu.VMEM((RET_HEADS, V7X_SUBLANES, V7X_LANES), F32),
    ]
    return pl.pallas_call(
        functools.partial(_ret_kernel, reverse),
        grid=(b, nb),
        in_specs=in_specs,
        out_specs=tok,
        out_shape=jax.ShapeDtypeStruct((b, s, RET_WIDTH), BF16 if reverse else F32),
        scratch_shapes=scratch,
        compiler_params=_params("arbitrary", "arbitrary"),
        name="ret_bwd" if reverse else "ret_fwd",
    )(*args)


NEG_BIG = -1e30


ATTN_GROUP = 2 * V7X_LANES


def _attn_kernel(q_ref, k_ref, vprev_ref, vcur_ref, o_ref, s_ref, m_old_ref, m_cur_ref, l_ref, acc_ref):
    t = pl.program_id(3)
    nk = pl.num_programs(3) - 1
    gw = ATTN_GROUP
    ng = q_ref.shape[2] // gw
    group = lambda g: slice(g * gw, (g + 1) * gw)

    @pl.when(t == 0)
    def _():
        m_old_ref[...] = jnp.full_like(m_old_ref, NEG_BIG)
        m_cur_ref[...] = jnp.full_like(m_cur_ref, NEG_BIG)
        l_ref[...] = jnp.zeros_like(l_ref)
        acc_ref[...] = jnp.zeros_like(acc_ref)
        s_ref[...] = jnp.full_like(s_ref, -jnp.inf)

    def softmax_pv(g, vt, m_prev, m_new):
        cols = group(g)
        alpha = jnp.exp2(m_prev - m_new)
        p = jnp.exp2(s_ref[:, cols] - m_new)
        l_ref[:, cols] = alpha * l_ref[:, cols] + jnp.sum(p, axis=0, keepdims=True)
        acc_ref[:, cols] = alpha * acc_ref[:, cols] + _dot(vt, p.astype(BF16))

    def scores(g, m_prev):
        cols = group(g)
        s = _nt_dot(k_ref[0, 0], q_ref[0, 0, cols, :])
        s_ref[:, cols] = s
        return jnp.maximum(m_prev, jnp.max(s, axis=0, keepdims=True))

    @pl.when(t < nk)
    def _():
        m_prev0 = m_cur_ref[:, group(0)]
        m_new0 = scores(0, m_prev0)
        for g in range(1, ng):
            cols = group(g)
            m_c = m_cur_ref[:, cols]
            softmax_pv(g, vprev_ref[0, 0], m_old_ref[:, cols], m_c)
            m_old_ref[:, cols] = m_c
            m_cur_ref[:, cols] = scores(g, m_c)
        softmax_pv(0, vcur_ref[0, 0], m_prev0, m_new0)
        m_cur_ref[:, group(0)] = m_new0

    @pl.when(t == nk)
    def _():
        for g in range(1, ng):
            cols = group(g)
            softmax_pv(g, vprev_ref[0, 0], m_old_ref[:, cols], m_cur_ref[:, cols])
        o = acc_ref[...] / l_ref[...]
        o_ref[0] = o.T.astype(o_ref.dtype)


def _attention(q, k, vt):
    b, nh, s, _ = q.shape
    tq = _tile(s, 2048)
    tk = _tile(s, 1024)
    nk = s // tk
    return pl.pallas_call(
        _attn_kernel,
        grid=(b, nh, s // tq, nk + 1),
        in_specs=[
            pl.BlockSpec((1, 1, tq, QK_PAD), lambda bi, hi, qi, t: (bi, hi, qi, 0)),
            pl.BlockSpec((1, 1, tk, QK_PAD), lambda bi, hi, qi, t: (bi, hi, jnp.minimum(t, nk - 1), 0)),
            pl.BlockSpec((1, 1, MLA_V_DIM, tk), lambda bi, hi, qi, t: (bi, hi, 0, jnp.maximum(t - 1, 0))),
            pl.BlockSpec((1, 1, MLA_V_DIM, tk), lambda bi, hi, qi, t: (bi, hi, 0, jnp.minimum(t, nk - 1))),
        ],
        out_specs=pl.BlockSpec((1, tq, MLA_V_DIM), lambda bi, hi, qi, t: (bi, qi, hi)),
        out_shape=jax.ShapeDtypeStruct((b, s, MLA_WIDTH), BF16),
        scratch_shapes=[
            pltpu.VMEM((tk, tq), F32),
            pltpu.VMEM((1, tq), F32),
            pltpu.VMEM((1, tq), F32),
            pltpu.VMEM((1, tq), F32),
            pltpu.VMEM((MLA_V_DIM, tq), F32),
        ],
        compiler_params=_params("arbitrary", "arbitrary", "arbitrary", "arbitrary"),
        name="attn",
    )(q, k, vt, vt)


FF_CHUNK = 1024


def _mlp_kernel(alpha, x_ref, ret_ref, att_ref, mod_ref, w_o_ref, ln1_w_ref, ln1_b_ref,
                w_up_ref, w_down_ref, ln2_w_ref, ln2_b_ref, o_ref):
    x = x_ref[0]
    g1 = mod_ref[0, 2:3, :]
    sh2 = mod_ref[0, 3:4, :]
    sc2 = mod_ref[0, 4:5, :]
    g2 = mod_ref[0, 5:6, :]
    mix = _dot(ret_ref[0], w_o_ref[0:RET_WIDTH, :]) + _dot(att_ref[0], w_o_ref[RET_WIDTH:, :])
    x1 = _layer_norm(alpha * x + g1 * mix) * ln1_w_ref[...] + ln1_b_ref[...]
    h = (_layer_norm(x1) * (1.0 + sc2) + sh2).astype(BF16)
    ff = jnp.zeros_like(x1)
    for c0 in range(0, D_FF, FF_CHUNK):
        u = jnp.maximum(_dot(h, w_up_ref[:, c0:c0 + FF_CHUNK]), 0.0)
        ff = ff + _dot((u * u).astype(BF16), w_down_ref[c0:c0 + FF_CHUNK, :])
    o_ref[0] = _layer_norm(alpha * x1 + g2 * ff) * ln2_w_ref[...] + ln2_b_ref[...]


def _mlp(alpha, x, ret, att, mod, mod_row0, w):
    b, s, _ = x.shape
    tm = _tile(s, 512)
    xs = pl.BlockSpec((1, tm, D_MODEL), lambda bi, i: (bi, i, 0))
    half = pl.BlockSpec((1, tm, RET_WIDTH), lambda bi, i: (bi, i, 0))
    vec = lambda: pl.BlockSpec((1, D_MODEL), lambda bi, i: (0, 0))
    return pl.pallas_call(
        functools.partial(_mlp_kernel, alpha),
        grid=(b, s // tm),
        in_specs=[
            xs, half, half,
            pl.BlockSpec((1, N_MOD, D_MODEL), lambda bi, i: (bi + mod_row0, 0, 0)),
            _const_spec(w["w_o"].shape), vec(), vec(),
            _const_spec(w["w_up"].shape), _const_spec(w["w_down"].shape), vec(), vec(),
        ],
        out_specs=xs,
        out_shape=jax.ShapeDtypeStruct((b, s, D_MODEL), F32),
        compiler_params=_params("arbitrary", "arbitrary"),
        name="mlp",
    )(x, ret, att, mod, w["w_o"], w["ln1_w"], w["ln1_b"], w["w_up"], w["w_down"], w["ln2_w"], w["ln2_b"])


def _prep_layer_weights(w_in, ret_decay_f, ret_decay_b, ret_gn_w, q_norm_w, w_uq, kv_norm_w, w_ukv,
                        w_o, ln1_w, ln1_b, w_up, w_down, ln2_w, ln2_b):
    rw = RET_WIDTH
    w_rq, w_rk, w_rest = w_in[:, :rw], w_in[:, rw:2 * rw], w_in[:, 2 * rw:]
    pad = jnp.zeros((D_MODEL, _W_IN_COLS - (w_in.shape[1] - rw)), w_in.dtype)
    w_in_p = jnp.concatenate([w_rq, w_rest, pad], axis=1).astype(BF16)
    uq = w_uq.reshape(Q_LORA, MLA_HEADS, MLA_QK_DIM)
    ukv = w_ukv.reshape(KV_LORA, MLA_HEADS, MLA_NOPE_DIM + MLA_V_DIM)
    lanes = lambda v: jnp.broadcast_to(v.reshape(RET_HEADS, 1), (RET_HEADS, V7X_LANES)).astype(F32)
    return {
        "w_in": w_in_p,
        "w_kt": w_rk.T.astype(BF16),
        "q_norm_w": q_norm_w.reshape(1, Q_LORA),
        "w_uqn": uq[:, :, :MLA_NOPE_DIM].reshape(Q_LORA, MLA_HEADS * MLA_NOPE_DIM).astype(BF16),
        "w_uqr": uq[:, :, MLA_NOPE_DIM:].reshape(Q_LORA, MLA_HEADS * MLA_ROPE_DIM).astype(BF16),
        "kv_norm_w": kv_norm_w.reshape(1, KV_LORA),
        "w_uk": ukv[:, :, :MLA_NOPE_DIM].reshape(KV_LORA, MLA_HEADS * MLA_NOPE_DIM).astype(BF16),
        "w_uvt": ukv[:, :, MLA_NOPE_DIM:].reshape(KV_LORA, MLA_HEADS * MLA_V_DIM).T.astype(BF16),
        "dec_f": lanes(ret_decay_f),
        "dec_b": lanes(ret_decay_b),
        "gn_w": ret_gn_w.reshape(1, RET_WIDTH),
        "w_o": w_o.astype(BF16),
        "ln1_w": ln1_w.reshape(1, D_MODEL), "ln1_b": ln1_b.reshape(1, D_MODEL),
        "w_up": w_up.astype(BF16), "w_down": w_down.astype(BF16),
        "ln2_w": ln2_w.reshape(1, D_MODEL), "ln2_b": ln2_b.reshape(1, D_MODEL),
    }


def _layer(alpha, x, mod, mod_row0, tables, w):
    rq, rkt, rv, rg, q, k, vt = _proj(x, mod, mod_row0, tables, w)
    fwd = _retention(w["dec_f"], rq, rkt, rv, reverse=False)
    ret = _retention(w["dec_b"], rq, rkt, rv, reverse=True, fwd=fwd, rg=rg, gn_w=w["gn_w"])
    att = _attention(q, k, vt)
    return _mlp(alpha, x, ret, att, mod, mod_row0, w)


def _trunks(xs, cs, w_ada, b_ada, layer_weights):
    depth = w_ada.shape[0]
    alpha = float((2 * depth) ** 0.25)
    n_rows = sum(c.shape[0] for c in cs)
    assert n_rows <= ROWS_PAD
    c_pad = jnp.concatenate(list(cs) + [jnp.zeros((ROWS_PAD - n_rows, D_MODEL), F32)], axis=0)
    tables = _rotary_tables(max(x.shape[1] for x in xs))
    outs = list(xs)
    for l in range(depth):
        mod = _ada(c_pad, w_ada[l], b_ada[l]).reshape(ROWS_PAD, N_MOD, D_MODEL)
        w = _prep_layer_weights(*[t[l] for t in layer_weights])
        row0 = 0
        for i, c in enumerate(cs):
            outs[i] = _layer(alpha, outs[i], mod, row0, tables, w)
            row0 += c.shape[0]
    return tuple(outs)


def kernel(x_prompt, x_sample, c_prompt, c_sample, w_ada, b_ada, w_in, ret_decay_f, ret_decay_b, ret_gn_w,
           q_norm_w, w_uq, kv_norm_w, w_ukv, w_o, ln1_w, ln1_b, w_up, w_down, ln2_w, ln2_b):
    layer_weights = (w_in, ret_decay_f, ret_decay_b, ret_gn_w, q_norm_w, w_uq, kv_norm_w, w_ukv,
                     w_o, ln1_w, ln1_b, w_up, w_down, ln2_w, ln2_b)
    return _trunks((x_prompt, x_sample), (c_prompt, c_sample), w_ada, b_ada, layer_weights)
```

```python
import functools
import math

import jax
import jax.numpy as jnp
from jax import lax
from jax.experimental import pallas as pl
from jax.experimental.pallas import tpu as pltpu

F32 = jnp.float32
BF16 = jnp.bfloat16

D_MODEL = 1024
RET_HEADS = 4
RET_HEAD_DIM = 128
RET_WIDTH = RET_HEADS * RET_HEAD_DIM
MLA_HEADS = 4
MLA_NOPE_DIM = 128
MLA_ROPE_DIM = 64
MLA_V_DIM = 128
MLA_WIDTH = MLA_HEADS * MLA_V_DIM
MLA_QK_DIM = MLA_NOPE_DIM + MLA_ROPE_DIM
Q_LORA = 384
KV_LORA = 256
D_FF = 4 * D_MODEL
ROPE_BASE = 10000.0
LN_EPS = 1e-5
RMS_EPS = 1e-6
N_MOD = 6
MLA_SCALE = float(MLA_QK_DIM ** -0.5)
LOG2E = 1.4426950408889634

V7X_LANES = 128
V7X_SUBLANES = 8
V7X_VMEM_BYTES = 64 * 1024 * 1024
VMEM_LIMIT = V7X_VMEM_BYTES - 8 * 1024 * 1024

QK_PAD = 2 * V7X_LANES
_OFF_RQ, _OFF_RV, _OFF_RG = 0, RET_WIDTH, 2 * RET_WIDTH
_OFF_CQ = 3 * RET_WIDTH
_OFF_CKV = _OFF_CQ + Q_LORA
_OFF_KR = _OFF_CKV + KV_LORA
_W_IN_COLS = _OFF_KR + V7X_LANES
ROWS_PAD = V7X_SUBLANES
RET_CHUNK = 128


def _tile(n, pref):
    t = min(n, pref)
    assert n % t == 0, (n, t)
    return t


def _const_spec(shape):
    nd = len(shape)
    return pl.BlockSpec(shape, lambda *_: (0,) * nd, pipeline_mode=pl.Buffered(1))


def _params(*sem):
    return pltpu.CompilerParams(dimension_semantics=sem, vmem_limit_bytes=VMEM_LIMIT)


def _nt_dot(a, b):
    return lax.dot_general(a, b, (((1,), (1,)), ((), ())), preferred_element_type=F32)


def _dot(a, b):
    return jnp.dot(a, b, preferred_element_type=F32)


def _ada_kernel(c_ref, w_ref, b_ref, o_ref):
    c = c_ref[...]
    s = c / (1.0 + jnp.exp(-c))
    o_ref[...] = _dot(s.astype(BF16), w_ref[...].astype(BF16)) + b_ref[...]


def _ada(c_pad, w_ada, b_ada):
    n = w_ada.shape[1]
    tn = _tile(n, D_MODEL)
    return pl.pallas_call(
        _ada_kernel,
        grid=(n // tn,),
        in_specs=[
            pl.BlockSpec((ROWS_PAD, D_MODEL), lambda j: (0, 0)),
            pl.BlockSpec((D_MODEL, tn), lambda j: (0, j)),
            pl.BlockSpec((1, tn), lambda j: (0, j)),
        ],
        out_specs=pl.BlockSpec((ROWS_PAD, tn), lambda j: (0, j)),
        out_shape=jax.ShapeDtypeStruct((ROWS_PAD, n), F32),
        compiler_params=_params("arbitrary"),
        name="ada",
    )(c_pad, w_ada, b_ada.reshape(1, n))


def _tables_kernel(inv_r_ref, sgn_r_ref, inv_m_ref, sgn_m_ref, inv_rt_ref, sgn_rt_ref,
                   cos_r_ref, sin_r_ref, cos_m_ref, sin_m_ref, cos_rt_ref, sin_rt_ref):
    tr = cos_r_ref.shape[0]
    base = pl.program_id(0) * tr
    pos = (lax.broadcasted_iota(jnp.int32, (tr, V7X_LANES), 0) + base).astype(F32)
    ang_r = pos * inv_r_ref[...]
    cos_r_ref[...] = jnp.cos(ang_r)
    sin_r_ref[...] = jnp.sin(ang_r) * sgn_r_ref[...]
    ang_m = pos * inv_m_ref[...]
    cos_m_ref[...] = jnp.cos(ang_m)
    sin_m_ref[...] = jnp.sin(ang_m) * sgn_m_ref[...]
    for c0 in range(0, tr, V7X_LANES):
        pos_t = (lax.broadcasted_iota(jnp.int32, (RET_HEAD_DIM, V7X_LANES), 1) + (base + c0)).astype(F32)
        ang_t = pos_t * inv_rt_ref[...]
        cos_rt_ref[:, c0:c0 + V7X_LANES] = jnp.cos(ang_t)
        sin_rt_ref[:, c0:c0 + V7X_LANES] = jnp.sin(ang_t) * sgn_rt_ref[...]


def _rotary_tables(seq):
    half_r = RET_HEAD_DIM // 2
    half_m = MLA_ROPE_DIM // 2
    inv_r = ROPE_BASE ** (-jnp.arange(0, RET_HEAD_DIM, 2, dtype=F32) / RET_HEAD_DIM)
    inv_m = ROPE_BASE ** (-jnp.arange(0, MLA_ROPE_DIM, 2, dtype=F32) / MLA_ROPE_DIM)
    inv_r_full = jnp.tile(inv_r, 2)
    sgn_r_full = jnp.concatenate([-jnp.ones(half_r, F32), jnp.ones(half_r, F32)])
    inv_m_full = jnp.tile(inv_m, V7X_LANES // half_m)
    sgn_m_full = jnp.tile(jnp.concatenate([-jnp.ones(half_m, F32), jnp.ones(half_m, F32)]),
                          V7X_LANES // MLA_ROPE_DIM)
    tr = _tile(seq, 1024)
    row = lambda v: v.reshape(1, V7X_LANES)
    col = lambda v: jnp.broadcast_to(v.reshape(RET_HEAD_DIM, 1), (RET_HEAD_DIM, V7X_LANES))
    tok = jax.ShapeDtypeStruct((seq, V7X_LANES), F32)
    feat = jax.ShapeDtypeStruct((RET_HEAD_DIM, seq), F32)
    vec = pl.BlockSpec((1, V7X_LANES), lambda i: (0, 0))
    cvec = pl.BlockSpec((RET_HEAD_DIM, V7X_LANES), lambda i: (0, 0))
    tok_spec = pl.BlockSpec((tr, V7X_LANES), lambda i: (i, 0))
    feat_spec = pl.BlockSpec((RET_HEAD_DIM, tr), lambda i: (0, i))
    return pl.pallas_call(
        _tables_kernel,
        grid=(seq // tr,),
        in_specs=[vec, vec, vec, vec, cvec, cvec],
        out_specs=[tok_spec, tok_spec, tok_spec, tok_spec, feat_spec, feat_spec],
        out_shape=[tok, tok, tok, tok, feat, feat],
        compiler_params=_params("arbitrary"),
        name="tables",
    )(row(inv_r_full), row(sgn_r_full), row(inv_m_full), row(sgn_m_full),
      col(inv_r_full), col(sgn_r_full))


def _layer_norm(x):
    mu = jnp.mean(x, axis=-1, keepdims=True)
    xc = x - mu
    var = jnp.mean(xc * xc, axis=-1, keepdims=True)
    return xc * lax.rsqrt(var + LN_EPS)


def _rms_norm(x, w):
    return x * lax.rsqrt(jnp.mean(x * x, axis=-1, keepdims=True) + RMS_EPS) * w


def _rope64(x, cos_m, sin_m, first_half):
    swapped = jnp.where(first_half,
                        pltpu.roll(x, V7X_LANES - MLA_ROPE_DIM // 2, 1),
                        pltpu.roll(x, MLA_ROPE_DIM // 2, 1))
    return x * cos_m + swapped * sin_m


def _proj_kernel(x_ref, mod_ref, cos_r_ref, sin_r_ref, cos_m_ref, sin_m_ref, cos_rt_ref, sin_rt_ref,
                 w_in_ref, w_kt_ref, qn_w_ref, w_uqn_ref, w_uqr_ref, kvn_w_ref, w_uk_ref, w_uvt_ref,
                 rq_ref, rkt_ref, rv_ref, rg_ref, q_ref, k_ref, vt_ref):
    x = x_ref[0]
    sh1 = mod_ref[0, 0:1, :]
    sc1 = mod_ref[0, 1:2, :]
    h = (_layer_norm(x) * (1.0 + sc1) + sh1).astype(BF16)
    proj = lambda lo, n: _dot(h, w_in_ref[:, lo:lo + n])
    hd = RET_HEAD_DIM

    cos_r = cos_r_ref[...]
    sin_r = sin_r_ref[...]
    rq = proj(_OFF_RQ, RET_WIDTH)
    for hh in range(RET_HEADS):
        t = rq[:, hh * hd:(hh + 1) * hd]
        rq_ref[0, :, hh * hd:(hh + 1) * hd] = (t * cos_r + pltpu.roll(t, hd // 2, 1) * sin_r).astype(BF16)

    cos_rt = cos_rt_ref[...]
    sin_rt = sin_rt_ref[...]
    rkt = _nt_dot(w_kt_ref[...], h)
    k_scale = RET_HEAD_DIM ** -0.5
    for hh in range(RET_HEADS):
        t = rkt[hh * hd:(hh + 1) * hd, :]
        rot = ((t * cos_rt + pltpu.roll(t, hd // 2, 0) * sin_rt) * k_scale).astype(BF16)
        for j in range(rkt_ref.shape[1]):
            rkt_ref[0, j, hh * hd:(hh + 1) * hd, :] = rot[:, j * RET_CHUNK:(j + 1) * RET_CHUNK]

    rv_ref[0] = proj(_OFF_RV, RET_WIDTH).astype(BF16)
    g = proj(_OFF_RG, RET_WIDTH)
    rg_ref[0] = (g / (1.0 + jnp.exp(-g))).astype(BF16)

    cos_m = cos_m_ref[...]
    sin_m = sin_m_ref[...]
    lane = lax.broadcasted_iota(jnp.int32, cos_m.shape, 1)
    first_half = (lane % MLA_ROPE_DIM) < (MLA_ROPE_DIM // 2)
    low_lanes = lane < MLA_ROPE_DIM
    q_scale = MLA_SCALE * LOG2E
    cq = _rms_norm(proj(_OFF_CQ, Q_LORA), qn_w_ref[...]).astype(BF16)
    q_nope = _dot(cq, w_uqn_ref[...]) * q_scale
    q_rope = _dot(cq, w_uqr_ref[...]) * q_scale
    for pair in range(MLA_HEADS // 2):
        r = _rope64(q_rope[:, pair * V7X_LANES:(pair + 1) * V7X_LANES], cos_m, sin_m, first_half)
        for sub in range(2):
            hh = 2 * pair + sub
            keep = low_lanes if sub == 0 else jnp.logical_not(low_lanes)
            q_ref[0, hh, :, 0:MLA_NOPE_DIM] = q_nope[:, hh * MLA_NOPE_DIM:(hh + 1) * MLA_NOPE_DIM].astype(BF16)
            q_ref[0, hh, :, MLA_NOPE_DIM:QK_PAD] = jnp.where(keep, r, 0.0).astype(BF16)

    ckv = _rms_norm(proj(_OFF_CKV, KV_LORA), kvn_w_ref[...]).astype(BF16)
    k_nope = _dot(ckv, w_uk_ref[...])
    v_t = _nt_dot(w_uvt_ref[...], ckv)
    k_pe = _rope64(proj(_OFF_KR, V7X_LANES), cos_m, sin_m, first_half)
    k_pe_hi = pltpu.roll(k_pe, MLA_ROPE_DIM, 1)
    for hh in range(MLA_HEADS):
        k_ref[0, hh, :, 0:MLA_NOPE_DIM] = k_nope[:, hh * MLA_NOPE_DIM:(hh + 1) * MLA_NOPE_DIM].astype(BF16)
        k_ref[0, hh, :, MLA_NOPE_DIM:QK_PAD] = (k_pe if hh % 2 == 0 else k_pe_hi).astype(BF16)
        vt_ref[0, hh] = v_t[hh * MLA_V_DIM:(hh + 1) * MLA_V_DIM, :].astype(BF16)


def _proj(x, mod, mod_row0, tables, w):
    b, s, _ = x.shape
    tm = _tile(s, 512)
    cos_r, sin_r, cos_m, sin_m, cos_rt, sin_rt = tables
    tok_tab = pl.BlockSpec((tm, V7X_LANES), lambda bi, i: (i, 0))
    feat_tab = pl.BlockSpec((RET_HEAD_DIM, tm), lambda bi, i: (0, i))
    tok_out = lambda: pl.BlockSpec((1, tm, RET_WIDTH), lambda bi, i: (bi, i, 0))
    out_shape = [
        jax.ShapeDtypeStruct((b, s, RET_WIDTH), BF16),
        jax.ShapeDtypeStruct((b, s // RET_CHUNK, RET_WIDTH, RET_CHUNK), BF16),
        jax.ShapeDtypeStruct((b, s, RET_WIDTH), BF16),
        jax.ShapeDtypeStruct((b, s, RET_WIDTH), BF16),
        jax.ShapeDtypeStruct((b, MLA_HEADS, s, QK_PAD), BF16),
        jax.ShapeDtypeStruct((b, MLA_HEADS, s, QK_PAD), BF16),
        jax.ShapeDtypeStruct((b, MLA_HEADS, MLA_V_DIM, s), BF16),
    ]
    out_specs = [
        tok_out(),
        pl.BlockSpec((1, tm // RET_CHUNK, RET_WIDTH, RET_CHUNK), lambda bi, i: (bi, i, 0, 0)),
        tok_out(),
        tok_out(),
        pl.BlockSpec((1, MLA_HEADS, tm, QK_PAD), lambda bi, i: (bi, 0, i, 0)),
        pl.BlockSpec((1, MLA_HEADS, tm, QK_PAD), lambda bi, i: (bi, 0, i, 0)),
        pl.BlockSpec((1, MLA_HEADS, MLA_V_DIM, tm), lambda bi, i: (bi, 0, 0, i)),
    ]
    in_specs = [
        pl.BlockSpec((1, tm, D_MODEL), lambda bi, i: (bi, i, 0)),
        pl.BlockSpec((1, N_MOD, D_MODEL), lambda bi, i: (bi + mod_row0, 0, 0)),
        tok_tab, tok_tab, tok_tab, tok_tab, feat_tab, feat_tab,
        _const_spec(w["w_in"].shape), _const_spec(w["w_kt"].shape),
        _const_spec(w["q_norm_w"].shape), _const_spec(w["w_uqn"].shape), _const_spec(w["w_uqr"].shape),
        _const_spec(w["kv_norm_w"].shape), _const_spec(w["w_uk"].shape), _const_spec(w["w_uvt"].shape),
    ]
    return pl.pallas_call(
        _proj_kernel,
        grid=(b, s // tm),
        in_specs=in_specs,
        out_specs=out_specs,
        out_shape=out_shape,
        compiler_params=_params("arbitrary", "arbitrary"),
        name="proj",
    )(x, mod, cos_r, sin_r, cos_m, sin_m, cos_rt, sin_rt,
      w["w_in"], w["w_kt"], w["q_norm_w"], w["w_uqn"], w["w_uqr"], w["kv_norm_w"], w["w_uk"], w["w_uvt"])


def _log_sigmoid(x):
    return jnp.minimum(x, 0.0) - jnp.log(1.0 + jnp.exp(-jnp.abs(x)))


def _ret_decay_tables(dec_row, reverse, d_in_ref, d_q_ref, d_k_ref, d_c_ref, hh):
    c = RET_CHUNK
    lg = _log_sigmoid(dec_row)[:, 0:1]
    row = lax.broadcasted_iota(jnp.int32, (c, c), 0)
    col = lax.broadcasted_iota(jnp.int32, (c, c), 1)
    rel = (col - row) if reverse else (row - col)
    mask = (rel > 0) if reverse else (rel >= 0)
    relf = jnp.maximum(rel, 0).astype(F32)
    d_in_ref[hh] = jnp.where(mask, jnp.exp(lg * relf), 0.0)
    qi = lax.broadcasted_iota(jnp.int32, (c, V7X_LANES), 0).astype(F32)
    kj = lax.broadcasted_iota(jnp.int32, (RET_HEAD_DIM, c), 1).astype(F32)
    q_steps = (c - qi) if reverse else (qi + 1.0)
    k_steps = kj if reverse else (c - 1.0 - kj)
    d_q_ref[hh] = jnp.exp(lg * q_steps)
    d_k_ref[hh] = jnp.exp(lg * k_steps)
    d_c_ref[hh] = jnp.exp(lg * float(c)) + jnp.zeros((V7X_SUBLANES, V7X_LANES), F32)


def _ret_kernel(reverse, *refs):
    if reverse:
        (dec_ref, rq_ref, rkt_ref, rv_ref, fwd_ref, rg_ref, gnw_ref, out_ref,
         state_ref, d_in_ref, d_q_ref, d_k_ref, d_c_ref) = refs
    else:
        (dec_ref, rq_ref, rkt_ref, rv_ref, out_ref,
         state_ref, d_in_ref, d_q_ref, d_k_ref, d_c_ref) = refs
    c = RET_CHUNK
    hd = RET_HEAD_DIM
    n_chunks = rq_ref.shape[1] // c

    @pl.when(pl.program_id(1) == 0)
    def _():
        state_ref[...] = jnp.zeros_like(state_ref)
        for hh in range(RET_HEADS):
            _ret_decay_tables(dec_ref[hh:hh + 1, :], reverse, d_in_ref, d_q_ref, d_k_ref, d_c_ref, hh)

    def chunk_body(ci, carry):
        cidx = (n_chunks - 1 - ci) if reverse else ci
        t0 = pl.multiple_of(cidx * c, c)
        for hh in range(RET_HEADS):
            cols = slice(hh * hd, (hh + 1) * hd)
            q = rq_ref[0, pl.ds(t0, c), cols]
            kt = rkt_ref[0, cidx, cols, :]
            v = rv_ref[0, pl.ds(t0, c), cols]
            state = state_ref[hh]
            s = _dot(q, kt) * d_in_ref[hh]
            inner = _dot(s.astype(BF16), v)
            cross = _dot(q, state.astype(BF16)) * d_q_ref[hh]
            kt_dec = (kt.astype(F32) * d_k_ref[hh]).astype(BF16)
            state_ref[hh] = state * d_c_ref[hh][0:1, :] + _dot(kt_dec, v)
            y = inner + cross
            if reverse:
                y = y + fwd_ref[0, pl.ds(t0, c), cols]
                mu = jnp.mean(y, axis=-1, keepdims=True)
                yc = y - mu
                var = jnp.mean(yc * yc, axis=-1, keepdims=True)
                yn = yc * lax.rsqrt(var + LN_EPS) * gnw_ref[:, cols]
                gate = rg_ref[0, pl.ds(t0, c), cols].astype(F32)
                out_ref[0, pl.ds(t0, c), cols] = (gate * yn).astype(out_ref.dtype)
            else:
                out_ref[0, pl.ds(t0, c), cols] = y
        return carry

    lax.fori_loop(0, n_chunks, chunk_body, 0)


def _retention(dec_rows, rq, rkt, rv, reverse, fwd=None, rg=None, gn_w=None):
    b, s, _ = rq.shape
    tb = _tile(s, 1024)
    nb = s // tb
    blk = (lambda i: nb - 1 - i) if reverse else (lambda i: i)
    tok = pl.BlockSpec((1, tb, RET_WIDTH), lambda bi, i: (bi, blk(i), 0))
    feat = pl.BlockSpec((1, tb // RET_CHUNK, RET_WIDTH, RET_CHUNK), lambda bi, i: (bi, blk(i), 0, 0))
    dec_spec = pl.BlockSpec((RET_HEADS, V7X_LANES), lambda bi, i: (0, 0))
    in_specs = [dec_spec, tok, feat, tok]
    args = [dec_rows, rq, rkt, rv]
    if reverse:
        in_specs += [tok, tok, pl.BlockSpec((1, RET_WIDTH), lambda bi, i: (0, 0))]
        args += [fwd, rg, gn_w]
    c = RET_CHUNK
    scratch = [
        pltpu.VMEM((RET_HEADS, RET_HEAD_DIM, RET_HEAD_DIM), F32),
        pltpu.VMEM((RET_HEADS, c, c), F32),
        pltpu.VMEM((RET_HEADS, c, V7X_LANES), F32),
        pltpu.VMEM((RET_HEADS, RET_HEAD_DIM, c), F32),
        pltpu.VMEM((RET_HEADS, V7X_SUBLANES, V7X_LANES), F32),
    ]
    return pl.pallas_call(
        functools.partial(_ret_kernel, reverse),
        grid=(b, nb),
        in_specs=in_specs,
        out_specs=tok,
        out_shape=jax.ShapeDtypeStruct((b, s, RET_WIDTH), BF16 if reverse else F32),
        scratch_shapes=scratch,
        compiler_params=_params("arbitrary", "arbitrary"),
        name="ret_bwd" if reverse else "ret_fwd",
    )(*args)


NEG_BIG = -1e30


ATTN_GROUP = 2 * V7X_LANES
ATTN_LAG = 3


def _attn_kernel(q_ref, k_ref, vprev_ref, vcur_ref, o_ref, s_ref, m_old_ref, m_cur_ref, l_ref, acc_ref):
    t = pl.program_id(3)
    nk = pl.num_programs(3) - 1
    gw = ATTN_GROUP
    ng = q_ref.shape[2] // gw
    lag = min(ATTN_LAG, ng - 1)
    first_carried = ng - lag
    group = lambda g: slice(g * gw, (g + 1) * gw)

    def softmax_pv(g, vt, m_prev, m_new):
        cols = group(g)
        alpha = jnp.exp2(m_prev - m_new)
        p = jnp.exp2(s_ref[:, cols] - m_new)
        l_ref[:, cols] = alpha * l_ref[:, cols] + jnp.sum(p, axis=0, keepdims=True)
        acc_ref[:, cols] = alpha * acc_ref[:, cols] + _dot(vt, p.astype(BF16))

    def scores(g, m_prev):
        cols = group(g)
        s = _nt_dot(k_ref[0, 0], q_ref[0, 0, cols, :])
        s_ref[:, cols] = s
        return jnp.maximum(m_prev, jnp.max(s, axis=0, keepdims=True))

    def carried_pv(g):
        cols = group(g)
        softmax_pv(g, vprev_ref[0, 0], m_old_ref[:, cols], m_cur_ref[:, cols])

    def step(has_prev):
        m_before, m_after = {}, {}
        for j in range(ng):
            cols = group(j)
            m_before[j] = m_cur_ref[:, cols]
            m_after[j] = scores(j, m_before[j])
            if j >= first_carried:
                m_old_ref[:, cols] = m_before[j]
                m_cur_ref[:, cols] = m_after[j]
            d = j - lag
            if d >= 0:
                softmax_pv(d, vcur_ref[0, 0], m_before[d], m_after[d])
                m_cur_ref[:, group(d)] = m_after[d]
            elif has_prev:
                carried_pv(ng + d)

    @pl.when(t == 0)
    def _():
        m_cur_ref[...] = jnp.full_like(m_cur_ref, NEG_BIG)
        l_ref[...] = jnp.zeros_like(l_ref)
        acc_ref[...] = jnp.zeros_like(acc_ref)
        step(False)

    @pl.when(jnp.logical_and(t > 0, t < nk))
    def _():
        step(True)

    @pl.when(t == nk)
    def _():
        for g in range(first_carried, ng):
            carried_pv(g)
        o = acc_ref[...] / l_ref[...]
        o_ref[0] = o.T.astype(o_ref.dtype)


def _attention(q, k, vt):
    b, nh, s, _ = q.shape
    tq = _tile(s, 2048)
    tk = _tile(s, 2048)
    nk = s // tk
    return pl.pallas_call(
        _attn_kernel,
        grid=(b, nh, s // tq, nk + 1),
        in_specs=[
            pl.BlockSpec((1, 1, tq, QK_PAD), lambda bi, hi, qi, t: (bi, hi, qi, 0)),
            pl.BlockSpec((1, 1, tk, QK_PAD), lambda bi, hi, qi, t: (bi, hi, jnp.minimum(t, nk - 1), 0)),
            pl.BlockSpec((1, 1, MLA_V_DIM, tk), lambda bi, hi, qi, t: (bi, hi, 0, jnp.maximum(t - 1, 0))),
            pl.BlockSpec((1, 1, MLA_V_DIM, tk), lambda bi, hi, qi, t: (bi, hi, 0, jnp.minimum(t, nk - 1))),
        ],
        out_specs=pl.BlockSpec((1, tq, MLA_V_DIM), lambda bi, hi, qi, t: (bi, qi, hi)),
        out_shape=jax.ShapeDtypeStruct((b, s, MLA_WIDTH), BF16),
        scratch_shapes=[
            pltpu.VMEM((tk, tq), F32),
            pltpu.VMEM((1, tq), F32),
            pltpu.VMEM((1, tq), F32),
            pltpu.VMEM((1, tq), F32),
            pltpu.VMEM((MLA_V_DIM, tq), F32),
        ],
        compiler_params=_params("arbitrary", "arbitrary", "arbitrary", "arbitrary"),
        name="attn",
    )(q, k, vt, vt)


FF_CHUNK = 1024


def _mlp_kernel(alpha, x_ref, ret_ref, att_ref, mod_ref, w_o_ref, ln1_w_ref, ln1_b_ref,
                w_up_ref, w_down_ref, ln2_w_ref, ln2_b_ref, o_ref):
    x = x_ref[0]
    g1 = mod_ref[0, 2:3, :]
    sh2 = mod_ref[0, 3:4, :]
    sc2 = mod_ref[0, 4:5, :]
    g2 = mod_ref[0, 5:6, :]
    mix = _dot(ret_ref[0], w_o_ref[0:RET_WIDTH, :]) + _dot(att_ref[0], w_o_ref[RET_WIDTH:, :])
    x1 = _layer_norm(alpha * x + g1 * mix) * ln1_w_ref[...] + ln1_b_ref[...]
    h = (_layer_norm(x1) * (1.0 + sc2) + sh2).astype(BF16)
    ff = jnp.zeros_like(x1)
    for c0 in range(0, D_FF, FF_CHUNK):
        u = jnp.maximum(_dot(h, w_up_ref[:, c0:c0 + FF_CHUNK]), 0.0)
        ff = ff + _dot((u * u).astype(BF16), w_down_ref[c0:c0 + FF_CHUNK, :])
    o_ref[0] = _layer_norm(alpha * x1 + g2 * ff) * ln2_w_ref[...] + ln2_b_ref[...]


def _mlp(alpha, x, ret, att, mod, mod_row0, w):
    b, s, _ = x.shape
    tm = _tile(s, 512)
    xs = pl.BlockSpec((1, tm, D_MODEL), lambda bi, i: (bi, i, 0))
    half = pl.BlockSpec((1, tm, RET_WIDTH), lambda bi, i: (bi, i, 0))
    vec = lambda: pl.BlockSpec((1, D_MODEL), lambda bi, i: (0, 0))
    return pl.pallas_call(
        functools.partial(_mlp_kernel, alpha),
        grid=(b, s // tm),
        in_specs=[
            xs, half, half,
            pl.BlockSpec((1, N_MOD, D_MODEL), lambda bi, i: (bi + mod_row0, 0, 0)),
            _const_spec(w["w_o"].shape), vec(), vec(),
            _const_spec(w["w_up"].shape), _const_spec(w["w_down"].shape), vec(), vec(),
        ],
        out_specs=xs,
        out_shape=jax.ShapeDtypeStruct((b, s, D_MODEL), F32),
        compiler_params=_params("arbitrary", "arbitrary"),
        name="mlp",
    )(x, ret, att, mod, w["w_o"], w["ln1_w"], w["ln1_b"], w["w_up"], w["w_down"], w["ln2_w"], w["ln2_b"])


def _prep_layer_weights(w_in, ret_decay_f, ret_decay_b, ret_gn_w, q_norm_w, w_uq, kv_norm_w, w_ukv,
                        w_o, ln1_w, ln1_b, w_up, w_down, ln2_w, ln2_b):
    rw = RET_WIDTH
    w_rq, w_rk, w_rest = w_in[:, :rw], w_in[:, rw:2 * rw], w_in[:, 2 * rw:]
    pad = jnp.zeros((D_MODEL, _W_IN_COLS - (w_in.shape[1] - rw)), w_in.dtype)
    w_in_p = jnp.concatenate([w_rq, w_rest, pad], axis=1).astype(BF16)
    uq = w_uq.reshape(Q_LORA, MLA_HEADS, MLA_QK_DIM)
    ukv = w_ukv.reshape(KV_LORA, MLA_HEADS, MLA_NOPE_DIM + MLA_V_DIM)
    lanes = lambda v: jnp.broadcast_to(v.reshape(RET_HEADS, 1), (RET_HEADS, V7X_LANES)).astype(F32)
    return {
        "w_in": w_in_p,
        "w_kt": w_rk.T.astype(BF16),
        "q_norm_w": q_norm_w.reshape(1, Q_LORA),
        "w_uqn": uq[:, :, :MLA_NOPE_DIM].reshape(Q_LORA, MLA_HEADS * MLA_NOPE_DIM).astype(BF16),
        "w_uqr": uq[:, :, MLA_NOPE_DIM:].reshape(Q_LORA, MLA_HEADS * MLA_ROPE_DIM).astype(BF16),
        "kv_norm_w": kv_norm_w.reshape(1, KV_LORA),
        "w_uk": ukv[:, :, :MLA_NOPE_DIM].reshape(KV_LORA, MLA_HEADS * MLA_NOPE_DIM).astype(BF16),
        "w_uvt": ukv[:, :, MLA_NOPE_DIM:].reshape(KV_LORA, MLA_HEADS * MLA_V_DIM).T.astype(BF16),
        "dec_f": lanes(ret_decay_f),
        "dec_b": lanes(ret_decay_b),
        "gn_w": ret_gn_w.reshape(1, RET_WIDTH),
        "w_o": w_o.astype(BF16),
        "ln1_w": ln1_w.reshape(1, D_MODEL), "ln1_b": ln1_b.reshape(1, D_MODEL),
        "w_up": w_up.astype(BF16), "w_down": w_down.astype(BF16),
        "ln2_w": ln2_w.reshape(1, D_MODEL), "ln2_b": ln2_b.reshape(1, D_MODEL),
    }


def _layer(alpha, x, mod, mod_row0, tables, w):
    rq, rkt, rv, rg, q, k, vt = _proj(x, mod, mod_row0, tables, w)
    fwd = _retention(w["dec_f"], rq, rkt, rv, reverse=False)
    ret = _retention(w["dec_b"], rq, rkt, rv, reverse=True, fwd=fwd, rg=rg, gn_w=w["gn_w"])
    att = _attention(q, k, vt)
    return _mlp(alpha, x, ret, att, mod, mod_row0, w)


def _trunks(xs, cs, w_ada, b_ada, layer_weights):
    depth = w_ada.shape[0]
    alpha = float((2 * depth) ** 0.25)
    n_rows = sum(c.shape[0] for c in cs)
    assert n_rows <= ROWS_PAD
    c_pad = jnp.concatenate(list(cs) + [jnp.zeros((ROWS_PAD - n_rows, D_MODEL), F32)], axis=0)
    tables = _rotary_tables(max(x.shape[1] for x in xs))
    outs = list(xs)
    for l in range(depth):
        mod = _ada(c_pad, w_ada[l], b_ada[l]).reshape(ROWS_PAD, N_MOD, D_MODEL)
        w = _prep_layer_weights(*[t[l] for t in layer_weights])
        row0 = 0
        for i, c in enumerate(cs):
            outs[i] = _layer(alpha, outs[i], mod, row0, tables, w)
            row0 += c.shape[0]
    return tuple(outs)


def kernel(x_prompt, x_sample, c_prompt, c_sample, w_ada, b_ada, w_in, ret_decay_f, ret_decay_b, ret_gn_w,
           q_norm_w, w_uq, kv_norm_w, w_ukv, w_o, ln1_w, ln1_b, w_up, w_down, ln2_w, ln2_b):
    layer_weights = (w_in, ret_decay_f, ret_decay_b, ret_gn_w, q_norm_w, w_uq, kv_norm_w, w_ukv,
                     w_o, ln1_w, ln1_b, w_up, w_down, ln2_w, ln2_b)
    return _trunks((x_prompt, x_sample), (c_prompt, c_sample), w_ada, b_ada, layer_weights)
```

```python
import functools
import math

import jax
import jax.numpy as jnp
from jax import lax
from jax.experimental import pallas as pl
from jax.experimental.pallas import tpu as pltpu

F32 = jnp.float32
BF16 = jnp.bfloat16

D_MODEL = 1024
RET_HEADS = 4
RET_HEAD_DIM = 128
RET_WIDTH = RET_HEADS * RET_HEAD_DIM
MLA_HEADS = 4
MLA_NOPE_DIM = 128
MLA_ROPE_DIM = 64
MLA_V_DIM = 128
MLA_WIDTH = MLA_HEADS * MLA_V_DIM
MLA_QK_DIM = MLA_NOPE_DIM + MLA_ROPE_DIM
Q_LORA = 384
KV_LORA = 256
D_FF = 4 * D_MODEL
ROPE_BASE = 10000.0
LN_EPS = 1e-5
RMS_EPS = 1e-6
N_MOD = 6
MLA_SCALE = float(MLA_QK_DIM ** -0.5)
LOG2E = 1.4426950408889634

V7X_LANES = 128
V7X_SUBLANES = 8
V7X_VMEM_BYTES = 64 * 1024 * 1024
VMEM_LIMIT = V7X_VMEM_BYTES - 8 * 1024 * 1024

QK_PAD = 2 * V7X_LANES
_OFF_RQ, _OFF_RV, _OFF_RG = 0, RET_WIDTH, 2 * RET_WIDTH
_OFF_CQ = 3 * RET_WIDTH
_OFF_CKV = _OFF_CQ + Q_LORA
_OFF_KR = _OFF_CKV + KV_LORA
_W_IN_COLS = _OFF_KR + V7X_LANES
ROWS_PAD = V7X_SUBLANES
RET_CHUNK = 128


def _tile(n, pref):
    t = min(n, pref)
    assert n % t == 0, (n, t)
    return t


def _const_spec(shape):
    nd = len(shape)
    return pl.BlockSpec(shape, lambda *_: (0,) * nd, pipeline_mode=pl.Buffered(1))


def _params(*sem):
    return pltpu.CompilerParams(dimension_semantics=sem, vmem_limit_bytes=VMEM_LIMIT)


def _nt_dot(a, b):
    return lax.dot_general(a, b, (((1,), (1,)), ((), ())), preferred_element_type=F32)


def _dot(a, b):
    return jnp.dot(a, b, preferred_element_type=F32)


def _ada_kernel(c_ref, w_ref, b_ref, o_ref):
    c = c_ref[...]
    s = c / (1.0 + jnp.exp(-c))
    o_ref[...] = _dot(s.astype(BF16), w_ref[...].astype(BF16)) + b_ref[...]


def _ada(c_pad, w_ada, b_ada):
    n = w_ada.shape[1]
    tn = _tile(n, D_MODEL)
    return pl.pallas_call(
        _ada_kernel,
        grid=(n // tn,),
        in_specs=[
            pl.BlockSpec((ROWS_PAD, D_MODEL), lambda j: (0, 0)),
            pl.BlockSpec((D_MODEL, tn), lambda j: (0, j)),
            pl.BlockSpec((1, tn), lambda j: (0, j)),
        ],
        out_specs=pl.BlockSpec((ROWS_PAD, tn), lambda j: (0, j)),
        out_shape=jax.ShapeDtypeStruct((ROWS_PAD, n), F32),
        compiler_params=_params("arbitrary"),
        name="ada",
    )(c_pad, w_ada, b_ada.reshape(1, n))


def _tables_kernel(inv_r_ref, sgn_r_ref, inv_m_ref, sgn_m_ref, inv_rt_ref, sgn_rt_ref,
                   cos_r_ref, sin_r_ref, cos_m_ref, sin_m_ref, cos_rt_ref, sin_rt_ref):
    tr = cos_r_ref.shape[0]
    base = pl.program_id(0) * tr
    pos = (lax.broadcasted_iota(jnp.int32, (tr, V7X_LANES), 0) + base).astype(F32)
    ang_r = pos * inv_r_ref[...]
    cos_r_ref[...] = jnp.cos(ang_r)
    sin_r_ref[...] = jnp.sin(ang_r) * sgn_r_ref[...]
    ang_m = pos * inv_m_ref[...]
    cos_m_ref[...] = jnp.cos(ang_m)
    sin_m_ref[...] = jnp.sin(ang_m) * sgn_m_ref[...]
    for c0 in range(0, tr, V7X_LANES):
        pos_t = (lax.broadcasted_iota(jnp.int32, (RET_HEAD_DIM, V7X_LANES), 1) + (base + c0)).astype(F32)
        ang_t = pos_t * inv_rt_ref[...]
        cos_rt_ref[:, c0:c0 + V7X_LANES] = jnp.cos(ang_t)
        sin_rt_ref[:, c0:c0 + V7X_LANES] = jnp.sin(ang_t) * sgn_rt_ref[...]


def _rotary_tables(seq):
    half_r = RET_HEAD_DIM // 2
    half_m = MLA_ROPE_DIM // 2
    inv_r = ROPE_BASE ** (-jnp.arange(0, RET_HEAD_DIM, 2, dtype=F32) / RET_HEAD_DIM)
    inv_m = ROPE_BASE ** (-jnp.arange(0, MLA_ROPE_DIM, 2, dtype=F32) / MLA_ROPE_DIM)
    inv_r_full = jnp.tile(inv_r, 2)
    sgn_r_full = jnp.concatenate([-jnp.ones(half_r, F32), jnp.ones(half_r, F32)])
    inv_m_full = jnp.tile(inv_m, V7X_LANES // half_m)
    sgn_m_full = jnp.tile(jnp.concatenate([-jnp.ones(half_m, F32), jnp.ones(half_m, F32)]),
                          V7X_LANES // MLA_ROPE_DIM)
    tr = _tile(seq, 1024)
    row = lambda v: v.reshape(1, V7X_LANES)
    col = lambda v: jnp.broadcast_to(v.reshape(RET_HEAD_DIM, 1), (RET_HEAD_DIM, V7X_LANES))
    tok = jax.ShapeDtypeStruct((seq, V7X_LANES), F32)
    feat = jax.ShapeDtypeStruct((RET_HEAD_DIM, seq), F32)
    vec = pl.BlockSpec((1, V7X_LANES), lambda i: (0, 0))
    cvec = pl.BlockSpec((RET_HEAD_DIM, V7X_LANES), lambda i: (0, 0))
    tok_spec = pl.BlockSpec((tr, V7X_LANES), lambda i: (i, 0))
    feat_spec = pl.BlockSpec((RET_HEAD_DIM, tr), lambda i: (0, i))
    return pl.pallas_call(
        _tables_kernel,
        grid=(seq // tr,),
        in_specs=[vec, vec, vec, vec, cvec, cvec],
        out_specs=[tok_spec, tok_spec, tok_spec, tok_spec, feat_spec, feat_spec],
        out_shape=[tok, tok, tok, tok, feat, feat],
        compiler_params=_params("arbitrary"),
        name="tables",
    )(row(inv_r_full), row(sgn_r_full), row(inv_m_full), row(sgn_m_full),
      col(inv_r_full), col(sgn_r_full))


def _layer_norm(x):
    mu = jnp.mean(x, axis=-1, keepdims=True)
    xc = x - mu
    var = jnp.mean(xc * xc, axis=-1, keepdims=True)
    return xc * lax.rsqrt(var + LN_EPS)


def _rms_norm(x, w):
    return x * lax.rsqrt(jnp.mean(x * x, axis=-1, keepdims=True) + RMS_EPS) * w


def _rope64(x, cos_m, sin_m, first_half):
    swapped = jnp.where(first_half,
                        pltpu.roll(x, V7X_LANES - MLA_ROPE_DIM // 2, 1),
                        pltpu.roll(x, MLA_ROPE_DIM // 2, 1))
    return x * cos_m + swapped * sin_m


def _proj_kernel(x_ref, mod_ref, cos_r_ref, sin_r_ref, cos_m_ref, sin_m_ref, cos_rt_ref, sin_rt_ref,
                 w_in_ref, w_kt_ref, qn_w_ref, w_uqn_ref, w_uqr_ref, kvn_w_ref, w_uk_ref, w_uvt_ref,
                 rq_ref, rkt_ref, rv_ref, rg_ref, q_ref, k_ref, vt_ref):
    x = x_ref[0]
    sh1 = mod_ref[0, 0:1, :]
    sc1 = mod_ref[0, 1:2, :]
    h = (_layer_norm(x) * (1.0 + sc1) + sh1).astype(BF16)
    proj = lambda lo, n: _dot(h, w_in_ref[:, lo:lo + n])
    hd = RET_HEAD_DIM

    cos_r = cos_r_ref[...]
    sin_r = sin_r_ref[...]
    rq = proj(_OFF_RQ, RET_WIDTH)
    for hh in range(RET_HEADS):
        t = rq[:, hh * hd:(hh + 1) * hd]
        rq_ref[0, :, hh * hd:(hh + 1) * hd] = (t * cos_r + pltpu.roll(t, hd // 2, 1) * sin_r).astype(BF16)

    cos_rt = cos_rt_ref[...]
    sin_rt = sin_rt_ref[...]
    rkt = _nt_dot(w_kt_ref[...], h)
    k_scale = RET_HEAD_DIM ** -0.5
    for hh in range(RET_HEADS):
        t = rkt[hh * hd:(hh + 1) * hd, :]
        rot = ((t * cos_rt + pltpu.roll(t, hd // 2, 0) * sin_rt) * k_scale).astype(BF16)
        for j in range(rkt_ref.shape[1]):
            rkt_ref[0, j, hh * hd:(hh + 1) * hd, :] = rot[:, j * RET_CHUNK:(j + 1) * RET_CHUNK]

    rv_ref[0] = proj(_OFF_RV, RET_WIDTH).astype(BF16)
    g = proj(_OFF_RG, RET_WIDTH)
    rg_ref[0] = (g / (1.0 + jnp.exp(-g))).astype(BF16)

    cos_m = cos_m_ref[...]
    sin_m = sin_m_ref[...]
    lane = lax.broadcasted_iota(jnp.int32, cos_m.shape, 1)
    first_half = (lane % MLA_ROPE_DIM) < (MLA_ROPE_DIM // 2)
    low_lanes = lane < MLA_ROPE_DIM
    q_scale = MLA_SCALE * LOG2E
    cq = _rms_norm(proj(_OFF_CQ, Q_LORA), qn_w_ref[...]).astype(BF16)
    q_nope = _dot(cq, w_uqn_ref[...]) * q_scale
    q_rope = _dot(cq, w_uqr_ref[...]) * q_scale
    for pair in range(MLA_HEADS // 2):
        r = _rope64(q_rope[:, pair * V7X_LANES:(pair + 1) * V7X_LANES], cos_m, sin_m, first_half)
        for sub in range(2):
            hh = 2 * pair + sub
            keep = low_lanes if sub == 0 else jnp.logical_not(low_lanes)
            q_ref[0, hh, :, 0:MLA_NOPE_DIM] = q_nope[:, hh * MLA_NOPE_DIM:(hh + 1) * MLA_NOPE_DIM].astype(BF16)
            q_ref[0, hh, :, MLA_NOPE_DIM:QK_PAD] = jnp.where(keep, r, 0.0).astype(BF16)

    ckv = _rms_norm(proj(_OFF_CKV, KV_LORA), kvn_w_ref[...]).astype(BF16)
    k_nope = _dot(ckv, w_uk_ref[...])
    v_t = _nt_dot(w_uvt_ref[...], ckv)
    k_pe = _rope64(proj(_OFF_KR, V7X_LANES), cos_m, sin_m, first_half)
    k_pe_hi = pltpu.roll(k_pe, MLA_ROPE_DIM, 1)
    for hh in range(MLA_HEADS):
        k_ref[0, hh, :, 0:MLA_NOPE_DIM] = k_nope[:, hh * MLA_NOPE_DIM:(hh + 1) * MLA_NOPE_DIM].astype(BF16)
        k_ref[0, hh, :, MLA_NOPE_DIM:QK_PAD] = (k_pe if hh % 2 == 0 else k_pe_hi).astype(BF16)
        vt_ref[0, hh] = v_t[hh * MLA_V_DIM:(hh + 1) * MLA_V_DIM, :].astype(BF16)


def _proj(x, mod, mod_row0, tables, w):
    b, s, _ = x.shape
    tm = _tile(s, 512)
    cos_r, sin_r, cos_m, sin_m, cos_rt, sin_rt = tables
    tok_tab = pl.BlockSpec((tm, V7X_LANES), lambda bi, i: (i, 0))
    feat_tab = pl.BlockSpec((RET_HEAD_DIM, tm), lambda bi, i: (0, i))
    tok_out = lambda: pl.BlockSpec((1, tm, RET_WIDTH), lambda bi, i: (bi, i, 0))
    out_shape = [
        jax.ShapeDtypeStruct((b, s, RET_WIDTH), BF16),
        jax.ShapeDtypeStruct((b, s // RET_CHUNK, RET_WIDTH, RET_CHUNK), BF16),
        jax.ShapeDtypeStruct((b, s, RET_WIDTH), BF16),
        jax.ShapeDtypeStruct((b, s, RET_WIDTH), BF16),
        jax.ShapeDtypeStruct((b, MLA_HEADS, s, QK_PAD), BF16),
        jax.ShapeDtypeStruct((b, MLA_HEADS, s, QK_PAD), BF16),
        jax.ShapeDtypeStruct((b, MLA_HEADS, MLA_V_DIM, s), BF16),
    ]
    out_specs = [
        tok_out(),
        pl.BlockSpec((1, tm // RET_CHUNK, RET_WIDTH, RET_CHUNK), lambda bi, i: (bi, i, 0, 0)),
        tok_out(),
        tok_out(),
        pl.BlockSpec((1, MLA_HEADS, tm, QK_PAD), lambda bi, i: (bi, 0, i, 0)),
        pl.BlockSpec((1, MLA_HEADS, tm, QK_PAD), lambda bi, i: (bi, 0, i, 0)),
        pl.BlockSpec((1, MLA_HEADS, MLA_V_DIM, tm), lambda bi, i: (bi, 0, 0, i)),
    ]
    in_specs = [
        pl.BlockSpec((1, tm, D_MODEL), lambda bi, i: (bi, i, 0)),
        pl.BlockSpec((1, N_MOD, D_MODEL), lambda bi, i: (bi + mod_row0, 0, 0)),
        tok_tab, tok_tab, tok_tab, tok_tab, feat_tab, feat_tab,
        _const_spec(w["w_in"].shape), _const_spec(w["w_kt"].shape),
        _const_spec(w["q_norm_w"].shape), _const_spec(w["w_uqn"].shape), _const_spec(w["w_uqr"].shape),
        _const_spec(w["kv_norm_w"].shape), _const_spec(w["w_uk"].shape), _const_spec(w["w_uvt"].shape),
    ]
    return pl.pallas_call(
        _proj_kernel,
        grid=(b, s // tm),
        in_specs=in_specs,
        out_specs=out_specs,
        out_shape=out_shape,
        compiler_params=_params("arbitrary", "arbitrary"),
        name="proj",
    )(x, mod, cos_r, sin_r, cos_m, sin_m, cos_rt, sin_rt,
      w["w_in"], w["w_kt"], w["q_norm_w"], w["w_uqn"], w["w_uqr"], w["kv_norm_w"], w["w_uk"], w["w_uvt"])


def _log_sigmoid(x):
    return jnp.minimum(x, 0.0) - jnp.log(1.0 + jnp.exp(-jnp.abs(x)))


def _ret_decay_tables(dec_row, reverse, d_in_ref, d_q_ref, d_k_ref, d_c_ref, hh):
    c = RET_CHUNK
    lg = _log_sigmoid(dec_row)[:, 0:1]
    row = lax.broadcasted_iota(jnp.int32, (c, c), 0)
    col = lax.broadcasted_iota(jnp.int32, (c, c), 1)
    rel = (col - row) if reverse else (row - col)
    mask = (rel > 0) if reverse else (rel >= 0)
    relf = jnp.maximum(rel, 0).astype(F32)
    d_in_ref[hh] = jnp.where(mask, jnp.exp(lg * relf), 0.0)
    qi = lax.broadcasted_iota(jnp.int32, (c, V7X_LANES), 0).astype(F32)
    kj = lax.broadcasted_iota(jnp.int32, (RET_HEAD_DIM, c), 1).astype(F32)
    q_steps = (c - qi) if reverse else (qi + 1.0)
    k_steps = kj if reverse else (c - 1.0 - kj)
    d_q_ref[hh] = jnp.exp(lg * q_steps)
    d_k_ref[hh] = jnp.exp(lg * k_steps)
    d_c_ref[hh] = jnp.exp(lg * float(c)) + jnp.zeros((V7X_SUBLANES, V7X_LANES), F32)


def _ret_kernel(reverse, *refs):
    if reverse:
        (dec_ref, rq_ref, rkt_ref, rv_ref, fwd_ref, rg_ref, gnw_ref, out_ref,
         state_ref, d_in_ref, d_q_ref, d_k_ref, d_c_ref) = refs
    else:
        (dec_ref, rq_ref, rkt_ref, rv_ref, out_ref,
         state_ref, d_in_ref, d_q_ref, d_k_ref, d_c_ref) = refs
    c = RET_CHUNK
    hd = RET_HEAD_DIM
    n_chunks = rq_ref.shape[1] // c

    @pl.when(pl.program_id(1) == 0)
    def _():
        state_ref[...] = jnp.zeros_like(state_ref)
        for hh in range(RET_HEADS):
            _ret_decay_tables(dec_ref[hh:hh + 1, :], reverse, d_in_ref, d_q_ref, d_k_ref, d_c_ref, hh)

    def chunk_body(ci, carry):
        cidx = (n_chunks - 1 - ci) if reverse else ci
        t0 = pl.multiple_of(cidx * c, c)
        for hh in range(RET_HEADS):
            cols = slice(hh * hd, (hh + 1) * hd)
            q = rq_ref[0, pl.ds(t0, c), cols]
            kt = rkt_ref[0, cidx, cols, :]
            v = rv_ref[0, pl.ds(t0, c), cols]
            state = state_ref[hh]
            s = _dot(q, kt) * d_in_ref[hh]
            inner = _dot(s.astype(BF16), v)
            cross = _dot(q, state.astype(BF16)) * d_q_ref[hh]
            kt_dec = (kt.astype(F32) * d_k_ref[hh]).astype(BF16)
            state_ref[hh] = state * d_c_ref[hh][0:1, :] + _dot(kt_dec, v)
            y = inner + cross
            if reverse:
                y = y + fwd_ref[0, pl.ds(t0, c), cols]
                mu = jnp.mean(y, axis=-1, keepdims=True)
                yc = y - mu
                var = jnp.mean(yc * yc, axis=-1, keepdims=True)
                yn = yc * lax.rsqrt(var + LN_EPS) * gnw_ref[:, cols]
                gate = rg_ref[0, pl.ds(t0, c), cols].astype(F32)
                out_ref[0, pl.ds(t0, c), cols] = (gate * yn).astype(out_ref.dtype)
            else:
                out_ref[0, pl.ds(t0, c), cols] = y
        return carry

    lax.fori_loop(0, n_chunks, chunk_body, 0, unroll=True)


def _retention(dec_rows, rq, rkt, rv, reverse, fwd=None, rg=None, gn_w=None):
    b, s, _ = rq.shape
    tb = _tile(s, 1024)
    nb = s // tb
    blk = (lambda i: nb - 1 - i) if reverse else (lambda i: i)
    tok = pl.BlockSpec((1, tb, RET_WIDTH), lambda bi, i: (bi, blk(i), 0))
    feat = pl.BlockSpec((1, tb // RET_CHUNK, RET_WIDTH, RET_CHUNK), lambda bi, i: (bi, blk(i), 0, 0))
    dec_spec = pl.BlockSpec((RET_HEADS, V7X_LANES), lambda bi, i: (0, 0))
    in_specs = [dec_spec, tok, feat, tok]
    args = [dec_rows, rq, rkt, rv]
    if reverse:
        in_specs += [tok, tok, pl.BlockSpec((1, RET_WIDTH), lambda bi, i: (0, 0))]
        args += [fwd, rg, gn_w]
    c = RET_CHUNK
    scratch = [
        pltpu.VMEM((RET_HEADS, RET_HEAD_DIM, RET_HEAD_DIM), F32),
        pltpu.VMEM((RET_HEADS, c, c), F32),
        pltpu.VMEM((RET_HEADS, c, V7X_LANES), F32),
        pltpu.VMEM((RET_HEADS, RET_HEAD_DIM, c), F32),
        pltpu.VMEM((RET_HEADS, V7X_SUBLANES, V7X_LANES), F32),
    ]
    return pl.pallas_call(
        functools.partial(_ret_kernel, reverse),
        grid=(b, nb),
        in_specs=in_specs,
        out_specs=tok,
        out_shape=jax.ShapeDtypeStruct((b, s, RET_WIDTH), BF16 if reverse else F32),
        scratch_shapes=scratch,
        compiler_params=_params("arbitrary", "arbitrary"),
        name="ret_bwd" if reverse else "ret_fwd",
    )(*args)


NEG_BIG = -1e30


ATTN_GROUP = 2 * V7X_LANES
ATTN_LAG = 3


def _attn_kernel(q_ref, k_ref, vprev_ref, vcur_ref, o_ref, s_ref, m_old_ref, m_cur_ref, l_ref, acc_ref):
    t = pl.program_id(3)
    nk = pl.num_programs(3) - 1
    gw = ATTN_GROUP
    ng = q_ref.shape[2] // gw
    lag = min(ATTN_LAG, ng - 1)
    first_carried = ng - lag
    group = lambda g: slice(g * gw, (g + 1) * gw)

    def softmax_pv(g, vt, m_prev, m_new):
        cols = group(g)
        alpha = jnp.exp2(m_prev - m_new)
        p = jnp.exp2(s_ref[:, cols] - m_new)
        l_ref[:, cols] = alpha * l_ref[:, cols] + jnp.sum(p, axis=0, keepdims=True)
        acc_ref[:, cols] = alpha * acc_ref[:, cols] + _dot(vt, p.astype(BF16))

    def scores(g, m_prev):
        cols = group(g)
        s = _nt_dot(k_ref[0, 0], q_ref[0, 0, cols, :])
        s_ref[:, cols] = s
        return jnp.maximum(m_prev, jnp.max(s, axis=0, keepdims=True))

    def carried_pv(g):
        cols = group(g)
        softmax_pv(g, vprev_ref[0, 0], m_old_ref[:, cols], m_cur_ref[:, cols])

    def step(has_prev):
        m_before, m_after = {}, {}
        for j in range(ng):
            cols = group(j)
            m_before[j] = m_cur_ref[:, cols]
            m_after[j] = scores(j, m_before[j])
            if j >= first_carried:
                m_old_ref[:, cols] = m_before[j]
                m_cur_ref[:, cols] = m_after[j]
            d = j - lag
            if d >= 0:
                softmax_pv(d, vcur_ref[0, 0], m_before[d], m_after[d])
                m_cur_ref[:, group(d)] = m_after[d]
            elif has_prev:
                carried_pv(ng + d)

    @pl.when(t == 0)
    def _():
        m_cur_ref[...] = jnp.full_like(m_cur_ref, NEG_BIG)
        l_ref[...] = jnp.zeros_like(l_ref)
        acc_ref[...] = jnp.zeros_like(acc_ref)
        step(False)

    @pl.when(jnp.logical_and(t > 0, t < nk))
    def _():
        step(True)

    @pl.when(t == nk)
    def _():
        for g in range(first_carried, ng):
            carried_pv(g)
        o = acc_ref[...] / l_ref[...]
        o_ref[0] = o.T.astype(o_ref.dtype)


def _attention(q, k, vt):
    b, nh, s, _ = q.shape
    tq = _tile(s, 2048)
    tk = _tile(s, 2048)
    nk = s // tk
    return pl.pallas_call(
        _attn_kernel,
        grid=(b, nh, s // tq, nk + 1),
        in_specs=[
            pl.BlockSpec((1, 1, tq, QK_PAD), lambda bi, hi, qi, t: (bi, hi, qi, 0)),
            pl.BlockSpec((1, 1, tk, QK_PAD), lambda bi, hi, qi, t: (bi, hi, jnp.minimum(t, nk - 1), 0)),
            pl.BlockSpec((1, 1, MLA_V_DIM, tk), lambda bi, hi, qi, t: (bi, hi, 0, jnp.maximum(t - 1, 0))),
            pl.BlockSpec((1, 1, MLA_V_DIM, tk), lambda bi, hi, qi, t: (bi, hi, 0, jnp.minimum(t, nk - 1))),
        ],
        out_specs=pl.BlockSpec((1, tq, MLA_V_DIM), lambda bi, hi, qi, t: (bi, qi, hi)),
        out_shape=jax.ShapeDtypeStruct((b, s, MLA_WIDTH), BF16),
        scratch_shapes=[
            pltpu.VMEM((tk, tq), F32),
            pltpu.VMEM((1, tq), F32),
            pltpu.VMEM((1, tq), F32),
            pltpu.VMEM((1, tq), F32),
            pltpu.VMEM((MLA_V_DIM, tq), F32),
        ],
        compiler_params=_params("arbitrary", "arbitrary", "arbitrary", "arbitrary"),
        name="attn",
    )(q, k, vt, vt)


FF_CHUNK = 1024
MLP_SUB = 512


def _mlp_kernel(alpha, x_ref, ret_ref, att_ref, mod_ref, w_o_ref, ln1_w_ref, ln1_b_ref,
                w_up_ref, w_down_ref, ln2_w_ref, ln2_b_ref, o_ref):
    g1 = mod_ref[0, 2:3, :]
    sh2 = mod_ref[0, 3:4, :]
    sc2 = mod_ref[0, 4:5, :]
    g2 = mod_ref[0, 5:6, :]
    n_sub = max(x_ref.shape[1] // MLP_SUB, 1)
    sub = x_ref.shape[1] // n_sub
    rows = [slice(i * sub, (i + 1) * sub) for i in range(n_sub)]
    mixes = [_dot(ret_ref[0, r, :], w_o_ref[0:RET_WIDTH, :]) + _dot(att_ref[0, r, :], w_o_ref[RET_WIDTH:, :])
             for r in rows]
    x1s, hs = [], []
    for r, mix in zip(rows, mixes):
        x1 = _layer_norm(alpha * x_ref[0, r, :] + g1 * mix) * ln1_w_ref[...] + ln1_b_ref[...]
        x1s.append(x1)
        hs.append((_layer_norm(x1) * (1.0 + sc2) + sh2).astype(BF16))
    for r, x1, h in zip(rows, x1s, hs):
        ff = jnp.zeros_like(x1)
        for c0 in range(0, D_FF, FF_CHUNK):
            u = jnp.maximum(_dot(h, w_up_ref[:, c0:c0 + FF_CHUNK]), 0.0)
            ff = ff + _dot((u * u).astype(BF16), w_down_ref[c0:c0 + FF_CHUNK, :])
        o_ref[0, r, :] = _layer_norm(alpha * x1 + g2 * ff) * ln2_w_ref[...] + ln2_b_ref[...]


def _mlp(alpha, x, ret, att, mod, mod_row0, w):
    b, s, _ = x.shape
    tm = _tile(s, 2 * MLP_SUB)
    xs = pl.BlockSpec((1, tm, D_MODEL), lambda bi, i: (bi, i, 0))
    half = pl.BlockSpec((1, tm, RET_WIDTH), lambda bi, i: (bi, i, 0))
    vec = lambda: pl.BlockSpec((1, D_MODEL), lambda bi, i: (0, 0))
    return pl.pallas_call(
        functools.partial(_mlp_kernel, alpha),
        grid=(b, s // tm),
        in_specs=[
            xs, half, half,
            pl.BlockSpec((1, N_MOD, D_MODEL), lambda bi, i: (bi + mod_row0, 0, 0)),
            _const_spec(w["w_o"].shape), vec(), vec(),
            _const_spec(w["w_up"].shape), _const_spec(w["w_down"].shape), vec(), vec(),
        ],
        out_specs=xs,
        out_shape=jax.ShapeDtypeStruct((b, s, D_MODEL), F32),
        compiler_params=_params("arbitrary", "arbitrary"),
        name="mlp",
    )(x, ret, att, mod, w["w_o"], w["ln1_w"], w["ln1_b"], w["w_up"], w["w_down"], w["ln2_w"], w["ln2_b"])


def _prep_layer_weights(w_in, ret_decay_f, ret_decay_b, ret_gn_w, q_norm_w, w_uq, kv_norm_w, w_ukv,
                        w_o, ln1_w, ln1_b, w_up, w_down, ln2_w, ln2_b):
    rw = RET_WIDTH
    w_rq, w_rk, w_rest = w_in[:, :rw], w_in[:, rw:2 * rw], w_in[:, 2 * rw:]
    pad = jnp.zeros((D_MODEL, _W_IN_COLS - (w_in.shape[1] - rw)), w_in.dtype)
    w_in_p = jnp.concatenate([w_rq, w_rest, pad], axis=1).astype(BF16)
    uq = w_uq.reshape(Q_LORA, MLA_HEADS, MLA_QK_DIM)
    ukv = w_ukv.reshape(KV_LORA, MLA_HEADS, MLA_NOPE_DIM + MLA_V_DIM)
    lanes = lambda v: jnp.broadcast_to(v.reshape(RET_HEADS, 1), (RET_HEADS, V7X_LANES)).astype(F32)
    return {
        "w_in": w_in_p,
        "w_kt": w_rk.T.astype(BF16),
        "q_norm_w": q_norm_w.reshape(1, Q_LORA),
        "w_uqn": uq[:, :, :MLA_NOPE_DIM].reshape(Q_LORA, MLA_HEADS * MLA_NOPE_DIM).astype(BF16),
        "w_uqr": uq[:, :, MLA_NOPE_DIM:].reshape(Q_LORA, MLA_HEADS * MLA_ROPE_DIM).astype(BF16),
        "kv_norm_w": kv_norm_w.reshape(1, KV_LORA),
        "w_uk": ukv[:, :, :MLA_NOPE_DIM].reshape(KV_LORA, MLA_HEADS * MLA_NOPE_DIM).astype(BF16),
        "w_uvt": ukv[:, :, MLA_NOPE_DIM:].reshape(KV_LORA, MLA_HEADS * MLA_V_DIM).T.astype(BF16),
        "dec_f": lanes(ret_decay_f),
        "dec_b": lanes(ret_decay_b),
        "gn_w": ret_gn_w.reshape(1, RET_WIDTH),
        "w_o": w_o.astype(BF16),
        "ln1_w": ln1_w.reshape(1, D_MODEL), "ln1_b": ln1_b.reshape(1, D_MODEL),
        "w_up": w_up.astype(BF16), "w_down": w_down.astype(BF16),
        "ln2_w": ln2_w.reshape(1, D_MODEL), "ln2_b": ln2_b.reshape(1, D_MODEL),
    }


def _layer(alpha, x, mod, mod_row0, tables, w):
    rq, rkt, rv, rg, q, k, vt = _proj(x, mod, mod_row0, tables, w)
    fwd = _retention(w["dec_f"], rq, rkt, rv, reverse=False)
    ret = _retention(w["dec_b"], rq, rkt, rv, reverse=True, fwd=fwd, rg=rg, gn_w=w["gn_w"])
    att = _attention(q, k, vt)
    return _mlp(alpha, x, ret, att, mod, mod_row0, w)


def _trunks(xs, cs, w_ada, b_ada, layer_weights):
    depth = w_ada.shape[0]
    alpha = float((2 * depth) ** 0.25)
    n_rows = sum(c.shape[0] for c in cs)
    assert n_rows <= ROWS_PAD
    c_pad = jnp.concatenate(list(cs) + [jnp.zeros((ROWS_PAD - n_rows, D_MODEL), F32)], axis=0)
    tables = _rotary_tables(max(x.shape[1] for x in xs))
    outs = list(xs)
    for l in range(depth):
        mod = _ada(c_pad, w_ada[l], b_ada[l]).reshape(ROWS_PAD, N_MOD, D_MODEL)
        w = _prep_layer_weights(*[t[l] for t in layer_weights])
        row0 = 0
        for i, c in enumerate(cs):
            outs[i] = _layer(alpha, outs[i], mod, row0, tables, w)
            row0 += c.shape[0]
    return tuple(outs)


def kernel(x_prompt, x_sample, c_prompt, c_sample, w_ada, b_ada, w_in, ret_decay_f, ret_decay_b, ret_gn_w,
           q_norm_w, w_uq, kv_norm_w, w_ukv, w_o, ln1_w, ln1_b, w_up, w_down, ln2_w, ln2_b):
    layer_weights = (w_in, ret_decay_f, ret_decay_b, ret_gn_w, q_norm_w, w_uq, kv_norm_w, w_ukv,
                     w_o, ln1_w, ln1_b, w_up, w_down, ln2_w, ln2_b)
    return _trunks((x_prompt, x_sample), (c_prompt, c_sample), w_ada, b_ada, layer_weights)
```

```python
import functools
import math

import jax
import jax.numpy as jnp
from jax import lax
from jax.experimental import pallas as pl
from jax.experimental.pallas import tpu as pltpu

F32 = jnp.float32
BF16 = jnp.bfloat16

D_MODEL = 1024
RET_HEADS = 4
RET_HEAD_DIM = 128
RET_WIDTH = RET_HEADS * RET_HEAD_DIM
MLA_HEADS = 4
MLA_NOPE_DIM = 128
MLA_ROPE_DIM = 64
MLA_V_DIM = 128
MLA_WIDTH = MLA_HEADS * MLA_V_DIM
MLA_QK_DIM = MLA_NOPE_DIM + MLA_ROPE_DIM
Q_LORA = 384
KV_LORA = 256
D_FF = 4 * D_MODEL
ROPE_BASE = 10000.0
LN_EPS = 1e-5
RMS_EPS = 1e-6
N_MOD = 6
MLA_SCALE = float(MLA_QK_DIM ** -0.5)
LOG2E = 1.4426950408889634

V7X_LANES = 128
V7X_SUBLANES = 8
V7X_VMEM_BYTES = 64 * 1024 * 1024
VMEM_LIMIT = V7X_VMEM_BYTES - 8 * 1024 * 1024

QK_PAD = 2 * V7X_LANES
_OFF_RQ, _OFF_RV, _OFF_RG = 0, RET_WIDTH, 2 * RET_WIDTH
_OFF_CQ = 3 * RET_WIDTH
_OFF_CKV = _OFF_CQ + Q_LORA
_OFF_KR = _OFF_CKV + KV_LORA
_W_IN_COLS = _OFF_KR + V7X_LANES
ROWS_PAD = V7X_SUBLANES
RET_CHUNK = 128


def _tile(n, pref):
    t = min(n, pref)
    assert n % t == 0, (n, t)
    return t


def _const_spec(shape):
    nd = len(shape)
    return pl.BlockSpec(shape, lambda *_: (0,) * nd, pipeline_mode=pl.Buffered(1))


def _params(*sem):
    return pltpu.CompilerParams(dimension_semantics=sem, vmem_limit_bytes=VMEM_LIMIT)


def _nt_dot(a, b):
    return lax.dot_general(a, b, (((1,), (1,)), ((), ())), preferred_element_type=F32)


def _dot(a, b):
    return jnp.dot(a, b, preferred_element_type=F32)


def _ada_kernel(c_ref, w_ref, b_ref, o_ref):
    c = c_ref[...]
    s = c / (1.0 + jnp.exp(-c))
    o_ref[...] = _dot(s.astype(BF16), w_ref[...].astype(BF16)) + b_ref[...]


def _ada(c_pad, w_ada, b_ada):
    n = w_ada.shape[1]
    tn = _tile(n, D_MODEL)
    return pl.pallas_call(
        _ada_kernel,
        grid=(n // tn,),
        in_specs=[
            pl.BlockSpec((ROWS_PAD, D_MODEL), lambda j: (0, 0)),
            pl.BlockSpec((D_MODEL, tn), lambda j: (0, j)),
            pl.BlockSpec((1, tn), lambda j: (0, j)),
        ],
        out_specs=pl.BlockSpec((ROWS_PAD, tn), lambda j: (0, j)),
        out_shape=jax.ShapeDtypeStruct((ROWS_PAD, n), F32),
        compiler_params=_params("arbitrary"),
        name="ada",
    )(c_pad, w_ada, b_ada.reshape(1, n))


def _tables_kernel(inv_ref, sgn_r_ref, sgn_m_ref,
                   cos_r_ref, sin_r_ref, cos_m_ref, sin_m_ref, cos_rt_ref, sin_rt_ref):
    tr = cos_r_ref.shape[0]
    half = V7X_LANES // 2
    pos = (lax.broadcasted_iota(jnp.int32, (tr, V7X_LANES), 0) + pl.program_id(0) * tr).astype(F32)
    ang = pos * inv_ref[...]
    c = jnp.cos(ang)
    s = jnp.sin(ang)
    c_sw = pltpu.roll(c, half, 1)
    s_sw = pltpu.roll(s, half, 1)
    low = lax.broadcasted_iota(jnp.int32, (tr, V7X_LANES), 1) < half
    cos_r = jnp.where(low, c, c_sw)
    sin_r = jnp.where(low, s, s_sw) * sgn_r_ref[...]
    cos_r_ref[...] = cos_r
    sin_r_ref[...] = sin_r
    cos_m_ref[...] = jnp.where(low, c_sw, c)
    sin_m_ref[...] = jnp.where(low, s_sw, s) * sgn_m_ref[...]
    cos_rt_ref[...] = cos_r.T
    sin_rt_ref[...] = sin_r.T


def _rotary_tables(seq):
    half_r = RET_HEAD_DIM // 2
    half_m = MLA_ROPE_DIM // 2
    assert half_r + 2 * half_m == V7X_LANES and RET_HEAD_DIM == V7X_LANES
    inv_r = ROPE_BASE ** (-jnp.arange(0, RET_HEAD_DIM, 2, dtype=F32) / RET_HEAD_DIM)
    inv_m = ROPE_BASE ** (-jnp.arange(0, MLA_ROPE_DIM, 2, dtype=F32) / MLA_ROPE_DIM)
    inv = jnp.concatenate([inv_r, inv_m, inv_m])
    sgn_r = jnp.concatenate([-jnp.ones(half_r, F32), jnp.ones(half_r, F32)])
    sgn_m = jnp.tile(jnp.concatenate([-jnp.ones(half_m, F32), jnp.ones(half_m, F32)]),
                     V7X_LANES // MLA_ROPE_DIM)
    tr = _tile(seq, 1024)
    row = lambda v: v.reshape(1, V7X_LANES)
    tok = jax.ShapeDtypeStruct((seq, V7X_LANES), F32)
    feat = jax.ShapeDtypeStruct((RET_HEAD_DIM, seq), F32)
    vec = pl.BlockSpec((1, V7X_LANES), lambda i: (0, 0))
    tok_spec = pl.BlockSpec((tr, V7X_LANES), lambda i: (i, 0))
    feat_spec = pl.BlockSpec((RET_HEAD_DIM, tr), lambda i: (0, i))
    return pl.pallas_call(
        _tables_kernel,
        grid=(seq // tr,),
        in_specs=[vec, vec, vec],
        out_specs=[tok_spec, tok_spec, tok_spec, tok_spec, feat_spec, feat_spec],
        out_shape=[tok, tok, tok, tok, feat, feat],
        compiler_params=_params("arbitrary"),
        name="tables",
    )(row(inv), row(sgn_r), row(sgn_m))


def _layer_norm(x):
    mu = jnp.mean(x, axis=-1, keepdims=True)
    xc = x - mu
    var = jnp.mean(xc * xc, axis=-1, keepdims=True)
    return xc * lax.rsqrt(var + LN_EPS)


def _rms_norm(x, w):
    return x * lax.rsqrt(jnp.mean(x * x, axis=-1, keepdims=True) + RMS_EPS) * w


def _rope64(x, cos_m, sin_m, first_half):
    swapped = jnp.where(first_half,
                        pltpu.roll(x, V7X_LANES - MLA_ROPE_DIM // 2, 1),
                        pltpu.roll(x, MLA_ROPE_DIM // 2, 1))
    return x * cos_m + swapped * sin_m


def _proj_kernel(x_ref, mod_ref, cos_r_ref, sin_r_ref, cos_m_ref, sin_m_ref, cos_rt_ref, sin_rt_ref,
                 w_in_ref, w_kt_ref, qn_w_ref, w_uqn_ref, w_uqr_ref, kvn_w_ref, w_uk_ref, w_uvt_ref,
                 rq_ref, rkt_ref, rv_ref, rg_ref, q_ref, k_ref, vt_ref):
    sh1 = mod_ref[0, 0:1, :]
    sc1 = mod_ref[0, 1:2, :]
    hd = RET_HEAD_DIM
    k_scale = RET_HEAD_DIM ** -0.5
    q_scale = MLA_SCALE * LOG2E
    n_sub = max(x_ref.shape[1] // PROJ_SUB, 1)
    sub = x_ref.shape[1] // n_sub
    rows = [slice(i * sub, (i + 1) * sub) for i in range(n_sub)]
    lane = lax.broadcasted_iota(jnp.int32, (sub, V7X_LANES), 1)
    first_half = (lane % MLA_ROPE_DIM) < (MLA_ROPE_DIM // 2)
    low_lanes = lane < MLA_ROPE_DIM

    hs = [(_layer_norm(x_ref[0, r, :]) * (1.0 + sc1) + sh1).astype(BF16) for r in rows]

    def retention_part(r, h):
        proj = lambda lo, n: _dot(h, w_in_ref[:, lo:lo + n])
        cos_r = cos_r_ref[r, :]
        sin_r = sin_r_ref[r, :]
        rq = proj(_OFF_RQ, RET_WIDTH)
        for hh in range(RET_HEADS):
            t = rq[:, hh * hd:(hh + 1) * hd]
            rq_ref[0, r, hh * hd:(hh + 1) * hd] = (t * cos_r + pltpu.roll(t, hd // 2, 1) * sin_r).astype(BF16)
        cos_rt = cos_rt_ref[:, r]
        sin_rt = sin_rt_ref[:, r]
        rkt = _nt_dot(w_kt_ref[...], h)
        chunk0 = r.start // RET_CHUNK
        for hh in range(RET_HEADS):
            t = rkt[hh * hd:(hh + 1) * hd, :]
            rot = ((t * cos_rt + pltpu.roll(t, hd // 2, 0) * sin_rt) * k_scale).astype(BF16)
            for j in range(sub // RET_CHUNK):
                rkt_ref[0, chunk0 + j, hh * hd:(hh + 1) * hd, :] = rot[:, j * RET_CHUNK:(j + 1) * RET_CHUNK]
        rv_ref[0, r, :] = proj(_OFF_RV, RET_WIDTH).astype(BF16)
        g = proj(_OFF_RG, RET_WIDTH)
        rg_ref[0, r, :] = (g / (1.0 + jnp.exp(-g))).astype(BF16)

    def mla_part(r, h):
        proj = lambda lo, n: _dot(h, w_in_ref[:, lo:lo + n])
        cos_m = cos_m_ref[r, :]
        sin_m = sin_m_ref[r, :]
        cq = _rms_norm(proj(_OFF_CQ, Q_LORA), qn_w_ref[...]).astype(BF16)
        ckv = _rms_norm(proj(_OFF_CKV, KV_LORA), kvn_w_ref[...]).astype(BF16)
        k_pe = _rope64(proj(_OFF_KR, V7X_LANES), cos_m, sin_m, first_half)
        q_nope = _dot(cq, w_uqn_ref[...]) * q_scale
        q_rope = _dot(cq, w_uqr_ref[...]) * q_scale
        k_nope = _dot(ckv, w_uk_ref[...])
        v_t = _nt_dot(w_uvt_ref[...], ckv)
        for pair in range(MLA_HEADS // 2):
            rp = _rope64(q_rope[:, pair * V7X_LANES:(pair + 1) * V7X_LANES], cos_m, sin_m, first_half)
            for side in range(2):
                hh = 2 * pair + side
                keep = low_lanes if side == 0 else jnp.logical_not(low_lanes)
                q_ref[0, hh, r, 0:MLA_NOPE_DIM] = q_nope[:, hh * MLA_NOPE_DIM:(hh + 1) * MLA_NOPE_DIM].astype(BF16)
                q_ref[0, hh, r, MLA_NOPE_DIM:QK_PAD] = jnp.where(keep, rp, 0.0).astype(BF16)
        k_pe_hi = pltpu.roll(k_pe, MLA_ROPE_DIM, 1)
        for hh in range(MLA_HEADS):
            k_ref[0, hh, r, 0:MLA_NOPE_DIM] = k_nope[:, hh * MLA_NOPE_DIM:(hh + 1) * MLA_NOPE_DIM].astype(BF16)
            k_ref[0, hh, r, MLA_NOPE_DIM:QK_PAD] = (k_pe if hh % 2 == 0 else k_pe_hi).astype(BF16)
            vt_ref[0, hh, :, r] = v_t[hh * MLA_V_DIM:(hh + 1) * MLA_V_DIM, :].astype(BF16)

    for r, h in zip(rows, hs):
        retention_part(r, h)
    for r, h in zip(rows, hs):
        mla_part(r, h)


PROJ_SUB = 512


def _proj(x, mod, mod_row0, tables, w):
    b, s, _ = x.shape
    tm = _tile(s, 2 * PROJ_SUB)
    cos_r, sin_r, cos_m, sin_m, cos_rt, sin_rt = tables
    tok_tab = pl.BlockSpec((tm, V7X_LANES), lambda bi, i: (i, 0))
    feat_tab = pl.BlockSpec((RET_HEAD_DIM, tm), lambda bi, i: (0, i))
    tok_out = lambda: pl.BlockSpec((1, tm, RET_WIDTH), lambda bi, i: (bi, i, 0))
    out_shape = [
        jax.ShapeDtypeStruct((b, s, RET_WIDTH), BF16),
        jax.ShapeDtypeStruct((b, s // RET_CHUNK, RET_WIDTH, RET_CHUNK), BF16),
        jax.ShapeDtypeStruct((b, s, RET_WIDTH), BF16),
        jax.ShapeDtypeStruct((b, s, RET_WIDTH), BF16),
        jax.ShapeDtypeStruct((b, MLA_HEADS, s, QK_PAD), BF16),
        jax.ShapeDtypeStruct((b, MLA_HEADS, s, QK_PAD), BF16),
        jax.ShapeDtypeStruct((b, MLA_HEADS, MLA_V_DIM, s), BF16),
    ]
    out_specs = [
        tok_out(),
        pl.BlockSpec((1, tm // RET_CHUNK, RET_WIDTH, RET_CHUNK), lambda bi, i: (bi, i, 0, 0)),
        tok_out(),
        tok_out(),
        pl.BlockSpec((1, MLA_HEADS, tm, QK_PAD), lambda bi, i: (bi, 0, i, 0)),
        pl.BlockSpec((1, MLA_HEADS, tm, QK_PAD), lambda bi, i: (bi, 0, i, 0)),
        pl.BlockSpec((1, MLA_HEADS, MLA_V_DIM, tm), lambda bi, i: (bi, 0, 0, i)),
    ]
    in_specs = [
        pl.BlockSpec((1, tm, D_MODEL), lambda bi, i: (bi, i, 0)),
        pl.BlockSpec((1, N_MOD, D_MODEL), lambda bi, i: (bi + mod_row0, 0, 0)),
        tok_tab, tok_tab, tok_tab, tok_tab, feat_tab, feat_tab,
        _const_spec(w["w_in"].shape), _const_spec(w["w_kt"].shape),
        _const_spec(w["q_norm_w"].shape), _const_spec(w["w_uqn"].shape), _const_spec(w["w_uqr"].shape),
        _const_spec(w["kv_norm_w"].shape), _const_spec(w["w_uk"].shape), _const_spec(w["w_uvt"].shape),
    ]
    return pl.pallas_call(
        _proj_kernel,
        grid=(b, s // tm),
        in_specs=in_specs,
        out_specs=out_specs,
        out_shape=out_shape,
        compiler_params=_params("arbitrary", "arbitrary"),
        name="proj",
    )(x, mod, cos_r, sin_r, cos_m, sin_m, cos_rt, sin_rt,
      w["w_in"], w["w_kt"], w["q_norm_w"], w["w_uqn"], w["w_uqr"], w["kv_norm_w"], w["w_uk"], w["w_uvt"])


def _log_sigmoid(x):
    return jnp.minimum(x, 0.0) - jnp.log(1.0 + jnp.exp(-jnp.abs(x)))


def _ret_decay_tables(dec_row, reverse, d_in_ref, d_q_ref, d_k_ref, d_c_ref, hh):
    c = RET_CHUNK
    lg = _log_sigmoid(dec_row)[:, 0:1]
    row = lax.broadcasted_iota(jnp.int32, (c, c), 0)
    col = lax.broadcasted_iota(jnp.int32, (c, c), 1)
    rel = (col - row) if reverse else (row - col)
    mask = (rel > 0) if reverse else (rel >= 0)
    relf = jnp.maximum(rel, 0).astype(F32)
    d_in_ref[hh] = jnp.where(mask, jnp.exp(lg * relf), 0.0)
    qi = lax.broadcasted_iota(jnp.int32, (c, V7X_LANES), 0).astype(F32)
    kj = lax.broadcasted_iota(jnp.int32, (RET_HEAD_DIM, c), 1).astype(F32)
    q_steps = (c - qi) if reverse else (qi + 1.0)
    k_steps = kj if reverse else (c - 1.0 - kj)
    d_q_ref[hh] = jnp.exp(lg * q_steps)
    d_k_ref[hh] = jnp.exp(lg * k_steps)
    d_c_ref[hh] = jnp.exp(lg * float(c)) + jnp.zeros((V7X_SUBLANES, V7X_LANES), F32)


def _ret_kernel(reverse, *refs):
    if reverse:
        (dec_ref, rq_ref, rkt_ref, rv_ref, fwd_ref, rg_ref, gnw_ref, out_ref,
         state_ref, d_in_ref, d_q_ref, d_k_ref, d_c_ref) = refs
    else:
        (dec_ref, rq_ref, rkt_ref, rv_ref, out_ref,
         state_ref, d_in_ref, d_q_ref, d_k_ref, d_c_ref) = refs
    c = RET_CHUNK
    hd = RET_HEAD_DIM
    n_chunks = rq_ref.shape[1] // c

    @pl.when(pl.program_id(1) == 0)
    def _():
        state_ref[...] = jnp.zeros_like(state_ref)
        for hh in range(RET_HEADS):
            _ret_decay_tables(dec_ref[hh:hh + 1, :], reverse, d_in_ref, d_q_ref, d_k_ref, d_c_ref, hh)

    def chunk_body(ci, carry):
        cidx = (n_chunks - 1 - ci) if reverse else ci
        t0 = pl.multiple_of(cidx * c, c)
        for hh in range(RET_HEADS):
            cols = slice(hh * hd, (hh + 1) * hd)
            q = rq_ref[0, pl.ds(t0, c), cols]
            kt = rkt_ref[0, cidx, cols, :]
            v = rv_ref[0, pl.ds(t0, c), cols]
            state = state_ref[hh]
            s = _dot(q, kt) * d_in_ref[hh]
            inner = _dot(s.astype(BF16), v)
            cross = _dot(q, state.astype(BF16)) * d_q_ref[hh]
            kt_dec = (kt.astype(F32) * d_k_ref[hh]).astype(BF16)
            state_ref[hh] = state * d_c_ref[hh][0:1, :] + _dot(kt_dec, v)
            y = inner + cross
            if reverse:
                y = y + fwd_ref[0, pl.ds(t0, c), cols]
                mu = jnp.mean(y, axis=-1, keepdims=True)
                yc = y - mu
                var = jnp.mean(yc * yc, axis=-1, keepdims=True)
                yn = yc * lax.rsqrt(var + LN_EPS) * gnw_ref[:, cols]
                gate = rg_ref[0, pl.ds(t0, c), cols].astype(F32)
                out_ref[0, pl.ds(t0, c), cols] = (gate * yn).astype(out_ref.dtype)
            else:
                out_ref[0, pl.ds(t0, c), cols] = y
        return carry

    lax.fori_loop(0, n_chunks, chunk_body, 0, unroll=True)


def _retention(dec_rows, rq, rkt, rv, reverse, fwd=None, rg=None, gn_w=None):
    b, s, _ = rq.shape
    tb = _tile(s, 1024)
    nb = s // tb
    blk = (lambda i: nb - 1 - i) if reverse else (lambda i: i)
    tok = pl.BlockSpec((1, tb, RET_WIDTH), lambda bi, i: (bi, blk(i), 0))
    feat = pl.BlockSpec((1, tb // RET_CHUNK, RET_WIDTH, RET_CHUNK), lambda bi, i: (bi, blk(i), 0, 0))
    dec_spec = pl.BlockSpec((RET_HEADS, V7X_LANES), lambda bi, i: (0, 0))
    in_specs = [dec_spec, tok, feat, tok]
    args = [dec_rows, rq, rkt, rv]
    if reverse:
        in_specs += [tok, tok, pl.BlockSpec((1, RET_WIDTH), lambda bi, i: (0, 0))]
        args += [fwd, rg, gn_w]
    c = RET_CHUNK
    scratch = [
        pltpu.VMEM((RET_HEADS, RET_HEAD_DIM, RET_HEAD_DIM), F32),
        pltpu.VMEM((RET_HEADS, c, c), F32),
        pltpu.VMEM((RET_HEADS, c, V7X_LANES), F32),
        pltpu.VMEM((RET_HEADS, RET_HEAD_DIM, c), F32),
        pltpu.VMEM((RET_HEADS, V7X_SUBLANES, V7X_LANES), F32),
    ]
    return pl.pallas_call(
        functools.partial(_ret_kernel, reverse),
        grid=(b, nb),
        in_specs=in_specs,
        out_specs=tok,
        out_shape=jax.ShapeDtypeStruct((b, s, RET_WIDTH), BF16 if reverse else F32),
        scratch_shapes=scratch,
        compiler_params=_params("arbitrary", "arbitrary"),
        name="ret_bwd" if reverse else "ret_fwd",
    )(*args)


NEG_BIG = -1e30


ATTN_GROUP = 2 * V7X_LANES
ATTN_LAG = 3


def _attn_kernel(q_ref, k_ref, vprev_ref, vcur_ref, o_ref, s_ref, m_old_ref, m_cur_ref, l_ref, acc_ref):
    t = pl.program_id(3)
    nk = pl.num_programs(3) - 1
    gw = ATTN_GROUP
    ng = q_ref.shape[2] // gw
    lag = min(ATTN_LAG, ng - 1)
    first_carried = ng - lag
    group = lambda g: slice(g * gw, (g + 1) * gw)

    def softmax_pv(g, vt, m_prev, m_new):
        cols = group(g)
        alpha = jnp.exp2(m_prev - m_new)
        p = jnp.exp2(s_ref[:, cols] - m_new)
        l_ref[:, cols] = alpha * l_ref[:, cols] + jnp.sum(p, axis=0, keepdims=True)
        acc_ref[:, cols] = alpha * acc_ref[:, cols] + _dot(vt, p.astype(BF16))

    def scores(g, m_prev):
        cols = group(g)
        s = _nt_dot(k_ref[0, 0], q_ref[0, 0, cols, :])
        s_ref[:, cols] = s
        return jnp.maximum(m_prev, jnp.max(s, axis=0, keepdims=True))

    def carried_pv(g):
        cols = group(g)
        softmax_pv(g, vprev_ref[0, 0], m_old_ref[:, cols], m_cur_ref[:, cols])

    def step(has_prev):
        m_before, m_after = {}, {}
        for j in range(ng):
            cols = group(j)
            m_before[j] = m_cur_ref[:, cols]
            m_after[j] = scores(j, m_before[j])
            if j >= first_carried:
                m_old_ref[:, cols] = m_before[j]
                m_cur_ref[:, cols] = m_after[j]
            d = j - lag
            if d >= 0:
                softmax_pv(d, vcur_ref[0, 0], m_before[d], m_after[d])
                m_cur_ref[:, group(d)] = m_after[d]
            elif has_prev:
                carried_pv(ng + d)

    @pl.when(t == 0)
    def _():
        m_cur_ref[...] = jnp.full_like(m_cur_ref, NEG_BIG)
        l_ref[...] = jnp.zeros_like(l_ref)
        acc_ref[...] = jnp.zeros_like(acc_ref)
        step(False)

    @pl.when(jnp.logical_and(t > 0, t < nk))
    def _():
        step(True)

    @pl.when(t == nk)
    def _():
        for g in range(first_carried, ng):
            carried_pv(g)
        o = acc_ref[...] / l_ref[...]
        o_ref[0] = o.T.astype(o_ref.dtype)


def _attention(q, k, vt):
    b, nh, s, _ = q.shape
    tq = _tile(s, 4096)
    tk = _tile(s, 2048)
    nk = s // tk
    return pl.pallas_call(
        _attn_kernel,
        grid=(b, nh, s // tq, nk + 1),
        in_specs=[
            pl.BlockSpec((1, 1, tq, QK_PAD), lambda bi, hi, qi, t: (bi, hi, qi, 0)),
            pl.BlockSpec((1, 1, tk, QK_PAD), lambda bi, hi, qi, t: (bi, hi, jnp.minimum(t, nk - 1), 0)),
            pl.BlockSpec((1, 1, MLA_V_DIM, tk), lambda bi, hi, qi, t: (bi, hi, 0, jnp.maximum(t - 1, 0))),
            pl.BlockSpec((1, 1, MLA_V_DIM, tk), lambda bi, hi, qi, t: (bi, hi, 0, jnp.minimum(t, nk - 1))),
        ],
        out_specs=pl.BlockSpec((1, tq, MLA_V_DIM), lambda bi, hi, qi, t: (bi, qi, hi)),
        out_shape=jax.ShapeDtypeStruct((b, s, MLA_WIDTH), BF16),
        scratch_shapes=[
            pltpu.VMEM((tk, tq), F32),
            pltpu.VMEM((1, tq), F32),
            pltpu.VMEM((1, tq), F32),
            pltpu.VMEM((1, tq), F32),
            pltpu.VMEM((MLA_V_DIM, tq), F32),
        ],
        compiler_params=_params("arbitrary", "arbitrary", "arbitrary", "arbitrary"),
        name="attn",
    )(q, k, vt, vt)


FF_CHUNK = 1024
MLP_SUB = 512


def _mlp_kernel(alpha, x_ref, ret_ref, att_ref, mod_ref, w_o_ref, ln1_w_ref, ln1_b_ref,
                w_up_ref, w_down_ref, ln2_w_ref, ln2_b_ref, o_ref):
    g1 = mod_ref[0, 2:3, :]
    sh2 = mod_ref[0, 3:4, :]
    sc2 = mod_ref[0, 4:5, :]
    g2 = mod_ref[0, 5:6, :]
    n_sub = max(x_ref.shape[1] // MLP_SUB, 1)
    sub = x_ref.shape[1] // n_sub
    rows = [slice(i * sub, (i + 1) * sub) for i in range(n_sub)]
    mixes = [_dot(ret_ref[0, r, :], w_o_ref[0:RET_WIDTH, :]) + _dot(att_ref[0, r, :], w_o_ref[RET_WIDTH:, :])
             for r in rows]
    x1s, hs = [], []
    for r, mix in zip(rows, mixes):
        x1 = _layer_norm(alpha * x_ref[0, r, :] + g1 * mix) * ln1_w_ref[...] + ln1_b_ref[...]
        x1s.append(x1)
        hs.append((_layer_norm(x1) * (1.0 + sc2) + sh2).astype(BF16))
    for r, x1, h in zip(rows, x1s, hs):
        ff = jnp.zeros_like(x1)
        for c0 in range(0, D_FF, FF_CHUNK):
            u = jnp.maximum(_dot(h, w_up_ref[:, c0:c0 + FF_CHUNK]), 0.0)
            ff = ff + _dot((u * u).astype(BF16), w_down_ref[c0:c0 + FF_CHUNK, :])
        o_ref[0, r, :] = _layer_norm(alpha * x1 + g2 * ff) * ln2_w_ref[...] + ln2_b_ref[...]


def _mlp(alpha, x, ret, att, mod, mod_row0, w):
    b, s, _ = x.shape
    tm = _tile(s, 2 * MLP_SUB)
    xs = pl.BlockSpec((1, tm, D_MODEL), lambda bi, i: (bi, i, 0))
    half = pl.BlockSpec((1, tm, RET_WIDTH), lambda bi, i: (bi, i, 0))
    vec = lambda: pl.BlockSpec((1, D_MODEL), lambda bi, i: (0, 0))
    return pl.pallas_call(
        functools.partial(_mlp_kernel, alpha),
        grid=(b, s // tm),
        in_specs=[
            xs, half, half,
            pl.BlockSpec((1, N_MOD, D_MODEL), lambda bi, i: (bi + mod_row0, 0, 0)),
            _const_spec(w["w_o"].shape), vec(), vec(),
            _const_spec(w["w_up"].shape), _const_spec(w["w_down"].shape), vec(), vec(),
        ],
        out_specs=xs,
        out_shape=jax.ShapeDtypeStruct((b, s, D_MODEL), F32),
        compiler_params=_params("arbitrary", "arbitrary"),
        name="mlp",
    )(x, ret, att, mod, w["w_o"], w["ln1_w"], w["ln1_b"], w["w_up"], w["w_down"], w["ln2_w"], w["ln2_b"])


def _prep_layer_weights(w_in, ret_decay_f, ret_decay_b, ret_gn_w, q_norm_w, w_uq, kv_norm_w, w_ukv,
                        w_o, ln1_w, ln1_b, w_up, w_down, ln2_w, ln2_b):
    rw = RET_WIDTH
    w_rq, w_rk, w_rest = w_in[:, :rw], w_in[:, rw:2 * rw], w_in[:, 2 * rw:]
    pad = jnp.zeros((D_MODEL, _W_IN_COLS - (w_in.shape[1] - rw)), w_in.dtype)
    w_in_p = jnp.concatenate([w_rq, w_rest, pad], axis=1).astype(BF16)
    uq = w_uq.reshape(Q_LORA, MLA_HEADS, MLA_QK_DIM)
    ukv = w_ukv.reshape(KV_LORA, MLA_HEADS, MLA_NOPE_DIM + MLA_V_DIM)
    lanes = lambda v: jnp.broadcast_to(v.reshape(RET_HEADS, 1), (RET_HEADS, V7X_LANES)).astype(F32)
    return {
        "w_in": w_in_p,
        "w_kt": w_rk.T.astype(BF16),
        "q_norm_w": q_norm_w.reshape(1, Q_LORA),
        "w_uqn": uq[:, :, :MLA_NOPE_DIM].reshape(Q_LORA, MLA_HEADS * MLA_NOPE_DIM).astype(BF16),
        "w_uqr": uq[:, :, MLA_NOPE_DIM:].reshape(Q_LORA, MLA_HEADS * MLA_ROPE_DIM).astype(BF16),
        "kv_norm_w": kv_norm_w.reshape(1, KV_LORA),
        "w_uk": ukv[:, :, :MLA_NOPE_DIM].reshape(KV_LORA, MLA_HEADS * MLA_NOPE_DIM).astype(BF16),
        "w_uvt": ukv[:, :, MLA_NOPE_DIM:].reshape(KV_LORA, MLA_HEADS * MLA_V_DIM).T.astype(BF16),
        "dec_f": lanes(ret_decay_f),
        "dec_b": lanes(ret_decay_b),
        "gn_w": ret_gn_w.reshape(1, RET_WIDTH),
        "w_o": w_o.astype(BF16),
        "ln1_w": ln1_w.reshape(1, D_MODEL), "ln1_b": ln1_b.reshape(1, D_MODEL),
        "w_up": w_up.astype(BF16), "w_down": w_down.astype(BF16),
        "ln2_w": ln2_w.reshape(1, D_MODEL), "ln2_b": ln2_b.reshape(1, D_MODEL),
    }


def _layer(alpha, x, mod, mod_row0, tables, w):
    rq, rkt, rv, rg, q, k, vt = _proj(x, mod, mod_row0, tables, w)
    fwd = _retention(w["dec_f"], rq, rkt, rv, reverse=False)
    ret = _retention(w["dec_b"], rq, rkt, rv, reverse=True, fwd=fwd, rg=rg, gn_w=w["gn_w"])
    att = _attention(q, k, vt)
    return _mlp(alpha, x, ret, att, mod, mod_row0, w)


def _trunks(xs, cs, w_ada, b_ada, layer_weights):
    depth = w_ada.shape[0]
    alpha = float((2 * depth) ** 0.25)
    n_rows = sum(c.shape[0] for c in cs)
    assert n_rows <= ROWS_PAD
    c_pad = jnp.concatenate(list(cs) + [jnp.zeros((ROWS_PAD - n_rows, D_MODEL), F32)], axis=0)
    tables = _rotary_tables(max(x.shape[1] for x in xs))
    outs = list(xs)
    for l in range(depth):
        mod = _ada(c_pad, w_ada[l], b_ada[l]).reshape(ROWS_PAD, N_MOD, D_MODEL)
        w = _prep_layer_weights(*[t[l] for t in layer_weights])
        row0 = 0
        for i, c in enumerate(cs):
            outs[i] = _layer(alpha, outs[i], mod, row0, tables, w)
            row0 += c.shape[0]
    return tuple(outs)


def kernel(x_prompt, x_sample, c_prompt, c_sample, w_ada, b_ada, w_in, ret_decay_f, ret_decay_b, ret_gn_w,
           q_norm_w, w_uq, kv_norm_w, w_ukv, w_o, ln1_w, ln1_b, w_up, w_down, ln2_w, ln2_b):
    layer_weights = (w_in, ret_decay_f, ret_decay_b, ret_gn_w, q_norm_w, w_uq, kv_norm_w, w_ukv,
                     w_o, ln1_w, ln1_b, w_up, w_down, ln2_w, ln2_b)
    return _trunks((x_prompt, x_sample), (c_prompt, c_sample), w_ada, b_ada, layer_weights)
```

```python
import functools
import math

import jax
import jax.numpy as jnp
from jax import lax
from jax.experimental import pallas as pl
from jax.experimental.pallas import tpu as pltpu

F32 = jnp.float32
BF16 = jnp.bfloat16

D_MODEL = 1024
RET_HEADS = 4
RET_HEAD_DIM = 128
RET_WIDTH = RET_HEADS * RET_HEAD_DIM
MLA_HEADS = 4
MLA_NOPE_DIM = 128
MLA_ROPE_DIM = 64
MLA_V_DIM = 128
MLA_WIDTH = MLA_HEADS * MLA_V_DIM
MLA_QK_DIM = MLA_NOPE_DIM + MLA_ROPE_DIM
Q_LORA = 384
KV_LORA = 256
D_FF = 4 * D_MODEL
ROPE_BASE = 10000.0
LN_EPS = 1e-5
RMS_EPS = 1e-6
N_MOD = 6
MLA_SCALE = float(MLA_QK_DIM ** -0.5)
LOG2E = 1.4426950408889634

V7X_LANES = 128
V7X_SUBLANES = 8
V7X_VMEM_BYTES = 64 * 1024 * 1024
VMEM_LIMIT = V7X_VMEM_BYTES - 8 * 1024 * 1024

QK_PAD = 2 * V7X_LANES
_OFF_RQ, _OFF_RV, _OFF_RG = 0, RET_WIDTH, 2 * RET_WIDTH
_OFF_CQ = 3 * RET_WIDTH
_OFF_CKV = _OFF_CQ + Q_LORA
_OFF_KR = _OFF_CKV + KV_LORA
_W_IN_COLS = _OFF_KR + V7X_LANES
ROWS_PAD = V7X_SUBLANES
RET_CHUNK = 128


def _tile(n, pref):
    t = min(n, pref)
    assert n % t == 0, (n, t)
    return t


def _const_spec(shape):
    nd = len(shape)
    return pl.BlockSpec(shape, lambda *_: (0,) * nd, pipeline_mode=pl.Buffered(1))


def _params(*sem):
    return pltpu.CompilerParams(dimension_semantics=sem, vmem_limit_bytes=VMEM_LIMIT)


def _nt_dot(a, b):
    return lax.dot_general(a, b, (((1,), (1,)), ((), ())), preferred_element_type=F32)


def _dot(a, b):
    return jnp.dot(a, b, preferred_element_type=F32)


def _ada_kernel(c_ref, w_ref, b_ref, o_ref):
    c = c_ref[...]
    s = c / (1.0 + jnp.exp(-c))
    o_ref[...] = _dot(s.astype(BF16), w_ref[...].astype(BF16)) + b_ref[...]


def _ada(c_pad, w_ada, b_ada):
    n = w_ada.shape[1]
    tn = _tile(n, D_MODEL)
    return pl.pallas_call(
        _ada_kernel,
        grid=(n // tn,),
        in_specs=[
            pl.BlockSpec((ROWS_PAD, D_MODEL), lambda j: (0, 0)),
            pl.BlockSpec((D_MODEL, tn), lambda j: (0, j)),
            pl.BlockSpec((1, tn), lambda j: (0, j)),
        ],
        out_specs=pl.BlockSpec((ROWS_PAD, tn), lambda j: (0, j)),
        out_shape=jax.ShapeDtypeStruct((ROWS_PAD, n), F32),
        compiler_params=_params("arbitrary"),
        name="ada",
    )(c_pad, w_ada, b_ada.reshape(1, n))


def _tables_kernel(inv_ref, sgn_r_ref, sgn_m_ref,
                   cos_r_ref, sin_r_ref, cos_m_ref, sin_m_ref, cos_rt_ref, sin_rt_ref):
    tr = cos_r_ref.shape[0]
    half = V7X_LANES // 2
    pos = (lax.broadcasted_iota(jnp.int32, (tr, V7X_LANES), 0) + pl.program_id(0) * tr).astype(F32)
    ang = pos * inv_ref[...]
    c = jnp.cos(ang)
    s = jnp.sin(ang)
    c_sw = pltpu.roll(c, half, 1)
    s_sw = pltpu.roll(s, half, 1)
    low = lax.broadcasted_iota(jnp.int32, (tr, V7X_LANES), 1) < half
    cos_r = jnp.where(low, c, c_sw)
    sin_r = jnp.where(low, s, s_sw) * sgn_r_ref[...]
    cos_r_ref[...] = cos_r
    sin_r_ref[...] = sin_r
    cos_m_ref[...] = jnp.where(low, c_sw, c)
    sin_m_ref[...] = jnp.where(low, s_sw, s) * sgn_m_ref[...]
    cos_rt_ref[...] = cos_r.T
    sin_rt_ref[...] = sin_r.T


def _rotary_tables(seq):
    half_r = RET_HEAD_DIM // 2
    half_m = MLA_ROPE_DIM // 2
    assert half_r + 2 * half_m == V7X_LANES and RET_HEAD_DIM == V7X_LANES
    inv_r = ROPE_BASE ** (-jnp.arange(0, RET_HEAD_DIM, 2, dtype=F32) / RET_HEAD_DIM)
    inv_m = ROPE_BASE ** (-jnp.arange(0, MLA_ROPE_DIM, 2, dtype=F32) / MLA_ROPE_DIM)
    inv = jnp.concatenate([inv_r, inv_m, inv_m])
    sgn_r = jnp.concatenate([-jnp.ones(half_r, F32), jnp.ones(half_r, F32)])
    sgn_m = jnp.tile(jnp.concatenate([-jnp.ones(half_m, F32), jnp.ones(half_m, F32)]),
                     V7X_LANES // MLA_ROPE_DIM)
    tr = _tile(seq, 1024)
    row = lambda v: v.reshape(1, V7X_LANES)
    tok = jax.ShapeDtypeStruct((seq, V7X_LANES), F32)
    feat = jax.ShapeDtypeStruct((RET_HEAD_DIM, seq), F32)
    vec = pl.BlockSpec((1, V7X_LANES), lambda i: (0, 0))
    tok_spec = pl.BlockSpec((tr, V7X_LANES), lambda i: (i, 0))
    feat_spec = pl.BlockSpec((RET_HEAD_DIM, tr), lambda i: (0, i))
    return pl.pallas_call(
        _tables_kernel,
        grid=(seq // tr,),
        in_specs=[vec, vec, vec],
        out_specs=[tok_spec, tok_spec, tok_spec, tok_spec, feat_spec, feat_spec],
        out_shape=[tok, tok, tok, tok, feat, feat],
        compiler_params=_params("arbitrary"),
        name="tables",
    )(row(inv), row(sgn_r), row(sgn_m))


def _layer_norm(x):
    mu = jnp.mean(x, axis=-1, keepdims=True)
    xc = x - mu
    var = jnp.mean(xc * xc, axis=-1, keepdims=True)
    return xc * lax.rsqrt(var + LN_EPS)


def _rms_norm(x, w):
    return x * lax.rsqrt(jnp.mean(x * x, axis=-1, keepdims=True) + RMS_EPS) * w


def _rope64(x, cos_m, sin_m, first_half):
    swapped = jnp.where(first_half,
                        pltpu.roll(x, V7X_LANES - MLA_ROPE_DIM // 2, 1),
                        pltpu.roll(x, MLA_ROPE_DIM // 2, 1))
    return x * cos_m + swapped * sin_m


def _proj_kernel(x_ref, mod_ref, cos_r_ref, sin_r_ref, cos_m_ref, sin_m_ref, cos_rt_ref, sin_rt_ref,
                 w_in_ref, w_kt_ref, qn_w_ref, w_uqn_ref, w_uqr_ref, kvn_w_ref, w_uk_ref, w_uvt_ref,
                 rq_ref, rkt_ref, rv_ref, rg_ref, q_ref, k_ref, vt_ref):
    sh1 = mod_ref[0, 0:1, :]
    sc1 = mod_ref[0, 1:2, :]
    hd = RET_HEAD_DIM
    k_scale = RET_HEAD_DIM ** -0.5
    q_scale = MLA_SCALE * LOG2E
    n_sub = max(x_ref.shape[1] // PROJ_SUB, 1)
    sub = x_ref.shape[1] // n_sub
    rows = [slice(i * sub, (i + 1) * sub) for i in range(n_sub)]
    lane = lax.broadcasted_iota(jnp.int32, (sub, V7X_LANES), 1)
    first_half = (lane % MLA_ROPE_DIM) < (MLA_ROPE_DIM // 2)
    low_lanes = lane < MLA_ROPE_DIM

    hs = [(_layer_norm(x_ref[0, r, :]) * (1.0 + sc1) + sh1).astype(BF16) for r in rows]

    def retention_part(r, h):
        proj = lambda lo, n: _dot(h, w_in_ref[:, lo:lo + n])
        cos_r = cos_r_ref[r, :]
        sin_r = sin_r_ref[r, :]
        rq = proj(_OFF_RQ, RET_WIDTH)
        for hh in range(RET_HEADS):
            t = rq[:, hh * hd:(hh + 1) * hd]
            rq_ref[0, r, hh * hd:(hh + 1) * hd] = (t * cos_r + pltpu.roll(t, hd // 2, 1) * sin_r).astype(BF16)
        cos_rt = cos_rt_ref[:, r]
        sin_rt = sin_rt_ref[:, r]
        rkt = _nt_dot(w_kt_ref[...], h)
        chunk0 = r.start // RET_CHUNK
        for hh in range(RET_HEADS):
            t = rkt[hh * hd:(hh + 1) * hd, :]
            rot = ((t * cos_rt + pltpu.roll(t, hd // 2, 0) * sin_rt) * k_scale).astype(BF16)
            for j in range(sub // RET_CHUNK):
                rkt_ref[0, chunk0 + j, hh * hd:(hh + 1) * hd, :] = rot[:, j * RET_CHUNK:(j + 1) * RET_CHUNK]
        rv_ref[0, r, :] = proj(_OFF_RV, RET_WIDTH).astype(BF16)
        g = proj(_OFF_RG, RET_WIDTH)
        rg_ref[0, r, :] = (g / (1.0 + jnp.exp(-g))).astype(BF16)

    def mla_part(r, h):
        proj = lambda lo, n: _dot(h, w_in_ref[:, lo:lo + n])
        cos_m = cos_m_ref[r, :]
        sin_m = sin_m_ref[r, :]
        cq = _rms_norm(proj(_OFF_CQ, Q_LORA), qn_w_ref[...]).astype(BF16)
        ckv = _rms_norm(proj(_OFF_CKV, KV_LORA), kvn_w_ref[...]).astype(BF16)
        k_pe = _rope64(proj(_OFF_KR, V7X_LANES), cos_m, sin_m, first_half)
        q_nope = _dot(cq, w_uqn_ref[...]) * q_scale
        q_rope = _dot(cq, w_uqr_ref[...]) * q_scale
        k_nope = _dot(ckv, w_uk_ref[...])
        v_t = _nt_dot(w_uvt_ref[...], ckv)
        for pair in range(MLA_HEADS // 2):
            rp = _rope64(q_rope[:, pair * V7X_LANES:(pair + 1) * V7X_LANES], cos_m, sin_m, first_half)
            for side in range(2):
                hh = 2 * pair + side
                keep = low_lanes if side == 0 else jnp.logical_not(low_lanes)
                q_ref[0, hh, r, 0:MLA_NOPE_DIM] = q_nope[:, hh * MLA_NOPE_DIM:(hh + 1) * MLA_NOPE_DIM].astype(BF16)
                q_ref[0, hh, r, MLA_NOPE_DIM:QK_PAD] = jnp.where(keep, rp, 0.0).astype(BF16)
        k_pe_hi = pltpu.roll(k_pe, MLA_ROPE_DIM, 1)
        for hh in range(MLA_HEADS):
            k_ref[0, hh, r, 0:MLA_NOPE_DIM] = k_nope[:, hh * MLA_NOPE_DIM:(hh + 1) * MLA_NOPE_DIM].astype(BF16)
            k_ref[0, hh, r, MLA_NOPE_DIM:QK_PAD] = (k_pe if hh % 2 == 0 else k_pe_hi).astype(BF16)
            vt_ref[0, hh, :, r] = v_t[hh * MLA_V_DIM:(hh + 1) * MLA_V_DIM, :].astype(BF16)

    for r, h in zip(rows, hs):
        retention_part(r, h)
    for r, h in zip(rows, hs):
        mla_part(r, h)


PROJ_TILE = 1024
PROJ_SUB = 256


def _proj(x, mod, mod_row0, tables, w):
    b, s, _ = x.shape
    tm = _tile(s, PROJ_TILE)
    cos_r, sin_r, cos_m, sin_m, cos_rt, sin_rt = tables
    tok_tab = pl.BlockSpec((tm, V7X_LANES), lambda bi, i: (i, 0))
    feat_tab = pl.BlockSpec((RET_HEAD_DIM, tm), lambda bi, i: (0, i))
    tok_out = lambda: pl.BlockSpec((1, tm, RET_WIDTH), lambda bi, i: (bi, i, 0))
    out_shape = [
        jax.ShapeDtypeStruct((b, s, RET_WIDTH), BF16),
        jax.ShapeDtypeStruct((b, s // RET_CHUNK, RET_WIDTH, RET_CHUNK), BF16),
        jax.ShapeDtypeStruct((b, s, RET_WIDTH), BF16),
        jax.ShapeDtypeStruct((b, s, RET_WIDTH), BF16),
        jax.ShapeDtypeStruct((b, MLA_HEADS, s, QK_PAD), BF16),
        jax.ShapeDtypeStruct((b, MLA_HEADS, s, QK_PAD), BF16),
        jax.ShapeDtypeStruct((b, MLA_HEADS, MLA_V_DIM, s), BF16),
    ]
    out_specs = [
        tok_out(),
        pl.BlockSpec((1, tm // RET_CHUNK, RET_WIDTH, RET_CHUNK), lambda bi, i: (bi, i, 0, 0)),
        tok_out(),
        tok_out(),
        pl.BlockSpec((1, MLA_HEADS, tm, QK_PAD), lambda bi, i: (bi, 0, i, 0)),
        pl.BlockSpec((1, MLA_HEADS, tm, QK_PAD), lambda bi, i: (bi, 0, i, 0)),
        pl.BlockSpec((1, MLA_HEADS, MLA_V_DIM, tm), lambda bi, i: (bi, 0, 0, i)),
    ]
    in_specs = [
        pl.BlockSpec((1, tm, D_MODEL), lambda bi, i: (bi, i, 0)),
        pl.BlockSpec((1, N_MOD, D_MODEL), lambda bi, i: (bi + mod_row0, 0, 0)),
        tok_tab, tok_tab, tok_tab, tok_tab, feat_tab, feat_tab,
        _const_spec(w["w_in"].shape), _const_spec(w["w_kt"].shape),
        _const_spec(w["q_norm_w"].shape), _const_spec(w["w_uqn"].shape), _const_spec(w["w_uqr"].shape),
        _const_spec(w["kv_norm_w"].shape), _const_spec(w["w_uk"].shape), _const_spec(w["w_uvt"].shape),
    ]
    return pl.pallas_call(
        _proj_kernel,
        grid=(b, s // tm),
        in_specs=in_specs,
        out_specs=out_specs,
        out_shape=out_shape,
        compiler_params=_params("arbitrary", "arbitrary"),
        name="proj",
    )(x, mod, cos_r, sin_r, cos_m, sin_m, cos_rt, sin_rt,
      w["w_in"], w["w_kt"], w["q_norm_w"], w["w_uqn"], w["w_uqr"], w["kv_norm_w"], w["w_uk"], w["w_uvt"])


def _log_sigmoid(x):
    return jnp.minimum(x, 0.0) - jnp.log(1.0 + jnp.exp(-jnp.abs(x)))


def _ret_decay_tables(dec_row, reverse, d_in_ref, d_q_ref, d_k_ref, d_c_ref, hh):
    c = RET_CHUNK
    lg = _log_sigmoid(dec_row)[:, 0:1]
    row = lax.broadcasted_iota(jnp.int32, (c, c), 0)
    col = lax.broadcasted_iota(jnp.int32, (c, c), 1)
    rel = (col - row) if reverse else (row - col)
    mask = (rel > 0) if reverse else (rel >= 0)
    relf = jnp.maximum(rel, 0).astype(F32)
    d_in_ref[hh] = jnp.where(mask, jnp.exp(lg * relf), 0.0)
    qi = lax.broadcasted_iota(jnp.int32, (c, V7X_LANES), 0).astype(F32)
    kj = lax.broadcasted_iota(jnp.int32, (RET_HEAD_DIM, c), 1).astype(F32)
    q_steps = (c - qi) if reverse else (qi + 1.0)
    k_steps = kj if reverse else (c - 1.0 - kj)
    d_q_ref[hh] = jnp.exp(lg * q_steps)
    d_k_ref[hh] = jnp.exp(lg * k_steps)
    d_c_ref[hh] = jnp.exp(lg * float(c)) + jnp.zeros((V7X_SUBLANES, V7X_LANES), F32)


def _ret_kernel(reverse, *refs):
    if reverse:
        (dec_ref, rq_ref, rkt_ref, rv_ref, fwd_ref, rg_ref, gnw_ref, out_ref,
         state_ref, d_in_ref, d_q_ref, d_k_ref, d_c_ref) = refs
    else:
        (dec_ref, rq_ref, rkt_ref, rv_ref, out_ref,
         state_ref, d_in_ref, d_q_ref, d_k_ref, d_c_ref) = refs
    c = RET_CHUNK
    hd = RET_HEAD_DIM
    n_chunks = rq_ref.shape[1] // c

    @pl.when(pl.program_id(1) == 0)
    def _():
        state_ref[...] = jnp.zeros_like(state_ref)
        for hh in range(RET_HEADS):
            _ret_decay_tables(dec_ref[hh:hh + 1, :], reverse, d_in_ref, d_q_ref, d_k_ref, d_c_ref, hh)

    def chunk_body(ci, carry):
        cidx = (n_chunks - 1 - ci) if reverse else ci
        t0 = pl.multiple_of(cidx * c, c)
        for hh in range(RET_HEADS):
            cols = slice(hh * hd, (hh + 1) * hd)
            q = rq_ref[0, pl.ds(t0, c), cols]
            kt = rkt_ref[0, cidx, cols, :]
            v = rv_ref[0, pl.ds(t0, c), cols]
            state = state_ref[hh]
            s = _dot(q, kt) * d_in_ref[hh]
            inner = _dot(s.astype(BF16), v)
            cross = _dot(q, state.astype(BF16)) * d_q_ref[hh]
            kt_dec = (kt.astype(F32) * d_k_ref[hh]).astype(BF16)
            state_ref[hh] = state * d_c_ref[hh][0:1, :] + _dot(kt_dec, v)
            y = inner + cross
            if reverse:
                y = y + fwd_ref[0, pl.ds(t0, c), cols]
                mu = jnp.mean(y, axis=-1, keepdims=True)
                yc = y - mu
                var = jnp.mean(yc * yc, axis=-1, keepdims=True)
                yn = yc * lax.rsqrt(var + LN_EPS) * gnw_ref[:, cols]
                gate = rg_ref[0, pl.ds(t0, c), cols].astype(F32)
                out_ref[0, pl.ds(t0, c), cols] = (gate * yn).astype(out_ref.dtype)
            else:
                out_ref[0, pl.ds(t0, c), cols] = y
        return carry

    lax.fori_loop(0, n_chunks, chunk_body, 0, unroll=True)


def _retention(dec_rows, rq, rkt, rv, reverse, fwd=None, rg=None, gn_w=None):
    b, s, _ = rq.shape
    tb = _tile(s, 1024)
    nb = s // tb
    blk = (lambda i: nb - 1 - i) if reverse else (lambda i: i)
    tok = pl.BlockSpec((1, tb, RET_WIDTH), lambda bi, i: (bi, blk(i), 0))
    feat = pl.BlockSpec((1, tb // RET_CHUNK, RET_WIDTH, RET_CHUNK), lambda bi, i: (bi, blk(i), 0, 0))
    dec_spec = pl.BlockSpec((RET_HEADS, V7X_LANES), lambda bi, i: (0, 0))
    in_specs = [dec_spec, tok, feat, tok]
    args = [dec_rows, rq, rkt, rv]
    if reverse:
        in_specs += [tok, tok, pl.BlockSpec((1, RET_WIDTH), lambda bi, i: (0, 0))]
        args += [fwd, rg, gn_w]
    c = RET_CHUNK
    scratch = [
        pltpu.VMEM((RET_HEADS, RET_HEAD_DIM, RET_HEAD_DIM), F32),
        pltpu.VMEM((RET_HEADS, c, c), F32),
        pltpu.VMEM((RET_HEADS, c, V7X_LANES), F32),
        pltpu.VMEM((RET_HEADS, RET_HEAD_DIM, c), F32),
        pltpu.VMEM((RET_HEADS, V7X_SUBLANES, V7X_LANES), F32),
    ]
    return pl.pallas_call(
        functools.partial(_ret_kernel, reverse),
        grid=(b, nb),
        in_specs=in_specs,
        out_specs=tok,
        out_shape=jax.ShapeDtypeStruct((b, s, RET_WIDTH), BF16 if reverse else F32),
        scratch_shapes=scratch,
        compiler_params=_params("arbitrary", "arbitrary"),
        name="ret_bwd" if reverse else "ret_fwd",
    )(*args)


NEG_BIG = -1e30


ATTN_GROUP = 2 * V7X_LANES
ATTN_LAG = 3


def _attn_kernel(q_ref, k_ref, vprev_ref, vcur_ref, o_ref, s_ref, m_old_ref, m_cur_ref, l_ref, acc_ref):
    t = pl.program_id(3)
    nk = pl.num_programs(3) - 1
    gw = ATTN_GROUP
    ng = q_ref.shape[2] // gw
    lag = min(ATTN_LAG, ng - 1)
    first_carried = ng - lag
    group = lambda g: slice(g * gw, (g + 1) * gw)

    def softmax_pv(g, vt, m_prev, m_new):
        cols = group(g)
        alpha = jnp.exp2(m_prev - m_new)
        p = jnp.exp2(s_ref[:, cols] - m_new)
        l_ref[:, cols] = alpha * l_ref[:, cols] + jnp.sum(p, axis=0, keepdims=True)
        acc_ref[:, cols] = alpha * acc_ref[:, cols] + _dot(vt, p.astype(BF16))

    def scores(g, m_prev):
        cols = group(g)
        s = _nt_dot(k_ref[0, 0], q_ref[0, 0, cols, :])
        s_ref[:, cols] = s
        return jnp.maximum(m_prev, jnp.max(s, axis=0, keepdims=True))

    def carried_pv(g):
        cols = group(g)
        softmax_pv(g, vprev_ref[0, 0], m_old_ref[:, cols], m_cur_ref[:, cols])

    def step(has_prev):
        m_before, m_after = {}, {}
        for j in range(ng):
            cols = group(j)
            m_before[j] = m_cur_ref[:, cols]
            m_after[j] = scores(j, m_before[j])
            if j >= first_carried:
                m_old_ref[:, cols] = m_before[j]
                m_cur_ref[:, cols] = m_after[j]
            d = j - lag
            if d >= 0:
                softmax_pv(d, vcur_ref[0, 0], m_before[d], m_after[d])
                m_cur_ref[:, group(d)] = m_after[d]
            elif has_prev:
                carried_pv(ng + d)

    @pl.when(t == 0)
    def _():
        m_cur_ref[...] = jnp.full_like(m_cur_ref, NEG_BIG)
        l_ref[...] = jnp.zeros_like(l_ref)
        acc_ref[...] = jnp.zeros_like(acc_ref)
        step(False)

    @pl.when(jnp.logical_and(t > 0, t < nk))
    def _():
        step(True)

    @pl.when(t == nk)
    def _():
        for g in range(first_carried, ng):
            carried_pv(g)
        o = acc_ref[...] / l_ref[...]
        o_ref[0] = o.T.astype(o_ref.dtype)


def _attention(q, k, vt):
    b, nh, s, _ = q.shape
    tq = _tile(s, 4096)
    tk = _tile(s, 2048)
    nk = s // tk
    return pl.pallas_call(
        _attn_kernel,
        grid=(b, nh, s // tq, nk + 1),
        in_specs=[
            pl.BlockSpec((1, 1, tq, QK_PAD), lambda bi, hi, qi, t: (bi, hi, qi, 0)),
            pl.BlockSpec((1, 1, tk, QK_PAD), lambda bi, hi, qi, t: (bi, hi, jnp.minimum(t, nk - 1), 0)),
            pl.BlockSpec((1, 1, MLA_V_DIM, tk), lambda bi, hi, qi, t: (bi, hi, 0, jnp.maximum(t - 1, 0))),
            pl.BlockSpec((1, 1, MLA_V_DIM, tk), lambda bi, hi, qi, t: (bi, hi, 0, jnp.minimum(t, nk - 1))),
        ],
        out_specs=pl.BlockSpec((1, tq, MLA_V_DIM), lambda bi, hi, qi, t: (bi, qi, hi)),
        out_shape=jax.ShapeDtypeStruct((b, s, MLA_WIDTH), BF16),
        scratch_shapes=[
            pltpu.VMEM((tk, tq), F32),
            pltpu.VMEM((1, tq), F32),
            pltpu.VMEM((1, tq), F32),
            pltpu.VMEM((1, tq), F32),
            pltpu.VMEM((MLA_V_DIM, tq), F32),
        ],
        compiler_params=_params("arbitrary", "arbitrary", "arbitrary", "arbitrary"),
        name="attn",
    )(q, k, vt, vt)


FF_CHUNK = 1024
MLP_TILE = 1024
MLP_SUB = 256


def _mlp_kernel(alpha, x_ref, ret_ref, att_ref, mod_ref, w_o_ref, ln1_w_ref, ln1_b_ref,
                w_up_ref, w_down_ref, ln2_w_ref, ln2_b_ref, o_ref):
    g1 = mod_ref[0, 2:3, :]
    sh2 = mod_ref[0, 3:4, :]
    sc2 = mod_ref[0, 4:5, :]
    g2 = mod_ref[0, 5:6, :]
    n_sub = max(x_ref.shape[1] // MLP_SUB, 1)
    sub = x_ref.shape[1] // n_sub
    rows = [slice(i * sub, (i + 1) * sub) for i in range(n_sub)]
    mixes = [_dot(ret_ref[0, r, :], w_o_ref[0:RET_WIDTH, :]) + _dot(att_ref[0, r, :], w_o_ref[RET_WIDTH:, :])
             for r in rows]
    x1s, hs = [], []
    for r, mix in zip(rows, mixes):
        x1 = _layer_norm(alpha * x_ref[0, r, :] + g1 * mix) * ln1_w_ref[...] + ln1_b_ref[...]
        x1s.append(x1)
        hs.append((_layer_norm(x1) * (1.0 + sc2) + sh2).astype(BF16))
    for r, x1, h in zip(rows, x1s, hs):
        ff = jnp.zeros_like(x1)
        for c0 in range(0, D_FF, FF_CHUNK):
            u = jnp.maximum(_dot(h, w_up_ref[:, c0:c0 + FF_CHUNK]), 0.0)
            ff = ff + _dot((u * u).astype(BF16), w_down_ref[c0:c0 + FF_CHUNK, :])
        o_ref[0, r, :] = _layer_norm(alpha * x1 + g2 * ff) * ln2_w_ref[...] + ln2_b_ref[...]


def _mlp(alpha, x, ret, att, mod, mod_row0, w):
    b, s, _ = x.shape
    tm = _tile(s, MLP_TILE)
    xs = pl.BlockSpec((1, tm, D_MODEL), lambda bi, i: (bi, i, 0))
    half = pl.BlockSpec((1, tm, RET_WIDTH), lambda bi, i: (bi, i, 0))
    vec = lambda: pl.BlockSpec((1, D_MODEL), lambda bi, i: (0, 0))
    return pl.pallas_call(
        functools.partial(_mlp_kernel, alpha),
        grid=(b, s // tm),
        in_specs=[
            xs, half, half,
            pl.BlockSpec((1, N_MOD, D_MODEL), lambda bi, i: (bi + mod_row0, 0, 0)),
            _const_spec(w["w_o"].shape), vec(), vec(),
            _const_spec(w["w_up"].shape), _const_spec(w["w_down"].shape), vec(), vec(),
        ],
        out_specs=xs,
        out_shape=jax.ShapeDtypeStruct((b, s, D_MODEL), F32),
        compiler_params=_params("arbitrary", "arbitrary"),
        name="mlp",
    )(x, ret, att, mod, w["w_o"], w["ln1_w"], w["ln1_b"], w["w_up"], w["w_down"], w["ln2_w"], w["ln2_b"])


def _prep_layer_weights(w_in, ret_decay_f, ret_decay_b, ret_gn_w, q_norm_w, w_uq, kv_norm_w, w_ukv,
                        w_o, ln1_w, ln1_b, w_up, w_down, ln2_w, ln2_b):
    rw = RET_WIDTH
    w_rq, w_rk, w_rest = w_in[:, :rw], w_in[:, rw:2 * rw], w_in[:, 2 * rw:]
    pad = jnp.zeros((D_MODEL, _W_IN_COLS - (w_in.shape[1] - rw)), w_in.dtype)
    w_in_p = jnp.concatenate([w_rq, w_rest, pad], axis=1).astype(BF16)
    uq = w_uq.reshape(Q_LORA, MLA_HEADS, MLA_QK_DIM)
    ukv = w_ukv.reshape(KV_LORA, MLA_HEADS, MLA_NOPE_DIM + MLA_V_DIM)
    lanes = lambda v: jnp.broadcast_to(v.reshape(RET_HEADS, 1), (RET_HEADS, V7X_LANES)).astype(F32)
    return {
        "w_in": w_in_p,
        "w_kt": w_rk.T.astype(BF16),
        "q_norm_w": q_norm_w.reshape(1, Q_LORA),
        "w_uqn": uq[:, :, :MLA_NOPE_DIM].reshape(Q_LORA, MLA_HEADS * MLA_NOPE_DIM).astype(BF16),
        "w_uqr": uq[:, :, MLA_NOPE_DIM:].reshape(Q_LORA, MLA_HEADS * MLA_ROPE_DIM).astype(BF16),
        "kv_norm_w": kv_norm_w.reshape(1, KV_LORA),
        "w_uk": ukv[:, :, :MLA_NOPE_DIM].reshape(KV_LORA, MLA_HEADS * MLA_NOPE_DIM).astype(BF16),
        "w_uvt": ukv[:, :, MLA_NOPE_DIM:].reshape(KV_LORA, MLA_HEADS * MLA_V_DIM).T.astype(BF16),
        "dec_f": lanes(ret_decay_f),
        "dec_b": lanes(ret_decay_b),
        "gn_w": ret_gn_w.reshape(1, RET_WIDTH),
        "w_o": w_o.astype(BF16),
        "ln1_w": ln1_w.reshape(1, D_MODEL), "ln1_b": ln1_b.reshape(1, D_MODEL),
        "w_up": w_up.astype(BF16), "w_down": w_down.astype(BF16),
        "ln2_w": ln2_w.reshape(1, D_MODEL), "ln2_b": ln2_b.reshape(1, D_MODEL),
    }


def _layer(alpha, x, mod, mod_row0, tables, w):
    rq, rkt, rv, rg, q, k, vt = _proj(x, mod, mod_row0, tables, w)
    fwd = _retention(w["dec_f"], rq, rkt, rv, reverse=False)
    ret = _retention(w["dec_b"], rq, rkt, rv, reverse=True, fwd=fwd, rg=rg, gn_w=w["gn_w"])
    att = _attention(q, k, vt)
    return _mlp(alpha, x, ret, att, mod, mod_row0, w)


def _trunks(xs, cs, w_ada, b_ada, layer_weights):
    depth = w_ada.shape[0]
    alpha = float((2 * depth) ** 0.25)
    n_rows = sum(c.shape[0] for c in cs)
    assert n_rows <= ROWS_PAD
    c_pad = jnp.concatenate(list(cs) + [jnp.zeros((ROWS_PAD - n_rows, D_MODEL), F32)], axis=0)
    tables = _rotary_tables(max(x.shape[1] for x in xs))
    outs = list(xs)
    for l in range(depth):
        mod = _ada(c_pad, w_ada[l], b_ada[l]).reshape(ROWS_PAD, N_MOD, D_MODEL)
        w = _prep_layer_weights(*[t[l] for t in layer_weights])
        row0 = 0
        for i, c in enumerate(cs):
            outs[i] = _layer(alpha, outs[i], mod, row0, tables, w)
            row0 += c.shape[0]
    return tuple(outs)


def kernel(x_prompt, x_sample, c_prompt, c_sample, w_ada, b_ada, w_in, ret_decay_f, ret_decay_b, ret_gn_w,
           q_norm_w, w_uq, kv_norm_w, w_ukv, w_o, ln1_w, ln1_b, w_up, w_down, ln2_w, ln2_b):
    layer_weights = (w_in, ret_decay_f, ret_decay_b, ret_gn_w, q_norm_w, w_uq, kv_norm_w, w_ukv,
                     w_o, ln1_w, ln1_b, w_up, w_down, ln2_w, ln2_b)
    return _trunks((x_prompt, x_sample), (c_prompt, c_sample), w_ada, b_ada, layer_weights)
```

```python
import functools
import math

import jax
import jax.numpy as jnp
from jax import lax
from jax.experimental import pallas as pl
from jax.experimental.pallas import tpu as pltpu

F32 = jnp.float32
BF16 = jnp.bfloat16

D_MODEL = 1024
RET_HEADS = 4
RET_HEAD_DIM = 128
RET_WIDTH = RET_HEADS * RET_HEAD_DIM
MLA_HEADS = 4
MLA_NOPE_DIM = 128
MLA_ROPE_DIM = 64
MLA_V_DIM = 128
MLA_WIDTH = MLA_HEADS * MLA_V_DIM
MLA_QK_DIM = MLA_NOPE_DIM + MLA_ROPE_DIM
Q_LORA = 384
KV_LORA = 256
D_FF = 4 * D_MODEL
ROPE_BASE = 10000.0
LN_EPS = 1e-5
RMS_EPS = 1e-6
N_MOD = 6
MLA_SCALE = float(MLA_QK_DIM ** -0.5)
LOG2E = 1.4426950408889634

V7X_LANES = 128
V7X_SUBLANES = 8
V7X_VMEM_BYTES = 64 * 1024 * 1024
VMEM_LIMIT = V7X_VMEM_BYTES - 8 * 1024 * 1024

QK_PAD = 2 * V7X_LANES
_OFF_RQ, _OFF_RV, _OFF_RG = 0, RET_WIDTH, 2 * RET_WIDTH
_OFF_CQ = 3 * RET_WIDTH
_OFF_CKV = _OFF_CQ + Q_LORA
_OFF_KR = _OFF_CKV + KV_LORA
_W_IN_COLS = _OFF_KR + V7X_LANES
ROWS_PAD = V7X_SUBLANES
RET_CHUNK = 128


def _tile(n, pref):
    t = min(n, pref)
    assert n % t == 0, (n, t)
    return t


def _const_spec(shape):
    nd = len(shape)
    return pl.BlockSpec(shape, lambda *_: (0,) * nd, pipeline_mode=pl.Buffered(1))


def _params(*sem):
    return pltpu.CompilerParams(dimension_semantics=sem, vmem_limit_bytes=VMEM_LIMIT)


def _nt_dot(a, b):
    return lax.dot_general(a, b, (((1,), (1,)), ((), ())), preferred_element_type=F32)


def _dot(a, b):
    return jnp.dot(a, b, preferred_element_type=F32)


def _ada_kernel(c_ref, w_ref, b_ref, o_ref):
    c = c_ref[...]
    s = c / (1.0 + jnp.exp(-c))
    o_ref[...] = _dot(s.astype(BF16), w_ref[...].astype(BF16)) + b_ref[...]


def _ada(c_pad, w_ada, b_ada):
    n = w_ada.shape[1]
    tn = _tile(n, D_MODEL)
    return pl.pallas_call(
        _ada_kernel,
        grid=(n // tn,),
        in_specs=[
            pl.BlockSpec((ROWS_PAD, D_MODEL), lambda j: (0, 0)),
            pl.BlockSpec((D_MODEL, tn), lambda j: (0, j)),
            pl.BlockSpec((1, tn), lambda j: (0, j)),
        ],
        out_specs=pl.BlockSpec((ROWS_PAD, tn), lambda j: (0, j)),
        out_shape=jax.ShapeDtypeStruct((ROWS_PAD, n), F32),
        compiler_params=_params("arbitrary"),
        name="ada",
    )(c_pad, w_ada, b_ada.reshape(1, n))


def _tables_kernel(inv_ref, sgn_r_ref, sgn_m_ref,
                   cos_r_ref, sin_r_ref, cos_m_ref, sin_m_ref, cos_rt_ref, sin_rt_ref):
    tr = cos_r_ref.shape[0]
    half = V7X_LANES // 2
    pos = (lax.broadcasted_iota(jnp.int32, (tr, V7X_LANES), 0) + pl.program_id(0) * tr).astype(F32)
    ang = pos * inv_ref[...]
    c = jnp.cos(ang)
    s = jnp.sin(ang)
    c_sw = pltpu.roll(c, half, 1)
    s_sw = pltpu.roll(s, half, 1)
    low = lax.broadcasted_iota(jnp.int32, (tr, V7X_LANES), 1) < half
    cos_r = jnp.where(low, c, c_sw)
    sin_r = jnp.where(low, s, s_sw) * sgn_r_ref[...]
    cos_r_ref[...] = cos_r
    sin_r_ref[...] = sin_r
    cos_m_ref[...] = jnp.where(low, c_sw, c)
    sin_m_ref[...] = jnp.where(low, s_sw, s) * sgn_m_ref[...]
    cos_rt_ref[...] = cos_r.T
    sin_rt_ref[...] = sin_r.T


def _rotary_tables(seq):
    half_r = RET_HEAD_DIM // 2
    half_m = MLA_ROPE_DIM // 2
    assert half_r + 2 * half_m == V7X_LANES and RET_HEAD_DIM == V7X_LANES
    inv_r = ROPE_BASE ** (-jnp.arange(0, RET_HEAD_DIM, 2, dtype=F32) / RET_HEAD_DIM)
    inv_m = ROPE_BASE ** (-jnp.arange(0, MLA_ROPE_DIM, 2, dtype=F32) / MLA_ROPE_DIM)
    inv = jnp.concatenate([inv_r, inv_m, inv_m])
    sgn_r = jnp.concatenate([-jnp.ones(half_r, F32), jnp.ones(half_r, F32)])
    sgn_m = jnp.tile(jnp.concatenate([-jnp.ones(half_m, F32), jnp.ones(half_m, F32)]),
                     V7X_LANES // MLA_ROPE_DIM)
    tr = _tile(seq, 1024)
    row = lambda v: v.reshape(1, V7X_LANES)
    tok = jax.ShapeDtypeStruct((seq, V7X_LANES), F32)
    feat = jax.ShapeDtypeStruct((RET_HEAD_DIM, seq), F32)
    vec = pl.BlockSpec((1, V7X_LANES), lambda i: (0, 0))
    tok_spec = pl.BlockSpec((tr, V7X_LANES), lambda i: (i, 0))
    feat_spec = pl.BlockSpec((RET_HEAD_DIM, tr), lambda i: (0, i))
    return pl.pallas_call(
        _tables_kernel,
        grid=(seq // tr,),
        in_specs=[vec, vec, vec],
        out_specs=[tok_spec, tok_spec, tok_spec, tok_spec, feat_spec, feat_spec],
        out_shape=[tok, tok, tok, tok, feat, feat],
        compiler_params=_params("arbitrary"),
        name="tables",
    )(row(inv), row(sgn_r), row(sgn_m))


def _layer_norm(x):
    mu = jnp.mean(x, axis=-1, keepdims=True)
    xc = x - mu
    var = jnp.mean(xc * xc, axis=-1, keepdims=True)
    return xc * lax.rsqrt(var + LN_EPS)


def _rms_norm(x, w):
    return x * lax.rsqrt(jnp.mean(x * x, axis=-1, keepdims=True) + RMS_EPS) * w


def _rope64(x, cos_m, sin_m, first_half):
    swapped = jnp.where(first_half,
                        pltpu.roll(x, V7X_LANES - MLA_ROPE_DIM // 2, 1),
                        pltpu.roll(x, MLA_ROPE_DIM // 2, 1))
    return x * cos_m + swapped * sin_m


def _proj_kernel(x_ref, mod_ref, cos_r_ref, sin_r_ref, cos_m_ref, sin_m_ref, cos_rt_ref, sin_rt_ref,
                 w_in_ref, w_kt_ref, qn_w_ref, w_uqn_ref, w_uqr_ref, kvn_w_ref, w_uk_ref, w_uvt_ref,
                 rq_ref, rkt_ref, rv_ref, rg_ref, q_ref, k_ref, vt_ref):
    sh1 = mod_ref[0, 0:1, :]
    sc1 = mod_ref[0, 1:2, :]
    hd = RET_HEAD_DIM
    k_scale = RET_HEAD_DIM ** -0.5
    q_scale = MLA_SCALE * LOG2E
    n_sub = max(x_ref.shape[1] // PROJ_SUB, 1)
    sub = x_ref.shape[1] // n_sub
    rows = [slice(i * sub, (i + 1) * sub) for i in range(n_sub)]
    lane = lax.broadcasted_iota(jnp.int32, (sub, V7X_LANES), 1)
    first_half = (lane % MLA_ROPE_DIM) < (MLA_ROPE_DIM // 2)
    low_lanes = lane < MLA_ROPE_DIM

    hs = [(_layer_norm(x_ref[0, r, :]) * (1.0 + sc1) + sh1).astype(BF16) for r in rows]

    def retention_part(r, h):
        proj = lambda lo, n: _dot(h, w_in_ref[:, lo:lo + n])
        cos_r = cos_r_ref[r, :]
        sin_r = sin_r_ref[r, :]
        rq = proj(_OFF_RQ, RET_WIDTH)
        for hh in range(RET_HEADS):
            t = rq[:, hh * hd:(hh + 1) * hd]
            rq_ref[0, r, hh * hd:(hh + 1) * hd] = (t * cos_r + pltpu.roll(t, hd // 2, 1) * sin_r).astype(BF16)
        cos_rt = cos_rt_ref[:, r]
        sin_rt = sin_rt_ref[:, r]
        rkt = _nt_dot(w_kt_ref[...], h)
        chunk0 = r.start // RET_CHUNK
        for hh in range(RET_HEADS):
            t = rkt[hh * hd:(hh + 1) * hd, :]
            rot = ((t * cos_rt + pltpu.roll(t, hd // 2, 0) * sin_rt) * k_scale).astype(BF16)
            for j in range(sub // RET_CHUNK):
                rkt_ref[0, chunk0 + j, hh * hd:(hh + 1) * hd, :] = rot[:, j * RET_CHUNK:(j + 1) * RET_CHUNK]
        rv_ref[0, r, :] = proj(_OFF_RV, RET_WIDTH).astype(BF16)
        g = proj(_OFF_RG, RET_WIDTH)
        rg_ref[0, r, :] = (g / (1.0 + jnp.exp(-g))).astype(BF16)

    def mla_latents(r, h):
        proj = lambda lo, n: _dot(h, w_in_ref[:, lo:lo + n])
        cq = _rms_norm(proj(_OFF_CQ, Q_LORA), qn_w_ref[...]).astype(BF16)
        ckv = _rms_norm(proj(_OFF_CKV, KV_LORA), kvn_w_ref[...]).astype(BF16)
        k_pe = _rope64(proj(_OFF_KR, V7X_LANES), cos_m_ref[r, :], sin_m_ref[r, :], first_half)
        return cq, ckv, k_pe

    def mla_heads(r, cq, ckv, k_pe):
        cos_m = cos_m_ref[r, :]
        sin_m = sin_m_ref[r, :]
        q_nope = _dot(cq, w_uqn_ref[...]) * q_scale
        q_rope = _dot(cq, w_uqr_ref[...]) * q_scale
        k_nope = _dot(ckv, w_uk_ref[...])
        v_t = _nt_dot(w_uvt_ref[...], ckv)
        for pair in range(MLA_HEADS // 2):
            rp = _rope64(q_rope[:, pair * V7X_LANES:(pair + 1) * V7X_LANES], cos_m, sin_m, first_half)
            for side in range(2):
                hh = 2 * pair + side
                keep = low_lanes if side == 0 else jnp.logical_not(low_lanes)
                q_ref[0, hh, r, 0:MLA_NOPE_DIM] = q_nope[:, hh * MLA_NOPE_DIM:(hh + 1) * MLA_NOPE_DIM].astype(BF16)
                q_ref[0, hh, r, MLA_NOPE_DIM:QK_PAD] = jnp.where(keep, rp, 0.0).astype(BF16)
        k_pe_hi = pltpu.roll(k_pe, MLA_ROPE_DIM, 1)
        for hh in range(MLA_HEADS):
            k_ref[0, hh, r, 0:MLA_NOPE_DIM] = k_nope[:, hh * MLA_NOPE_DIM:(hh + 1) * MLA_NOPE_DIM].astype(BF16)
            k_ref[0, hh, r, MLA_NOPE_DIM:QK_PAD] = (k_pe if hh % 2 == 0 else k_pe_hi).astype(BF16)
            vt_ref[0, hh, :, r] = v_t[hh * MLA_V_DIM:(hh + 1) * MLA_V_DIM, :].astype(BF16)

    for r, h in zip(rows, hs):
        retention_part(r, h)
    latents = [mla_latents(r, h) for r, h in zip(rows, hs)]
    for r, lat in zip(rows, latents):
        mla_heads(r, *lat)


PROJ_TILE = 1024
PROJ_SUB = 256


def _proj(x, mod, mod_row0, tables, w):
    b, s, _ = x.shape
    tm = _tile(s, PROJ_TILE)
    cos_r, sin_r, cos_m, sin_m, cos_rt, sin_rt = tables
    tok_tab = pl.BlockSpec((tm, V7X_LANES), lambda bi, i: (i, 0))
    feat_tab = pl.BlockSpec((RET_HEAD_DIM, tm), lambda bi, i: (0, i))
    tok_out = lambda: pl.BlockSpec((1, tm, RET_WIDTH), lambda bi, i: (bi, i, 0))
    out_shape = [
        jax.ShapeDtypeStruct((b, s, RET_WIDTH), BF16),
        jax.ShapeDtypeStruct((b, s // RET_CHUNK, RET_WIDTH, RET_CHUNK), BF16),
        jax.ShapeDtypeStruct((b, s, RET_WIDTH), BF16),
        jax.ShapeDtypeStruct((b, s, RET_WIDTH), BF16),
        jax.ShapeDtypeStruct((b, MLA_HEADS, s, QK_PAD), BF16),
        jax.ShapeDtypeStruct((b, MLA_HEADS, s, QK_PAD), BF16),
        jax.ShapeDtypeStruct((b, MLA_HEADS, MLA_V_DIM, s), BF16),
    ]
    out_specs = [
        tok_out(),
        pl.BlockSpec((1, tm // RET_CHUNK, RET_WIDTH, RET_CHUNK), lambda bi, i: (bi, i, 0, 0)),
        tok_out(),
        tok_out(),
        pl.BlockSpec((1, MLA_HEADS, tm, QK_PAD), lambda bi, i: (bi, 0, i, 0)),
        pl.BlockSpec((1, MLA_HEADS, tm, QK_PAD), lambda bi, i: (bi, 0, i, 0)),
        pl.BlockSpec((1, MLA_HEADS, MLA_V_DIM, tm), lambda bi, i: (bi, 0, 0, i)),
    ]
    in_specs = [
        pl.BlockSpec((1, tm, D_MODEL), lambda bi, i: (bi, i, 0)),
        pl.BlockSpec((1, N_MOD, D_MODEL), lambda bi, i: (bi + mod_row0, 0, 0)),
        tok_tab, tok_tab, tok_tab, tok_tab, feat_tab, feat_tab,
        _const_spec(w["w_in"].shape), _const_spec(w["w_kt"].shape),
        _const_spec(w["q_norm_w"].shape), _const_spec(w["w_uqn"].shape), _const_spec(w["w_uqr"].shape),
        _const_spec(w["kv_norm_w"].shape), _const_spec(w["w_uk"].shape), _const_spec(w["w_uvt"].shape),
    ]
    return pl.pallas_call(
        _proj_kernel,
        grid=(b, s // tm),
        in_specs=in_specs,
        out_specs=out_specs,
        out_shape=out_shape,
        compiler_params=_params("arbitrary", "arbitrary"),
        name="proj",
    )(x, mod, cos_r, sin_r, cos_m, sin_m, cos_rt, sin_rt,
      w["w_in"], w["w_kt"], w["q_norm_w"], w["w_uqn"], w["w_uqr"], w["kv_norm_w"], w["w_uk"], w["w_uvt"])


def _log_sigmoid(x):
    return jnp.minimum(x, 0.0) - jnp.log(1.0 + jnp.exp(-jnp.abs(x)))


def _ret_decay_tables(dec_row, reverse, d_in_ref, d_q_ref, d_k_ref, d_c_ref, hh):
    c = RET_CHUNK
    lg = _log_sigmoid(dec_row)[:, 0:1]
    row = lax.broadcasted_iota(jnp.int32, (c, c), 0)
    col = lax.broadcasted_iota(jnp.int32, (c, c), 1)
    rel = (col - row) if reverse else (row - col)
    mask = (rel > 0) if reverse else (rel >= 0)
    relf = jnp.maximum(rel, 0).astype(F32)
    d_in_ref[hh] = jnp.where(mask, jnp.exp(lg * relf), 0.0)
    qi = lax.broadcasted_iota(jnp.int32, (c, V7X_LANES), 0).astype(F32)
    kj = lax.broadcasted_iota(jnp.int32, (RET_HEAD_DIM, c), 1).astype(F32)
    q_steps = (c - qi) if reverse else (qi + 1.0)
    k_steps = kj if reverse else (c - 1.0 - kj)
    d_q_ref[hh] = jnp.exp(lg * q_steps)
    d_k_ref[hh] = jnp.exp(lg * k_steps)
    d_c_ref[hh] = jnp.exp(lg * float(c)) + jnp.zeros((V7X_SUBLANES, V7X_LANES), F32)


def _ret_kernel(reverse, *refs):
    if reverse:
        (dec_ref, rq_ref, rkt_ref, rv_ref, fwd_ref, rg_ref, gnw_ref, out_ref,
         state_ref, d_in_ref, d_q_ref, d_k_ref, d_c_ref) = refs
    else:
        (dec_ref, rq_ref, rkt_ref, rv_ref, out_ref,
         state_ref, d_in_ref, d_q_ref, d_k_ref, d_c_ref) = refs
    c = RET_CHUNK
    hd = RET_HEAD_DIM
    n_chunks = rq_ref.shape[1] // c

    @pl.when(pl.program_id(1) == 0)
    def _():
        state_ref[...] = jnp.zeros_like(state_ref)
        for hh in range(RET_HEADS):
            _ret_decay_tables(dec_ref[hh:hh + 1, :], reverse, d_in_ref, d_q_ref, d_k_ref, d_c_ref, hh)

    def chunk_body(ci, carry):
        cidx = (n_chunks - 1 - ci) if reverse else ci
        t0 = pl.multiple_of(cidx * c, c)
        for hh in range(RET_HEADS):
            cols = slice(hh * hd, (hh + 1) * hd)
            q = rq_ref[0, pl.ds(t0, c), cols]
            kt = rkt_ref[0, cidx, cols, :]
            v = rv_ref[0, pl.ds(t0, c), cols]
            state = state_ref[hh]
            s = _dot(q, kt) * d_in_ref[hh]
            inner = _dot(s.astype(BF16), v)
            cross = _dot(q, state.astype(BF16)) * d_q_ref[hh]
            kt_dec = (kt.astype(F32) * d_k_ref[hh]).astype(BF16)
            state_ref[hh] = state * d_c_ref[hh][0:1, :] + _dot(kt_dec, v)
            y = inner + cross
            if reverse:
                y = y + fwd_ref[0, pl.ds(t0, c), cols].astype(F32)
                mu = jnp.mean(y, axis=-1, keepdims=True)
                yc = y - mu
                var = jnp.mean(yc * yc, axis=-1, keepdims=True)
                yn = yc * lax.rsqrt(var + LN_EPS) * gnw_ref[:, cols]
                gate = rg_ref[0, pl.ds(t0, c), cols].astype(F32)
                out_ref[0, pl.ds(t0, c), cols] = (gate * yn).astype(out_ref.dtype)
            else:
                out_ref[0, pl.ds(t0, c), cols] = y.astype(out_ref.dtype)
        return carry

    lax.fori_loop(0, n_chunks, chunk_body, 0, unroll=True)


def _retention(dec_rows, rq, rkt, rv, reverse, fwd=None, rg=None, gn_w=None):
    b, s, _ = rq.shape
    tb = _tile(s, 2048)
    nb = s // tb
    blk = (lambda i: nb - 1 - i) if reverse else (lambda i: i)
    tok = pl.BlockSpec((1, tb, RET_WIDTH), lambda bi, i: (bi, blk(i), 0))
    feat = pl.BlockSpec((1, tb // RET_CHUNK, RET_WIDTH, RET_CHUNK), lambda bi, i: (bi, blk(i), 0, 0))
    dec_spec = pl.BlockSpec((RET_HEADS, V7X_LANES), lambda bi, i: (0, 0))
    in_specs = [dec_spec, tok, feat, tok]
    args = [dec_rows, rq, rkt, rv]
    if reverse:
        in_specs += [tok, tok, pl.BlockSpec((1, RET_WIDTH), lambda bi, i: (0, 0))]
        args += [fwd, rg, gn_w]
    c = RET_CHUNK
    scratch = [
        pltpu.VMEM((RET_HEADS, RET_HEAD_DIM, RET_HEAD_DIM), F32),
        pltpu.VMEM((RET_HEADS, c, c), F32),
        pltpu.VMEM((RET_HEADS, c, V7X_LANES), F32),
        pltpu.VMEM((RET_HEADS, RET_HEAD_DIM, c), F32),
        pltpu.VMEM((RET_HEADS, V7X_SUBLANES, V7X_LANES), F32),
    ]
    return pl.pallas_call(
        functools.partial(_ret_kernel, reverse),
        grid=(b, nb),
        in_specs=in_specs,
        out_specs=tok,
        out_shape=jax.ShapeDtypeStruct((b, s, RET_WIDTH), BF16),
        scratch_shapes=scratch,
        compiler_params=_params("arbitrary", "arbitrary"),
        name="ret_bwd" if reverse else "ret_fwd",
    )(*args)


NEG_BIG = -1e30


ATTN_GROUP = 2 * V7X_LANES
ATTN_LAG = 3


def _attn_kernel(q_ref, k_ref, vprev_ref, vcur_ref, o_ref, s_ref, m_old_ref, m_cur_ref, l_ref, acc_ref):
    t = pl.program_id(3)
    nk = pl.num_programs(3) - 1
    gw = ATTN_GROUP
    ng = q_ref.shape[2] // gw
    lag = min(ATTN_LAG, ng - 1)
    first_carried = ng - lag
    group = lambda g: slice(g * gw, (g + 1) * gw)

    def softmax_pv(g, vt, m_prev, m_new):
        cols = group(g)
        alpha = jnp.exp2(m_prev - m_new)
        p = jnp.exp2(s_ref[:, cols] - m_new)
        l_ref[:, cols] = alpha * l_ref[:, cols] + jnp.sum(p, axis=0, keepdims=True)
        acc_ref[:, cols] = alpha * acc_ref[:, cols] + _dot(vt, p.astype(BF16))

    def scores(g, m_prev):
        cols = group(g)
        s = _nt_dot(k_ref[0, 0], q_ref[0, 0, cols, :])
        s_ref[:, cols] = s
        return jnp.maximum(m_prev, jnp.max(s, axis=0, keepdims=True))

    def carried_pv(g):
        cols = group(g)
        softmax_pv(g, vprev_ref[0, 0], m_old_ref[:, cols], m_cur_ref[:, cols])

    def step(has_prev):
        m_before, m_after = {}, {}
        for j in range(ng):
            cols = group(j)
            m_before[j] = m_cur_ref[:, cols]
            m_after[j] = scores(j, m_before[j])
            if j >= first_carried:
                m_old_ref[:, cols] = m_before[j]
                m_cur_ref[:, cols] = m_after[j]
            d = j - lag
            if d >= 0:
                softmax_pv(d, vcur_ref[0, 0], m_before[d], m_after[d])
                m_cur_ref[:, group(d)] = m_after[d]
            elif has_prev:
                carried_pv(ng + d)

    @pl.when(t == 0)
    def _():
        m_cur_ref[...] = jnp.full_like(m_cur_ref, NEG_BIG)
        l_ref[...] = jnp.zeros_like(l_ref)
        acc_ref[...] = jnp.zeros_like(acc_ref)
        step(False)

    @pl.when(jnp.logical_and(t > 0, t < nk))
    def _():
        step(True)

    @pl.when(t == nk)
    def _():
        for g in range(first_carried, ng):
            carried_pv(g)
        o = acc_ref[...] / l_ref[...]
        o_ref[0] = o.T.astype(o_ref.dtype)


def _attention(q, k, vt):
    b, nh, s, _ = q.shape
    tq = _tile(s, 4096)
    tk = _tile(s, 2048)
    nk = s // tk
    return pl.pallas_call(
        _attn_kernel,
        grid=(b, nh, s // tq, nk + 1),
        in_specs=[
            pl.BlockSpec((1, 1, tq, QK_PAD), lambda bi, hi, qi, t: (bi, hi, qi, 0)),
            pl.BlockSpec((1, 1, tk, QK_PAD), lambda bi, hi, qi, t: (bi, hi, jnp.minimum(t, nk - 1), 0)),
            pl.BlockSpec((1, 1, MLA_V_DIM, tk), lambda bi, hi, qi, t: (bi, hi, 0, jnp.maximum(t - 1, 0))),
            pl.BlockSpec((1, 1, MLA_V_DIM, tk), lambda bi, hi, qi, t: (bi, hi, 0, jnp.minimum(t, nk - 1))),
        ],
        out_specs=pl.BlockSpec((1, tq, MLA_V_DIM), lambda bi, hi, qi, t: (bi, qi, hi)),
        out_shape=jax.ShapeDtypeStruct((b, s, MLA_WIDTH), BF16),
        scratch_shapes=[
            pltpu.VMEM((tk, tq), F32),
            pltpu.VMEM((1, tq), F32),
            pltpu.VMEM((1, tq), F32),
            pltpu.VMEM((1, tq), F32),
            pltpu.VMEM((MLA_V_DIM, tq), F32),
        ],
        compiler_params=_params("arbitrary", "arbitrary", "arbitrary", "arbitrary"),
        name="attn",
    )(q, k, vt, vt)


FF_CHUNK = 1024
MLP_TILE = 1024
MLP_SUB = 256


def _mlp_kernel(alpha, x_ref, ret_ref, att_ref, mod_ref, w_o_ref, ln1_w_ref, ln1_b_ref,
                w_up_ref, w_down_ref, ln2_w_ref, ln2_b_ref, o_ref):
    g1 = mod_ref[0, 2:3, :]
    sh2 = mod_ref[0, 3:4, :]
    sc2 = mod_ref[0, 4:5, :]
    g2 = mod_ref[0, 5:6, :]
    n_sub = max(x_ref.shape[1] // MLP_SUB, 1)
    sub = x_ref.shape[1] // n_sub
    rows = [slice(i * sub, (i + 1) * sub) for i in range(n_sub)]
    mixes = [_dot(ret_ref[0, r, :], w_o_ref[0:RET_WIDTH, :]) + _dot(att_ref[0, r, :], w_o_ref[RET_WIDTH:, :])
             for r in rows]
    x1s, hs = [], []
    for r, mix in zip(rows, mixes):
        x1 = _layer_norm(alpha * x_ref[0, r, :] + g1 * mix) * ln1_w_ref[...] + ln1_b_ref[...]
        x1s.append(x1)
        hs.append((_layer_norm(x1) * (1.0 + sc2) + sh2).astype(BF16))
    for r, x1, h in zip(rows, x1s, hs):
        ff = jnp.zeros_like(x1)
        for c0 in range(0, D_FF, FF_CHUNK):
            u = jnp.maximum(_dot(h, w_up_ref[:, c0:c0 + FF_CHUNK]), 0.0)
            ff = ff + _dot((u * u).astype(BF16), w_down_ref[c0:c0 + FF_CHUNK, :])
        o_ref[0, r, :] = _layer_norm(alpha * x1 + g2 * ff) * ln2_w_ref[...] + ln2_b_ref[...]


def _mlp(alpha, x, ret, att, mod, mod_row0, w):
    b, s, _ = x.shape
    tm = _tile(s, MLP_TILE)
    xs = pl.BlockSpec((1, tm, D_MODEL), lambda bi, i: (bi, i, 0))
    half = pl.BlockSpec((1, tm, RET_WIDTH), lambda bi, i: (bi, i, 0))
    vec = lambda: pl.BlockSpec((1, D_MODEL), lambda bi, i: (0, 0))
    return pl.pallas_call(
        functools.partial(_mlp_kernel, alpha),
        grid=(b, s // tm),
        in_specs=[
            xs, half, half,
            pl.BlockSpec((1, N_MOD, D_MODEL), lambda bi, i: (bi + mod_row0, 0, 0)),
            _const_spec(w["w_o"].shape), vec(), vec(),
            _const_spec(w["w_up"].shape), _const_spec(w["w_down"].shape), vec(), vec(),
        ],
        out_specs=xs,
        out_shape=jax.ShapeDtypeStruct((b, s, D_MODEL), F32),
        compiler_params=_params("arbitrary", "arbitrary"),
        name="mlp",
    )(x, ret, att, mod, w["w_o"], w["ln1_w"], w["ln1_b"], w["w_up"], w["w_down"], w["ln2_w"], w["ln2_b"])


def _prep_layer_weights(w_in, ret_decay_f, ret_decay_b, ret_gn_w, q_norm_w, w_uq, kv_norm_w, w_ukv,
                        w_o, ln1_w, ln1_b, w_up, w_down, ln2_w, ln2_b):
    rw = RET_WIDTH
    w_rq, w_rk, w_rest = w_in[:, :rw], w_in[:, rw:2 * rw], w_in[:, 2 * rw:]
    pad = jnp.zeros((D_MODEL, _W_IN_COLS - (w_in.shape[1] - rw)), w_in.dtype)
    w_in_p = jnp.concatenate([w_rq, w_rest, pad], axis=1).astype(BF16)
    uq = w_uq.reshape(Q_LORA, MLA_HEADS, MLA_QK_DIM)
    ukv = w_ukv.reshape(KV_LORA, MLA_HEADS, MLA_NOPE_DIM + MLA_V_DIM)
    lanes = lambda v: jnp.broadcast_to(v.reshape(RET_HEADS, 1), (RET_HEADS, V7X_LANES)).astype(F32)
    return {
        "w_in": w_in_p,
        "w_kt": w_rk.T.astype(BF16),
        "q_norm_w": q_norm_w.reshape(1, Q_LORA),
        "w_uqn": uq[:, :, :MLA_NOPE_DIM].reshape(Q_LORA, MLA_HEADS * MLA_NOPE_DIM).astype(BF16),
        "w_uqr": uq[:, :, MLA_NOPE_DIM:].reshape(Q_LORA, MLA_HEADS * MLA_ROPE_DIM).astype(BF16),
        "kv_norm_w": kv_norm_w.reshape(1, KV_LORA),
        "w_uk": ukv[:, :, :MLA_NOPE_DIM].reshape(KV_LORA, MLA_HEADS * MLA_NOPE_DIM).astype(BF16),
        "w_uvt": ukv[:, :, MLA_NOPE_DIM:].reshape(KV_LORA, MLA_HEADS * MLA_V_DIM).T.astype(BF16),
        "dec_f": lanes(ret_decay_f),
        "dec_b": lanes(ret_decay_b),
        "gn_w": ret_gn_w.reshape(1, RET_WIDTH),
        "w_o": w_o.astype(BF16),
        "ln1_w": ln1_w.reshape(1, D_MODEL), "ln1_b": ln1_b.reshape(1, D_MODEL),
        "w_up": w_up.astype(BF16), "w_down": w_down.astype(BF16),
        "ln2_w": ln2_w.reshape(1, D_MODEL), "ln2_b": ln2_b.reshape(1, D_MODEL),
    }


def _layer(alpha, x, mod, mod_row0, tables, w):
    rq, rkt, rv, rg, q, k, vt = _proj(x, mod, mod_row0, tables, w)
    fwd = _retention(w["dec_f"], rq, rkt, rv, reverse=False)
    ret = _retention(w["dec_b"], rq, rkt, rv, reverse=True, fwd=fwd, rg=rg, gn_w=w["gn_w"])
    att = _attention(q, k, vt)
    return _mlp(alpha, x, ret, att, mod, mod_row0, w)


def _trunks(xs, cs, w_ada, b_ada, layer_weights):
    depth = w_ada.shape[0]
    alpha = float((2 * depth) ** 0.25)
    n_rows = sum(c.shape[0] for c in cs)
    assert n_rows <= ROWS_PAD
    c_pad = jnp.concatenate(list(cs) + [jnp.zeros((ROWS_PAD - n_rows, D_MODEL), F32)], axis=0)
    tables = _rotary_tables(max(x.shape[1] for x in xs))
    outs = list(xs)
    for l in range(depth):
        mod = _ada(c_pad, w_ada[l], b_ada[l]).reshape(ROWS_PAD, N_MOD, D_MODEL)
        w = _prep_layer_weights(*[t[l] for t in layer_weights])
        row0 = 0
        for i, c in enumerate(cs):
            outs[i] = _layer(alpha, outs[i], mod, row0, tables, w)
            row0 += c.shape[0]
    return tuple(outs)


def kernel(x_prompt, x_sample, c_prompt, c_sample, w_ada, b_ada, w_in, ret_decay_f, ret_decay_b, ret_gn_w,
           q_norm_w, w_uq, kv_norm_w, w_ukv, w_o, ln1_w, ln1_b, w_up, w_down, ln2_w, ln2_b):
    layer_weights = (w_in, ret_decay_f, ret_decay_b, ret_gn_w, q_norm_w, w_uq, kv_norm_w, w_ukv,
                     w_o, ln1_w, ln1_b, w_up, w_down, ln2_w, ln2_b)
    return _trunks((x_prompt, x_sample), (c_prompt, c_sample), w_ada, b_ada, layer_weights)
```

```python
import functools
import math

import jax
import jax.numpy as jnp
from jax import lax
from jax.experimental import pallas as pl
from jax.experimental.pallas import tpu as pltpu

F32 = jnp.float32
BF16 = jnp.bfloat16

D_MODEL = 1024
RET_HEADS = 4
RET_HEAD_DIM = 128
RET_WIDTH = RET_HEADS * RET_HEAD_DIM
MLA_HEADS = 4
MLA_NOPE_DIM = 128
MLA_ROPE_DIM = 64
MLA_V_DIM = 128
MLA_WIDTH = MLA_HEADS * MLA_V_DIM
MLA_QK_DIM = MLA_NOPE_DIM + MLA_ROPE_DIM
Q_LORA = 384
KV_LORA = 256
D_FF = 4 * D_MODEL
ROPE_BASE = 10000.0
LN_EPS = 1e-5
RMS_EPS = 1e-6
N_MOD = 6
MLA_SCALE = float(MLA_QK_DIM ** -0.5)
LOG2E = 1.4426950408889634

V7X_LANES = 128
V7X_SUBLANES = 8
V7X_VMEM_BYTES = 64 * 1024 * 1024
VMEM_LIMIT = V7X_VMEM_BYTES - 8 * 1024 * 1024

QK_PAD = 2 * V7X_LANES
_OFF_RQ, _OFF_RV, _OFF_RG = 0, RET_WIDTH, 2 * RET_WIDTH
_OFF_CQ = 3 * RET_WIDTH
_OFF_CKV = _OFF_CQ + Q_LORA
_OFF_KR = _OFF_CKV + KV_LORA
_W_IN_COLS = _OFF_KR + V7X_LANES
ROWS_PAD = V7X_SUBLANES
RET_CHUNK = 128


def _tile(n, pref):
    t = min(n, pref)
    assert n % t == 0, (n, t)
    return t


def _const_spec(shape):
    nd = len(shape)
    return pl.BlockSpec(shape, lambda *_: (0,) * nd, pipeline_mode=pl.Buffered(1))


def _params(*sem):
    return pltpu.CompilerParams(dimension_semantics=sem, vmem_limit_bytes=VMEM_LIMIT)


def _nt_dot(a, b):
    return lax.dot_general(a, b, (((1,), (1,)), ((), ())), preferred_element_type=F32)


def _dot(a, b):
    return jnp.dot(a, b, preferred_element_type=F32)


def _ada_kernel(c_ref, w_ref, b_ref, o_ref):
    c = c_ref[...]
    s = c / (1.0 + jnp.exp(-c))
    o_ref[...] = _dot(s.astype(BF16), w_ref[...].astype(BF16)) + b_ref[...]


def _ada(c_pad, w_ada, b_ada):
    n = w_ada.shape[1]
    tn = _tile(n, D_MODEL)
    return pl.pallas_call(
        _ada_kernel,
        grid=(n // tn,),
        in_specs=[
            pl.BlockSpec((ROWS_PAD, D_MODEL), lambda j: (0, 0)),
            pl.BlockSpec((D_MODEL, tn), lambda j: (0, j)),
            pl.BlockSpec((1, tn), lambda j: (0, j)),
        ],
        out_specs=pl.BlockSpec((ROWS_PAD, tn), lambda j: (0, j)),
        out_shape=jax.ShapeDtypeStruct((ROWS_PAD, n), F32),
        compiler_params=_params("arbitrary"),
        name="ada",
    )(c_pad, w_ada, b_ada.reshape(1, n))


def _tables_kernel(inv_ref, sgn_r_ref, sgn_m_ref,
                   cos_r_ref, sin_r_ref, cos_m_ref, sin_m_ref, cos_rt_ref, sin_rt_ref):
    tr = cos_r_ref.shape[0]
    half = V7X_LANES // 2
    pos = (lax.broadcasted_iota(jnp.int32, (tr, V7X_LANES), 0) + pl.program_id(0) * tr).astype(F32)
    ang = pos * inv_ref[...]
    c = jnp.cos(ang)
    s = jnp.sin(ang)
    c_sw = pltpu.roll(c, half, 1)
    s_sw = pltpu.roll(s, half, 1)
    low = lax.broadcasted_iota(jnp.int32, (tr, V7X_LANES), 1) < half
    cos_r = jnp.where(low, c, c_sw)
    sin_r = jnp.where(low, s, s_sw) * sgn_r_ref[...]
    cos_r_ref[...] = cos_r
    sin_r_ref[...] = sin_r
    cos_m_ref[...] = jnp.where(low, c_sw, c)
    sin_m_ref[...] = jnp.where(low, s_sw, s) * sgn_m_ref[...]
    cos_rt_ref[...] = cos_r.T
    sin_rt_ref[...] = sin_r.T


def _rotary_tables(seq):
    half_r = RET_HEAD_DIM // 2
    half_m = MLA_ROPE_DIM // 2
    assert half_r + 2 * half_m == V7X_LANES and RET_HEAD_DIM == V7X_LANES
    inv_r = ROPE_BASE ** (-jnp.arange(0, RET_HEAD_DIM, 2, dtype=F32) / RET_HEAD_DIM)
    inv_m = ROPE_BASE ** (-jnp.arange(0, MLA_ROPE_DIM, 2, dtype=F32) / MLA_ROPE_DIM)
    inv = jnp.concatenate([inv_r, inv_m, inv_m])
    sgn_r = jnp.concatenate([-jnp.ones(half_r, F32), jnp.ones(half_r, F32)])
    sgn_m = jnp.tile(jnp.concatenate([-jnp.ones(half_m, F32), jnp.ones(half_m, F32)]),
                     V7X_LANES // MLA_ROPE_DIM)
    tr = _tile(seq, 1024)
    row = lambda v: v.reshape(1, V7X_LANES)
    tok = jax.ShapeDtypeStruct((seq, V7X_LANES), F32)
    feat = jax.ShapeDtypeStruct((RET_HEAD_DIM, seq), F32)
    vec = pl.BlockSpec((1, V7X_LANES), lambda i: (0, 0))
    tok_spec = pl.BlockSpec((tr, V7X_LANES), lambda i: (i, 0))
    feat_spec = pl.BlockSpec((RET_HEAD_DIM, tr), lambda i: (0, i))
    return pl.pallas_call(
        _tables_kernel,
        grid=(seq // tr,),
        in_specs=[vec, vec, vec],
        out_specs=[tok_spec, tok_spec, tok_spec, tok_spec, feat_spec, feat_spec],
        out_shape=[tok, tok, tok, tok, feat, feat],
        compiler_params=_params("arbitrary"),
        name="tables",
    )(row(inv), row(sgn_r), row(sgn_m))


def _layer_norm(x):
    mu = jnp.mean(x, axis=-1, keepdims=True)
    xc = x - mu
    var = jnp.mean(xc * xc, axis=-1, keepdims=True)
    return xc * lax.rsqrt(var + LN_EPS)


def _rms_norm(x, w):
    return x * lax.rsqrt(jnp.mean(x * x, axis=-1, keepdims=True) + RMS_EPS) * w


def _rope64(x, cos_m, sin_m, first_half):
    swapped = jnp.where(first_half,
                        pltpu.roll(x, V7X_LANES - MLA_ROPE_DIM // 2, 1),
                        pltpu.roll(x, MLA_ROPE_DIM // 2, 1))
    return x * cos_m + swapped * sin_m


def _proj_kernel(x_ref, mod_ref, cos_r_ref, sin_r_ref, cos_m_ref, sin_m_ref, cos_rt_ref, sin_rt_ref,
                 w_in_ref, w_kt_ref, qn_w_ref, w_uqn_ref, w_uqr_ref, kvn_w_ref, w_uk_ref, w_uvt_ref,
                 rq_ref, rkt_ref, rv_ref, rg_ref, q_ref, k_ref, vt_ref):
    sh1 = mod_ref[0, 0:1, :]
    sc1 = mod_ref[0, 1:2, :]
    hd = RET_HEAD_DIM
    k_scale = RET_HEAD_DIM ** -0.5
    q_scale = MLA_SCALE * LOG2E
    n_sub = max(x_ref.shape[1] // PROJ_SUB, 1)
    sub = x_ref.shape[1] // n_sub
    rows = [slice(i * sub, (i + 1) * sub) for i in range(n_sub)]
    lane = lax.broadcasted_iota(jnp.int32, (sub, V7X_LANES), 1)
    first_half = (lane % MLA_ROPE_DIM) < (MLA_ROPE_DIM // 2)
    low_lanes = lane < MLA_ROPE_DIM

    hs = [(_layer_norm(x_ref[0, r, :]) * (1.0 + sc1) + sh1).astype(BF16) for r in rows]

    def retention_part(r, h):
        proj = lambda lo, n: _dot(h, w_in_ref[:, lo:lo + n])
        cos_r = cos_r_ref[r, :]
        sin_r = sin_r_ref[r, :]
        rq = proj(_OFF_RQ, RET_WIDTH)
        for hh in range(RET_HEADS):
            t = rq[:, hh * hd:(hh + 1) * hd]
            rq_ref[0, r, hh * hd:(hh + 1) * hd] = (t * cos_r + pltpu.roll(t, hd // 2, 1) * sin_r).astype(BF16)
        cos_rt = cos_rt_ref[:, r]
        sin_rt = sin_rt_ref[:, r]
        rkt = _nt_dot(w_kt_ref[...], h)
        chunk0 = r.start // RET_CHUNK
        for hh in range(RET_HEADS):
            t = rkt[hh * hd:(hh + 1) * hd, :]
            rot = ((t * cos_rt + pltpu.roll(t, hd // 2, 0) * sin_rt) * k_scale).astype(BF16)
            for j in range(sub // RET_CHUNK):
                rkt_ref[0, chunk0 + j, hh * hd:(hh + 1) * hd, :] = rot[:, j * RET_CHUNK:(j + 1) * RET_CHUNK]
        rv_ref[0, r, :] = proj(_OFF_RV, RET_WIDTH).astype(BF16)
        g = proj(_OFF_RG, RET_WIDTH)
        rg_ref[0, r, :] = (g / (1.0 + jnp.exp(-g))).astype(BF16)

    def mla_latents(r, h):
        proj = lambda lo, n: _dot(h, w_in_ref[:, lo:lo + n])
        cq = _rms_norm(proj(_OFF_CQ, Q_LORA), qn_w_ref[...]).astype(BF16)
        ckv = _rms_norm(proj(_OFF_CKV, KV_LORA), kvn_w_ref[...]).astype(BF16)
        k_pe = _rope64(proj(_OFF_KR, V7X_LANES), cos_m_ref[r, :], sin_m_ref[r, :], first_half)
        return cq, ckv, k_pe

    def mla_heads(r, cq, ckv, k_pe):
        cos_m = cos_m_ref[r, :]
        sin_m = sin_m_ref[r, :]
        q_nope = _dot(cq, w_uqn_ref[...]) * q_scale
        q_rope = _dot(cq, w_uqr_ref[...]) * q_scale
        k_nope = _dot(ckv, w_uk_ref[...])
        v_t = _nt_dot(w_uvt_ref[...], ckv)
        for pair in range(MLA_HEADS // 2):
            rp = _rope64(q_rope[:, pair * V7X_LANES:(pair + 1) * V7X_LANES], cos_m, sin_m, first_half)
            for side in range(2):
                hh = 2 * pair + side
                keep = low_lanes if side == 0 else jnp.logical_not(low_lanes)
                q_ref[0, hh, r, 0:MLA_NOPE_DIM] = q_nope[:, hh * MLA_NOPE_DIM:(hh + 1) * MLA_NOPE_DIM].astype(BF16)
                q_ref[0, hh, r, MLA_NOPE_DIM:QK_PAD] = jnp.where(keep, rp, 0.0).astype(BF16)
        k_pe_hi = pltpu.roll(k_pe, MLA_ROPE_DIM, 1)
        for hh in range(MLA_HEADS):
            k_ref[0, hh, r, 0:MLA_NOPE_DIM] = k_nope[:, hh * MLA_NOPE_DIM:(hh + 1) * MLA_NOPE_DIM].astype(BF16)
            k_ref[0, hh, r, MLA_NOPE_DIM:QK_PAD] = (k_pe if hh % 2 == 0 else k_pe_hi).astype(BF16)
            vt_ref[0, hh, :, r] = v_t[hh * MLA_V_DIM:(hh + 1) * MLA_V_DIM, :].astype(BF16)

    for r, h in zip(rows, hs):
        retention_part(r, h)
    latents = [mla_latents(r, h) for r, h in zip(rows, hs)]
    for r, lat in zip(rows, latents):
        mla_heads(r, *lat)


PROJ_TILE = 1024
PROJ_SUB = 256


def _proj(x, mod, mod_row0, tables, w):
    b, s, _ = x.shape
    tm = _tile(s, PROJ_TILE)
    cos_r, sin_r, cos_m, sin_m, cos_rt, sin_rt = tables
    tok_tab = pl.BlockSpec((tm, V7X_LANES), lambda bi, i: (i, 0))
    feat_tab = pl.BlockSpec((RET_HEAD_DIM, tm), lambda bi, i: (0, i))
    tok_out = lambda: pl.BlockSpec((1, tm, RET_WIDTH), lambda bi, i: (bi, i, 0))
    out_shape = [
        jax.ShapeDtypeStruct((b, s, RET_WIDTH), BF16),
        jax.ShapeDtypeStruct((b, s // RET_CHUNK, RET_WIDTH, RET_CHUNK), BF16),
        jax.ShapeDtypeStruct((b, s, RET_WIDTH), BF16),
        jax.ShapeDtypeStruct((b, s, RET_WIDTH), BF16),
        jax.ShapeDtypeStruct((b, MLA_HEADS, s, QK_PAD), BF16),
        jax.ShapeDtypeStruct((b, MLA_HEADS, s, QK_PAD), BF16),
        jax.ShapeDtypeStruct((b, MLA_HEADS, MLA_V_DIM, s), BF16),
    ]
    out_specs = [
        tok_out(),
        pl.BlockSpec((1, tm // RET_CHUNK, RET_WIDTH, RET_CHUNK), lambda bi, i: (bi, i, 0, 0)),
        tok_out(),
        tok_out(),
        pl.BlockSpec((1, MLA_HEADS, tm, QK_PAD), lambda bi, i: (bi, 0, i, 0)),
        pl.BlockSpec((1, MLA_HEADS, tm, QK_PAD), lambda bi, i: (bi, 0, i, 0)),
        pl.BlockSpec((1, MLA_HEADS, MLA_V_DIM, tm), lambda bi, i: (bi, 0, 0, i)),
    ]
    in_specs = [
        pl.BlockSpec((1, tm, D_MODEL), lambda bi, i: (bi, i, 0)),
        pl.BlockSpec((1, N_MOD, D_MODEL), lambda bi, i: (bi + mod_row0, 0, 0)),
        tok_tab, tok_tab, tok_tab, tok_tab, feat_tab, feat_tab,
        _const_spec(w["w_in"].shape), _const_spec(w["w_kt"].shape),
        _const_spec(w["q_norm_w"].shape), _const_spec(w["w_uqn"].shape), _const_spec(w["w_uqr"].shape),
        _const_spec(w["kv_norm_w"].shape), _const_spec(w["w_uk"].shape), _const_spec(w["w_uvt"].shape),
    ]
    return pl.pallas_call(
        _proj_kernel,
        grid=(b, s // tm),
        in_specs=in_specs,
        out_specs=out_specs,
        out_shape=out_shape,
        compiler_params=_params("arbitrary", "arbitrary"),
        name="proj",
    )(x, mod, cos_r, sin_r, cos_m, sin_m, cos_rt, sin_rt,
      w["w_in"], w["w_kt"], w["q_norm_w"], w["w_uqn"], w["w_uqr"], w["kv_norm_w"], w["w_uk"], w["w_uvt"])


def _log_sigmoid(x):
    return jnp.minimum(x, 0.0) - jnp.log(1.0 + jnp.exp(-jnp.abs(x)))


def _ret_decay_tables(dec_row, reverse, d_in_ref, d_q_ref, d_k_ref, d_c_ref, hh):
    c = RET_CHUNK
    lg = _log_sigmoid(dec_row)[:, 0:1]
    row = lax.broadcasted_iota(jnp.int32, (c, c), 0)
    col = lax.broadcasted_iota(jnp.int32, (c, c), 1)
    rel = (col - row) if reverse else (row - col)
    mask = (rel > 0) if reverse else (rel >= 0)
    relf = jnp.maximum(rel, 0).astype(F32)
    d_in_ref[hh] = jnp.where(mask, jnp.exp(lg * relf), 0.0)
    qi = lax.broadcasted_iota(jnp.int32, (c, V7X_LANES), 0).astype(F32)
    kj = lax.broadcasted_iota(jnp.int32, (RET_HEAD_DIM, c), 1).astype(F32)
    q_steps = (c - qi) if reverse else (qi + 1.0)
    k_steps = kj if reverse else (c - 1.0 - kj)
    d_q_ref[hh] = jnp.exp(lg * q_steps)
    d_k_ref[hh] = jnp.exp(lg * k_steps)
    d_c_ref[hh] = jnp.exp(lg * float(c)) + jnp.zeros((V7X_SUBLANES, V7X_LANES), F32)


def _ret_kernel(reverse, *refs):
    if reverse:
        (dec_ref, rq_ref, rkt_ref, rv_ref, fwd_ref, rg_ref, gnw_ref, out_ref,
         state_ref, d_in_ref, d_q_ref, d_k_ref, d_c_ref) = refs
    else:
        (dec_ref, rq_ref, rkt_ref, rv_ref, out_ref,
         state_ref, d_in_ref, d_q_ref, d_k_ref, d_c_ref) = refs
    c = RET_CHUNK
    hd = RET_HEAD_DIM
    n_chunks = rq_ref.shape[1] // c

    @pl.when(pl.program_id(1) == 0)
    def _():
        state_ref[...] = jnp.zeros_like(state_ref)
        for hh in range(RET_HEADS):
            _ret_decay_tables(dec_ref[hh:hh + 1, :], reverse, d_in_ref, d_q_ref, d_k_ref, d_c_ref, hh)

    def chunk_body(ci, carry):
        cidx = (n_chunks - 1 - ci) if reverse else ci
        t0 = pl.multiple_of(cidx * c, c)
        for hh in range(RET_HEADS):
            cols = slice(hh * hd, (hh + 1) * hd)
            q = rq_ref[0, pl.ds(t0, c), cols]
            kt = rkt_ref[0, cidx, cols, :]
            v = rv_ref[0, pl.ds(t0, c), cols]
            state = state_ref[hh]
            s = _dot(q, kt) * d_in_ref[hh]
            inner = _dot(s.astype(BF16), v)
            cross = _dot(q, state.astype(BF16)) * d_q_ref[hh]
            kt_dec = (kt.astype(F32) * d_k_ref[hh]).astype(BF16)
            state_ref[hh] = state * d_c_ref[hh][0:1, :] + _dot(kt_dec, v)
            y = inner + cross
            if reverse:
                y = y + fwd_ref[0, pl.ds(t0, c), cols].astype(F32)
                mu = jnp.mean(y, axis=-1, keepdims=True)
                yc = y - mu
                var = jnp.mean(yc * yc, axis=-1, keepdims=True)
                yn = yc * lax.rsqrt(var + LN_EPS) * gnw_ref[:, cols]
                gate = rg_ref[0, pl.ds(t0, c), cols].astype(F32)
                out_ref[0, pl.ds(t0, c), cols] = (gate * yn).astype(out_ref.dtype)
            else:
                out_ref[0, pl.ds(t0, c), cols] = y.astype(out_ref.dtype)
        return carry

    lax.fori_loop(0, n_chunks, chunk_body, 0, unroll=True)


def _retention(dec_rows, rq, rkt, rv, reverse, fwd=None, rg=None, gn_w=None):
    b, s, _ = rq.shape
    tb = _tile(s, 2048)
    nb = s // tb
    blk = (lambda i: nb - 1 - i) if reverse else (lambda i: i)
    tok = pl.BlockSpec((1, tb, RET_WIDTH), lambda bi, i: (bi, blk(i), 0))
    feat = pl.BlockSpec((1, tb // RET_CHUNK, RET_WIDTH, RET_CHUNK), lambda bi, i: (bi, blk(i), 0, 0))
    dec_spec = pl.BlockSpec((RET_HEADS, V7X_LANES), lambda bi, i: (0, 0))
    in_specs = [dec_spec, tok, feat, tok]
    args = [dec_rows, rq, rkt, rv]
    if reverse:
        in_specs += [tok, tok, pl.BlockSpec((1, RET_WIDTH), lambda bi, i: (0, 0))]
        args += [fwd, rg, gn_w]
    c = RET_CHUNK
    scratch = [
        pltpu.VMEM((RET_HEADS, RET_HEAD_DIM, RET_HEAD_DIM), F32),
        pltpu.VMEM((RET_HEADS, c, c), F32),
        pltpu.VMEM((RET_HEADS, c, V7X_LANES), F32),
        pltpu.VMEM((RET_HEADS, RET_HEAD_DIM, c), F32),
        pltpu.VMEM((RET_HEADS, V7X_SUBLANES, V7X_LANES), F32),
    ]
    return pl.pallas_call(
        functools.partial(_ret_kernel, reverse),
        grid=(b, nb),
        in_specs=in_specs,
        out_specs=tok,
        out_shape=jax.ShapeDtypeStruct((b, s, RET_WIDTH), BF16),
        scratch_shapes=scratch,
        compiler_params=_params("arbitrary", "arbitrary"),
        name="ret_bwd" if reverse else "ret_fwd",
    )(*args)


NEG_BIG = -1e30


ATTN_GROUP = 2 * V7X_LANES
ATTN_LAG = 3
ATTN_KV_ITEM = 2048
ATTN_KV_STEP = 4096
ATTN_Q_TILE = 4096


def _attn_kernel(q_ref, k_ref, vprev_ref, vcur_ref, o_ref, s_ref, m_old_ref, m_cur_ref, l_ref, acc_ref):
    t = pl.program_id(3)
    nk = pl.num_programs(3) - 1
    gw = ATTN_GROUP
    ng = q_ref.shape[2] // gw
    rows = s_ref.shape[0]
    n_sub = k_ref.shape[2] // rows
    n_items = n_sub * ng
    lag = min(ATTN_LAG, ng - 1)
    first_carried = ng - lag
    group = lambda g: slice(g * gw, (g + 1) * gw)
    kv = lambda a: slice(a * rows, (a + 1) * rows)

    def softmax_pv(g, vt, m_prev, m_new):
        cols = group(g)
        alpha = jnp.exp2(m_prev - m_new)
        p = jnp.exp2(s_ref[:, cols] - m_new)
        l_ref[:, cols] = alpha * l_ref[:, cols] + jnp.sum(p, axis=0, keepdims=True)
        acc_ref[:, cols] = alpha * acc_ref[:, cols] + _dot(vt, p.astype(BF16))

    def scores(a, g, m_prev):
        cols = group(g)
        s = _nt_dot(k_ref[0, 0, kv(a), :], q_ref[0, 0, cols, :])
        s_ref[:, cols] = s
        return jnp.maximum(m_prev, jnp.max(s, axis=0, keepdims=True))

    def carried_pv(g):
        cols = group(g)
        softmax_pv(g, vprev_ref[0, 0], m_old_ref[:, cols], m_cur_ref[:, cols])

    def step(has_prev):
        m_before, m_after = {}, {}
        for p in range(n_items):
            a, g = divmod(p, ng)
            cols = group(g)
            m_before[p] = m_cur_ref[:, cols] if a == 0 else m_after[p - ng]
            m_after[p] = scores(a, g, m_before[p])
            if p >= n_items - lag:
                m_old_ref[:, cols] = m_before[p]
                m_cur_ref[:, cols] = m_after[p]
            d = p - lag
            if d >= 0:
                da, dg = divmod(d, ng)
                softmax_pv(dg, vcur_ref[0, 0, :, kv(da)], m_before[d], m_after[d])
                if da == n_sub - 1:
                    m_cur_ref[:, group(dg)] = m_after[d]
            elif has_prev:
                carried_pv(ng + d)

    @pl.when(t == 0)
    def _():
        m_cur_ref[...] = jnp.full_like(m_cur_ref, NEG_BIG)
        l_ref[...] = jnp.zeros_like(l_ref)
        acc_ref[...] = jnp.zeros_like(acc_ref)
        step(False)

    @pl.when(jnp.logical_and(t > 0, t < nk))
    def _():
        step(True)

    @pl.when(t == nk)
    def _():
        for g in range(first_carried, ng):
            carried_pv(g)
        o = acc_ref[...] / l_ref[...]
        o_ref[0] = o.T.astype(o_ref.dtype)


def _attention(q, k, vt):
    b, nh, s, _ = q.shape
    tq = _tile(s, ATTN_Q_TILE)
    tk = _tile(s, ATTN_KV_STEP)
    rows = _tile(tk, ATTN_KV_ITEM)
    n_sub = tk // rows
    nk = s // tk
    return pl.pallas_call(
        _attn_kernel,
        grid=(b, nh, s // tq, nk + 1),
        in_specs=[
            pl.BlockSpec((1, 1, tq, QK_PAD), lambda bi, hi, qi, t: (bi, hi, qi, 0)),
            pl.BlockSpec((1, 1, tk, QK_PAD), lambda bi, hi, qi, t: (bi, hi, jnp.minimum(t, nk - 1), 0)),
            pl.BlockSpec((1, 1, MLA_V_DIM, rows),
                         lambda bi, hi, qi, t: (bi, hi, 0, jnp.maximum(t * n_sub - 1, 0))),
            pl.BlockSpec((1, 1, MLA_V_DIM, tk), lambda bi, hi, qi, t: (bi, hi, 0, jnp.minimum(t, nk - 1))),
        ],
        out_specs=pl.BlockSpec((1, tq, MLA_V_DIM), lambda bi, hi, qi, t: (bi, qi, hi)),
        out_shape=jax.ShapeDtypeStruct((b, s, MLA_WIDTH), BF16),
        scratch_shapes=[
            pltpu.VMEM((rows, tq), F32),
            pltpu.VMEM((1, tq), F32),
            pltpu.VMEM((1, tq), F32),
            pltpu.VMEM((1, tq), F32),
            pltpu.VMEM((MLA_V_DIM, tq), F32),
        ],
        compiler_params=_params("arbitrary", "arbitrary", "arbitrary", "arbitrary"),
        name="attn",
    )(q, k, vt, vt)


FF_CHUNK = 1024
MLP_TILE = 1024
MLP_SUB = 256


def _mlp_kernel(alpha, x_ref, ret_ref, att_ref, mod_ref, w_o_ref, ln1_w_ref, ln1_b_ref,
                w_up_ref, w_down_ref, ln2_w_ref, ln2_b_ref, o_ref):
    g1 = mod_ref[0, 2:3, :]
    sh2 = mod_ref[0, 3:4, :]
    sc2 = mod_ref[0, 4:5, :]
    g2 = mod_ref[0, 5:6, :]
    n_sub = max(x_ref.shape[1] // MLP_SUB, 1)
    sub = x_ref.shape[1] // n_sub
    rows = [slice(i * sub, (i + 1) * sub) for i in range(n_sub)]
    mixes = [_dot(ret_ref[0, r, :], w_o_ref[0:RET_WIDTH, :]) + _dot(att_ref[0, r, :], w_o_ref[RET_WIDTH:, :])
             for r in rows]
    x1s, hs = [], []
    for r, mix in zip(rows, mixes):
        x1 = _layer_norm(alpha * x_ref[0, r, :] + g1 * mix) * ln1_w_ref[...] + ln1_b_ref[...]
        x1s.append(x1)
        hs.append((_layer_norm(x1) * (1.0 + sc2) + sh2).astype(BF16))
    for r, x1, h in zip(rows, x1s, hs):
        ff = jnp.zeros_like(x1)
        for c0 in range(0, D_FF, FF_CHUNK):
            u = jnp.maximum(_dot(h, w_up_ref[:, c0:c0 + FF_CHUNK]), 0.0)
            ff = ff + _dot((u * u).astype(BF16), w_down_ref[c0:c0 + FF_CHUNK, :])
        o_ref[0, r, :] = _layer_norm(alpha * x1 + g2 * ff) * ln2_w_ref[...] + ln2_b_ref[...]


def _mlp(alpha, x, ret, att, mod, mod_row0, w):
    b, s, _ = x.shape
    tm = _tile(s, MLP_TILE)
    xs = pl.BlockSpec((1, tm, D_MODEL), lambda bi, i: (bi, i, 0))
    half = pl.BlockSpec((1, tm, RET_WIDTH), lambda bi, i: (bi, i, 0))
    vec = lambda: pl.BlockSpec((1, D_MODEL), lambda bi, i: (0, 0))
    return pl.pallas_call(
        functools.partial(_mlp_kernel, alpha),
        grid=(b, s // tm),
        in_specs=[
            xs, half, half,
            pl.BlockSpec((1, N_MOD, D_MODEL), lambda bi, i: (bi + mod_row0, 0, 0)),
            _const_spec(w["w_o"].shape), vec(), vec(),
            _const_spec(w["w_up"].shape), _const_spec(w["w_down"].shape), vec(), vec(),
        ],
        out_specs=xs,
        out_shape=jax.ShapeDtypeStruct((b, s, D_MODEL), F32),
        compiler_params=_params("arbitrary", "arbitrary"),
        name="mlp",
    )(x, ret, att, mod, w["w_o"], w["ln1_w"], w["ln1_b"], w["w_up"], w["w_down"], w["ln2_w"], w["ln2_b"])


def _prep_layer_weights(w_in, ret_decay_f, ret_decay_b, ret_gn_w, q_norm_w, w_uq, kv_norm_w, w_ukv,
                        w_o, ln1_w, ln1_b, w_up, w_down, ln2_w, ln2_b):
    rw = RET_WIDTH
    w_rq, w_rk, w_rest = w_in[:, :rw], w_in[:, rw:2 * rw], w_in[:, 2 * rw:]
    pad = jnp.zeros((D_MODEL, _W_IN_COLS - (w_in.shape[1] - rw)), w_in.dtype)
    w_in_p = jnp.concatenate([w_rq, w_rest, pad], axis=1).astype(BF16)
    uq = w_uq.reshape(Q_LORA, MLA_HEADS, MLA_QK_DIM)
    ukv = w_ukv.reshape(KV_LORA, MLA_HEADS, MLA_NOPE_DIM + MLA_V_DIM)
    lanes = lambda v: jnp.broadcast_to(v.reshape(RET_HEADS, 1), (RET_HEADS, V7X_LANES)).astype(F32)
    return {
        "w_in": w_in_p,
        "w_kt": w_rk.T.astype(BF16),
        "q_norm_w": q_norm_w.reshape(1, Q_LORA),
        "w_uqn": uq[:, :, :MLA_NOPE_DIM].reshape(Q_LORA, MLA_HEADS * MLA_NOPE_DIM).astype(BF16),
        "w_uqr": uq[:, :, MLA_NOPE_DIM:].reshape(Q_LORA, MLA_HEADS * MLA_ROPE_DIM).astype(BF16),
        "kv_norm_w": kv_norm_w.reshape(1, KV_LORA),
        "w_uk": ukv[:, :, :MLA_NOPE_DIM].reshape(KV_LORA, MLA_HEADS * MLA_NOPE_DIM).astype(BF16),
        "w_uvt": ukv[:, :, MLA_NOPE_DIM:].reshape(KV_LORA, MLA_HEADS * MLA_V_DIM).T.astype(BF16),
        "dec_f": lanes(ret_decay_f),
        "dec_b": lanes(ret_decay_b),
        "gn_w": ret_gn_w.reshape(1, RET_WIDTH),
        "w_o": w_o.astype(BF16),
        "ln1_w": ln1_w.reshape(1, D_MODEL), "ln1_b": ln1_b.reshape(1, D_MODEL),
        "w_up": w_up.astype(BF16), "w_down": w_down.astype(BF16),
        "ln2_w": ln2_w.reshape(1, D_MODEL), "ln2_b": ln2_b.reshape(1, D_MODEL),
    }


def _layer(alpha, x, mod, mod_row0, tables, w):
    rq, rkt, rv, rg, q, k, vt = _proj(x, mod, mod_row0, tables, w)
    fwd = _retention(w["dec_f"], rq, rkt, rv, reverse=False)
    ret = _retention(w["dec_b"], rq, rkt, rv, reverse=True, fwd=fwd, rg=rg, gn_w=w["gn_w"])
    att = _attention(q, k, vt)
    return _mlp(alpha, x, ret, att, mod, mod_row0, w)


def _trunks(xs, cs, w_ada, b_ada, layer_weights):
    depth = w_ada.shape[0]
    alpha = float((2 * depth) ** 0.25)
    n_rows = sum(c.shape[0] for c in cs)
    assert n_rows <= ROWS_PAD
    c_pad = jnp.concatenate(list(cs) + [jnp.zeros((ROWS_PAD - n_rows, D_MODEL), F32)], axis=0)
    tables = _rotary_tables(max(x.shape[1] for x in xs))
    outs = list(xs)
    for l in range(depth):
        mod = _ada(c_pad, w_ada[l], b_ada[l]).reshape(ROWS_PAD, N_MOD, D_MODEL)
        w = _prep_layer_weights(*[t[l] for t in layer_weights])
        row0 = 0
        for i, c in enumerate(cs):
            outs[i] = _layer(alpha, outs[i], mod, row0, tables, w)
            row0 += c.shape[0]
    return tuple(outs)


def kernel(x_prompt, x_sample, c_prompt, c_sample, w_ada, b_ada, w_in, ret_decay_f, ret_decay_b, ret_gn_w,
           q_norm_w, w_uq, kv_norm_w, w_ukv, w_o, ln1_w, ln1_b, w_up, w_down, ln2_w, ln2_b):
    layer_weights = (w_in, ret_decay_f, ret_decay_b, ret_gn_w, q_norm_w, w_uq, kv_norm_w, w_ukv,
                     w_o, ln1_w, ln1_b, w_up, w_down, ln2_w, ln2_b)
    return _trunks((x_prompt, x_sample), (c_prompt, c_sample), w_ada, b_ada, layer_weights)
```

```python
import functools
import math

import jax
import jax.numpy as jnp
from jax import lax
from jax.experimental import pallas as pl
from jax.experimental.pallas import tpu as pltpu

F32 = jnp.float32
BF16 = jnp.bfloat16

D_MODEL = 1024
RET_HEADS = 4
RET_HEAD_DIM = 128
RET_WIDTH = RET_HEADS * RET_HEAD_DIM
MLA_HEADS = 4
MLA_NOPE_DIM = 128
MLA_ROPE_DIM = 64
MLA_V_DIM = 128
MLA_WIDTH = MLA_HEADS * MLA_V_DIM
MLA_QK_DIM = MLA_NOPE_DIM + MLA_ROPE_DIM
Q_LORA = 384
KV_LORA = 256
D_FF = 4 * D_MODEL
ROPE_BASE = 10000.0
LN_EPS = 1e-5
RMS_EPS = 1e-6
N_MOD = 6
MLA_SCALE = float(MLA_QK_DIM ** -0.5)
LOG2E = 1.4426950408889634

V7X_LANES = 128
V7X_SUBLANES = 8
V7X_VMEM_BYTES = 64 * 1024 * 1024
VMEM_LIMIT = V7X_VMEM_BYTES - 8 * 1024 * 1024

QK_PAD = 2 * V7X_LANES
VT_ROWS = MLA_V_DIM + V7X_SUBLANES
_OFF_RQ, _OFF_RV, _OFF_RG = 0, RET_WIDTH, 2 * RET_WIDTH
_OFF_CQ = 3 * RET_WIDTH
_OFF_CKV = _OFF_CQ + Q_LORA
_OFF_KR = _OFF_CKV + KV_LORA
_W_IN_COLS = _OFF_KR + V7X_LANES
ROWS_PAD = V7X_SUBLANES
RET_CHUNK = 128


def _tile(n, pref):
    t = min(n, pref)
    assert n % t == 0, (n, t)
    return t


def _const_spec(shape):
    nd = len(shape)
    return pl.BlockSpec(shape, lambda *_: (0,) * nd, pipeline_mode=pl.Buffered(1))


def _params(*sem):
    return pltpu.CompilerParams(dimension_semantics=sem, vmem_limit_bytes=VMEM_LIMIT)


def _nt_dot(a, b):
    return lax.dot_general(a, b, (((1,), (1,)), ((), ())), preferred_element_type=F32)


def _dot(a, b):
    return jnp.dot(a, b, preferred_element_type=F32)


def _ada_kernel(c_ref, w_ref, b_ref, o_ref):
    c = c_ref[...]
    s = c / (1.0 + jnp.exp(-c))
    o_ref[...] = _dot(s.astype(BF16), w_ref[...].astype(BF16)) + b_ref[...]


def _ada(c_pad, w_ada, b_ada):
    n = w_ada.shape[1]
    tn = _tile(n, D_MODEL)
    return pl.pallas_call(
        _ada_kernel,
        grid=(n // tn,),
        in_specs=[
            pl.BlockSpec((ROWS_PAD, D_MODEL), lambda j: (0, 0)),
            pl.BlockSpec((D_MODEL, tn), lambda j: (0, j)),
            pl.BlockSpec((1, tn), lambda j: (0, j)),
        ],
        out_specs=pl.BlockSpec((ROWS_PAD, tn), lambda j: (0, j)),
        out_shape=jax.ShapeDtypeStruct((ROWS_PAD, n), F32),
        compiler_params=_params("arbitrary"),
        name="ada",
    )(c_pad, w_ada, b_ada.reshape(1, n))


def _tables_kernel(inv_ref, sgn_r_ref, sgn_m_ref,
                   cos_r_ref, sin_r_ref, cos_m_ref, sin_m_ref, cos_rt_ref, sin_rt_ref):
    tr = cos_r_ref.shape[0]
    half = V7X_LANES // 2
    pos = (lax.broadcasted_iota(jnp.int32, (tr, V7X_LANES), 0) + pl.program_id(0) * tr).astype(F32)
    ang = pos * inv_ref[...]
    c = jnp.cos(ang)
    s = jnp.sin(ang)
    c_sw = pltpu.roll(c, half, 1)
    s_sw = pltpu.roll(s, half, 1)
    low = lax.broadcasted_iota(jnp.int32, (tr, V7X_LANES), 1) < half
    cos_r = jnp.where(low, c, c_sw)
    sin_r = jnp.where(low, s, s_sw) * sgn_r_ref[...]
    cos_r_ref[...] = cos_r
    sin_r_ref[...] = sin_r
    cos_m_ref[...] = jnp.where(low, c_sw, c)
    sin_m_ref[...] = jnp.where(low, s_sw, s) * sgn_m_ref[...]
    cos_rt_ref[...] = cos_r.T
    sin_rt_ref[...] = sin_r.T


def _rotary_tables(seq):
    half_r = RET_HEAD_DIM // 2
    half_m = MLA_ROPE_DIM // 2
    assert half_r + 2 * half_m == V7X_LANES and RET_HEAD_DIM == V7X_LANES
    inv_r = ROPE_BASE ** (-jnp.arange(0, RET_HEAD_DIM, 2, dtype=F32) / RET_HEAD_DIM)
    inv_m = ROPE_BASE ** (-jnp.arange(0, MLA_ROPE_DIM, 2, dtype=F32) / MLA_ROPE_DIM)
    inv = jnp.concatenate([inv_r, inv_m, inv_m])
    sgn_r = jnp.concatenate([-jnp.ones(half_r, F32), jnp.ones(half_r, F32)])
    sgn_m = jnp.tile(jnp.concatenate([-jnp.ones(half_m, F32), jnp.ones(half_m, F32)]),
                     V7X_LANES // MLA_ROPE_DIM)
    tr = _tile(seq, 1024)
    row = lambda v: v.reshape(1, V7X_LANES)
    tok = jax.ShapeDtypeStruct((seq, V7X_LANES), F32)
    feat = jax.ShapeDtypeStruct((RET_HEAD_DIM, seq), F32)
    vec = pl.BlockSpec((1, V7X_LANES), lambda i: (0, 0))
    tok_spec = pl.BlockSpec((tr, V7X_LANES), lambda i: (i, 0))
    feat_spec = pl.BlockSpec((RET_HEAD_DIM, tr), lambda i: (0, i))
    return pl.pallas_call(
        _tables_kernel,
        grid=(seq // tr,),
        in_specs=[vec, vec, vec],
        out_specs=[tok_spec, tok_spec, tok_spec, tok_spec, feat_spec, feat_spec],
        out_shape=[tok, tok, tok, tok, feat, feat],
        compiler_params=_params("arbitrary"),
        name="tables",
    )(row(inv), row(sgn_r), row(sgn_m))


def _layer_norm(x):
    mu = jnp.mean(x, axis=-1, keepdims=True)
    xc = x - mu
    var = jnp.mean(xc * xc, axis=-1, keepdims=True)
    return xc * lax.rsqrt(var + LN_EPS)


def _rms_norm(x, w):
    return x * lax.rsqrt(jnp.mean(x * x, axis=-1, keepdims=True) + RMS_EPS) * w


def _rope64(x, cos_m, sin_m, first_half):
    swapped = jnp.where(first_half,
                        pltpu.roll(x, V7X_LANES - MLA_ROPE_DIM // 2, 1),
                        pltpu.roll(x, MLA_ROPE_DIM // 2, 1))
    return x * cos_m + swapped * sin_m


def _proj_kernel(x_ref, mod_ref, cos_r_ref, sin_r_ref, cos_m_ref, sin_m_ref, cos_rt_ref, sin_rt_ref,
                 w_in_ref, w_kt_ref, qn_w_ref, w_uqn_ref, w_uqr_ref, kvn_w_ref, w_uk_ref, w_uvt_ref,
                 rq_ref, rkt_ref, rv_ref, rg_ref, q_ref, k_ref, vt_ref):
    sh1 = mod_ref[0, 0:1, :]
    sc1 = mod_ref[0, 1:2, :]
    hd = RET_HEAD_DIM
    k_scale = RET_HEAD_DIM ** -0.5
    q_scale = MLA_SCALE * LOG2E
    n_sub = max(x_ref.shape[1] // PROJ_SUB, 1)
    sub = x_ref.shape[1] // n_sub
    rows = [slice(i * sub, (i + 1) * sub) for i in range(n_sub)]
    lane = lax.broadcasted_iota(jnp.int32, (sub, V7X_LANES), 1)
    first_half = (lane % MLA_ROPE_DIM) < (MLA_ROPE_DIM // 2)
    low_lanes = lane < MLA_ROPE_DIM
    pad_row = lax.broadcasted_iota(jnp.int32, (VT_ROWS - MLA_V_DIM, sub), 0)
    ones_row = jnp.where(pad_row == 0, 1.0, 0.0).astype(BF16)

    hs = [(_layer_norm(x_ref[0, r, :]) * (1.0 + sc1) + sh1).astype(BF16) for r in rows]

    def retention_part(r, h):
        proj = lambda lo, n: _dot(h, w_in_ref[:, lo:lo + n])
        cos_r = cos_r_ref[r, :]
        sin_r = sin_r_ref[r, :]
        rq = proj(_OFF_RQ, RET_WIDTH)
        for hh in range(RET_HEADS):
            t = rq[:, hh * hd:(hh + 1) * hd]
            rq_ref[0, r, hh * hd:(hh + 1) * hd] = (t * cos_r + pltpu.roll(t, hd // 2, 1) * sin_r).astype(BF16)
        cos_rt = cos_rt_ref[:, r]
        sin_rt = sin_rt_ref[:, r]
        rkt = _nt_dot(w_kt_ref[...], h)
        chunk0 = r.start // RET_CHUNK
        for hh in range(RET_HEADS):
            t = rkt[hh * hd:(hh + 1) * hd, :]
            rot = ((t * cos_rt + pltpu.roll(t, hd // 2, 0) * sin_rt) * k_scale).astype(BF16)
            for j in range(sub // RET_CHUNK):
                rkt_ref[0, chunk0 + j, hh * hd:(hh + 1) * hd, :] = rot[:, j * RET_CHUNK:(j + 1) * RET_CHUNK]
        rv_ref[0, r, :] = proj(_OFF_RV, RET_WIDTH).astype(BF16)
        g = proj(_OFF_RG, RET_WIDTH)
        rg_ref[0, r, :] = (g / (1.0 + jnp.exp(-g))).astype(BF16)

    def mla_latents(r, h):
        proj = lambda lo, n: _dot(h, w_in_ref[:, lo:lo + n])
        cq = _rms_norm(proj(_OFF_CQ, Q_LORA), qn_w_ref[...]).astype(BF16)
        ckv = _rms_norm(proj(_OFF_CKV, KV_LORA), kvn_w_ref[...]).astype(BF16)
        k_pe = _rope64(proj(_OFF_KR, V7X_LANES), cos_m_ref[r, :], sin_m_ref[r, :], first_half)
        return cq, ckv, k_pe

    def mla_heads(r, cq, ckv, k_pe):
        cos_m = cos_m_ref[r, :]
        sin_m = sin_m_ref[r, :]
        q_nope = _dot(cq, w_uqn_ref[...]) * q_scale
        q_rope = _dot(cq, w_uqr_ref[...]) * q_scale
        k_nope = _dot(ckv, w_uk_ref[...])
        v_t = _nt_dot(w_uvt_ref[...], ckv)
        for pair in range(MLA_HEADS // 2):
            rp = _rope64(q_rope[:, pair * V7X_LANES:(pair + 1) * V7X_LANES], cos_m, sin_m, first_half)
            for side in range(2):
                hh = 2 * pair + side
                keep = low_lanes if side == 0 else jnp.logical_not(low_lanes)
                q_ref[0, hh, r, 0:MLA_NOPE_DIM] = q_nope[:, hh * MLA_NOPE_DIM:(hh + 1) * MLA_NOPE_DIM].astype(BF16)
                q_ref[0, hh, r, MLA_NOPE_DIM:QK_PAD] = jnp.where(keep, rp, 0.0).astype(BF16)
        k_pe_hi = pltpu.roll(k_pe, MLA_ROPE_DIM, 1)
        for hh in range(MLA_HEADS):
            k_ref[0, hh, r, 0:MLA_NOPE_DIM] = k_nope[:, hh * MLA_NOPE_DIM:(hh + 1) * MLA_NOPE_DIM].astype(BF16)
            k_ref[0, hh, r, MLA_NOPE_DIM:QK_PAD] = (k_pe if hh % 2 == 0 else k_pe_hi).astype(BF16)
            vt_ref[0, hh, 0:MLA_V_DIM, r] = v_t[hh * MLA_V_DIM:(hh + 1) * MLA_V_DIM, :].astype(BF16)
            vt_ref[0, hh, MLA_V_DIM:VT_ROWS, r] = ones_row

    for r, h in zip(rows, hs):
        retention_part(r, h)
    latents = [mla_latents(r, h) for r, h in zip(rows, hs)]
    for r, lat in zip(rows, latents):
        mla_heads(r, *lat)


PROJ_TILE = 1024
PROJ_SUB = 256


def _proj(x, mod, mod_row0, tables, w):
    b, s, _ = x.shape
    tm = _tile(s, PROJ_TILE)
    cos_r, sin_r, cos_m, sin_m, cos_rt, sin_rt = tables
    tok_tab = pl.BlockSpec((tm, V7X_LANES), lambda bi, i: (i, 0))
    feat_tab = pl.BlockSpec((RET_HEAD_DIM, tm), lambda bi, i: (0, i))
    tok_out = lambda: pl.BlockSpec((1, tm, RET_WIDTH), lambda bi, i: (bi, i, 0))
    out_shape = [
        jax.ShapeDtypeStruct((b, s, RET_WIDTH), BF16),
        jax.ShapeDtypeStruct((b, s // RET_CHUNK, RET_WIDTH, RET_CHUNK), BF16),
        jax.ShapeDtypeStruct((b, s, RET_WIDTH), BF16),
        jax.ShapeDtypeStruct((b, s, RET_WIDTH), BF16),
        jax.ShapeDtypeStruct((b, MLA_HEADS, s, QK_PAD), BF16),
        jax.ShapeDtypeStruct((b, MLA_HEADS, s, QK_PAD), BF16),
        jax.ShapeDtypeStruct((b, MLA_HEADS, VT_ROWS, s), BF16),
    ]
    out_specs = [
        tok_out(),
        pl.BlockSpec((1, tm // RET_CHUNK, RET_WIDTH, RET_CHUNK), lambda bi, i: (bi, i, 0, 0)),
        tok_out(),
        tok_out(),
        pl.BlockSpec((1, MLA_HEADS, tm, QK_PAD), lambda bi, i: (bi, 0, i, 0)),
        pl.BlockSpec((1, MLA_HEADS, tm, QK_PAD), lambda bi, i: (bi, 0, i, 0)),
        pl.BlockSpec((1, MLA_HEADS, VT_ROWS, tm), lambda bi, i: (bi, 0, 0, i)),
    ]
    in_specs = [
        pl.BlockSpec((1, tm, D_MODEL), lambda bi, i: (bi, i, 0)),
        pl.BlockSpec((1, N_MOD, D_MODEL), lambda bi, i: (bi + mod_row0, 0, 0)),
        tok_tab, tok_tab, tok_tab, tok_tab, feat_tab, feat_tab,
        _const_spec(w["w_in"].shape), _const_spec(w["w_kt"].shape),
        _const_spec(w["q_norm_w"].shape), _const_spec(w["w_uqn"].shape), _const_spec(w["w_uqr"].shape),
        _const_spec(w["kv_norm_w"].shape), _const_spec(w["w_uk"].shape), _const_spec(w["w_uvt"].shape),
    ]
    return pl.pallas_call(
        _proj_kernel,
        grid=(b, s // tm),
        in_specs=in_specs,
        out_specs=out_specs,
        out_shape=out_shape,
        compiler_params=_params("arbitrary", "arbitrary"),
        name="proj",
    )(x, mod, cos_r, sin_r, cos_m, sin_m, cos_rt, sin_rt,
      w["w_in"], w["w_kt"], w["q_norm_w"], w["w_uqn"], w["w_uqr"], w["kv_norm_w"], w["w_uk"], w["w_uvt"])


def _log_sigmoid(x):
    return jnp.minimum(x, 0.0) - jnp.log(1.0 + jnp.exp(-jnp.abs(x)))


def _ret_decay_tables(dec_row, reverse, d_in_ref, d_q_ref, d_k_ref, d_c_ref, hh):
    c = RET_CHUNK
    lg = _log_sigmoid(dec_row)[:, 0:1]
    row = lax.broadcasted_iota(jnp.int32, (c, c), 0)
    col = lax.broadcasted_iota(jnp.int32, (c, c), 1)
    rel = (col - row) if reverse else (row - col)
    mask = (rel > 0) if reverse else (rel >= 0)
    relf = jnp.maximum(rel, 0).astype(F32)
    d_in_ref[hh] = jnp.where(mask, jnp.exp(lg * relf), 0.0)
    qi = lax.broadcasted_iota(jnp.int32, (c, V7X_LANES), 0).astype(F32)
    kj = lax.broadcasted_iota(jnp.int32, (RET_HEAD_DIM, c), 1).astype(F32)
    q_steps = (c - qi) if reverse else (qi + 1.0)
    k_steps = kj if reverse else (c - 1.0 - kj)
    d_q_ref[hh] = jnp.exp(lg * q_steps)
    d_k_ref[hh] = jnp.exp(lg * k_steps)
    d_c_ref[hh] = jnp.exp(lg * float(c)) + jnp.zeros((V7X_SUBLANES, V7X_LANES), F32)


def _ret_kernel(reverse, *refs):
    if reverse:
        (dec_ref, rq_ref, rkt_ref, rv_ref, fwd_ref, rg_ref, gnw_ref, out_ref,
         state_ref, d_in_ref, d_q_ref, d_k_ref, d_c_ref) = refs
    else:
        (dec_ref, rq_ref, rkt_ref, rv_ref, out_ref,
         state_ref, d_in_ref, d_q_ref, d_k_ref, d_c_ref) = refs
    c = RET_CHUNK
    hd = RET_HEAD_DIM
    n_chunks = rq_ref.shape[1] // c

    @pl.when(pl.program_id(1) == 0)
    def _():
        state_ref[...] = jnp.zeros_like(state_ref)
        for hh in range(RET_HEADS):
            _ret_decay_tables(dec_ref[hh:hh + 1, :], reverse, d_in_ref, d_q_ref, d_k_ref, d_c_ref, hh)

    def chunk_body(ci, carry):
        cidx = (n_chunks - 1 - ci) if reverse else ci
        t0 = pl.multiple_of(cidx * c, c)
        for hh in range(RET_HEADS):
            cols = slice(hh * hd, (hh + 1) * hd)
            q = rq_ref[0, pl.ds(t0, c), cols]
            kt = rkt_ref[0, cidx, cols, :]
            v = rv_ref[0, pl.ds(t0, c), cols]
            state = state_ref[hh]
            s = _dot(q, kt) * d_in_ref[hh]
            inner = _dot(s.astype(BF16), v)
            cross = _dot(q, state.astype(BF16)) * d_q_ref[hh]
            kt_dec = (kt.astype(F32) * d_k_ref[hh]).astype(BF16)
            state_ref[hh] = state * d_c_ref[hh][0:1, :] + _dot(kt_dec, v)
            y = inner + cross
            if reverse:
                y = y + fwd_ref[0, pl.ds(t0, c), cols].astype(F32)
                mu = jnp.mean(y, axis=-1, keepdims=True)
                yc = y - mu
                var = jnp.mean(yc * yc, axis=-1, keepdims=True)
                yn = yc * lax.rsqrt(var + LN_EPS) * gnw_ref[:, cols]
                gate = rg_ref[0, pl.ds(t0, c), cols].astype(F32)
                out_ref[0, pl.ds(t0, c), cols] = (gate * yn).astype(out_ref.dtype)
            else:
                out_ref[0, pl.ds(t0, c), cols] = y.astype(out_ref.dtype)
        return carry

    lax.fori_loop(0, n_chunks, chunk_body, 0, unroll=True)


def _retention(dec_rows, rq, rkt, rv, reverse, fwd=None, rg=None, gn_w=None):
    b, s, _ = rq.shape
    tb = _tile(s, 2048)
    nb = s // tb
    blk = (lambda i: nb - 1 - i) if reverse else (lambda i: i)
    tok = pl.BlockSpec((1, tb, RET_WIDTH), lambda bi, i: (bi, blk(i), 0))
    feat = pl.BlockSpec((1, tb // RET_CHUNK, RET_WIDTH, RET_CHUNK), lambda bi, i: (bi, blk(i), 0, 0))
    dec_spec = pl.BlockSpec((RET_HEADS, V7X_LANES), lambda bi, i: (0, 0))
    in_specs = [dec_spec, tok, feat, tok]
    args = [dec_rows, rq, rkt, rv]
    if reverse:
        in_specs += [tok, tok, pl.BlockSpec((1, RET_WIDTH), lambda bi, i: (0, 0))]
        args += [fwd, rg, gn_w]
    c = RET_CHUNK
    scratch = [
        pltpu.VMEM((RET_HEADS, RET_HEAD_DIM, RET_HEAD_DIM), F32),
        pltpu.VMEM((RET_HEADS, c, c), F32),
        pltpu.VMEM((RET_HEADS, c, V7X_LANES), F32),
        pltpu.VMEM((RET_HEADS, RET_HEAD_DIM, c), F32),
        pltpu.VMEM((RET_HEADS, V7X_SUBLANES, V7X_LANES), F32),
    ]
    return pl.pallas_call(
        functools.partial(_ret_kernel, reverse),
        grid=(b, nb),
        in_specs=in_specs,
        out_specs=tok,
        out_shape=jax.ShapeDtypeStruct((b, s, RET_WIDTH), BF16),
        scratch_shapes=scratch,
        compiler_params=_params("arbitrary", "arbitrary"),
        name="ret_bwd" if reverse else "ret_fwd",
    )(*args)


NEG_BIG = -1e30


ATTN_GROUP = 2 * V7X_LANES
ATTN_LAG = 3
ATTN_KV_ITEM = 2048
ATTN_KV_STEP = 2048
ATTN_Q_TILE = 4096


def _attn_kernel(q_ref, k_ref, vprev_ref, vcur_ref, o_ref, s_ref, m_old_ref, m_cur_ref, acc_ref):
    t = pl.program_id(3)
    nk = pl.num_programs(3) - 1
    gw = ATTN_GROUP
    ng = q_ref.shape[2] // gw
    rows = s_ref.shape[0]
    n_sub = k_ref.shape[2] // rows
    n_items = n_sub * ng
    lag = min(ATTN_LAG, ng - 1)
    first_carried = ng - lag
    group = lambda g: slice(g * gw, (g + 1) * gw)
    kv = lambda a: slice(a * rows, (a + 1) * rows)

    def softmax_pv(g, vt, m_prev, m_new):
        cols = group(g)
        alpha = jnp.exp2(m_prev - m_new)
        p = jnp.exp2(s_ref[:, cols] - m_new)
        acc_ref[:, cols] = alpha * acc_ref[:, cols] + _dot(vt, p.astype(BF16))

    def scores(a, g, m_prev):
        cols = group(g)
        s = _nt_dot(k_ref[0, 0, kv(a), :], q_ref[0, 0, cols, :])
        s_ref[:, cols] = s
        return jnp.maximum(m_prev, jnp.max(s, axis=0, keepdims=True))

    def carried_pv(g):
        cols = group(g)
        softmax_pv(g, vprev_ref[0, 0], m_old_ref[:, cols], m_cur_ref[:, cols])

    def step(has_prev):
        m_before, m_after = {}, {}
        for p in range(n_items):
            a, g = divmod(p, ng)
            cols = group(g)
            m_before[p] = m_cur_ref[:, cols] if a == 0 else m_after[p - ng]
            m_after[p] = scores(a, g, m_before[p])
            if p >= n_items - lag:
                m_old_ref[:, cols] = m_before[p]
                m_cur_ref[:, cols] = m_after[p]
            d = p - lag
            if d >= 0:
                da, dg = divmod(d, ng)
                softmax_pv(dg, vcur_ref[0, 0, :, kv(da)], m_before[d], m_after[d])
                if da == n_sub - 1:
                    m_cur_ref[:, group(dg)] = m_after[d]
            elif has_prev:
                carried_pv(ng + d)

    @pl.when(t == 0)
    def _():
        m_cur_ref[...] = jnp.full_like(m_cur_ref, NEG_BIG)
        acc_ref[...] = jnp.zeros_like(acc_ref)
        step(False)

    @pl.when(jnp.logical_and(t > 0, t < nk))
    def _():
        step(True)

    @pl.when(t == nk)
    def _():
        for g in range(first_carried, ng):
            carried_pv(g)
        o = acc_ref[0:MLA_V_DIM, :] / acc_ref[MLA_V_DIM:MLA_V_DIM + 1, :]
        o_ref[0] = o.T.astype(o_ref.dtype)


def _attention(q, k, vt):
    b, nh, s, _ = q.shape
    tq = _tile(s, ATTN_Q_TILE)
    tk = _tile(s, ATTN_KV_STEP)
    rows = _tile(tk, ATTN_KV_ITEM)
    n_sub = tk // rows
    nk = s // tk
    return pl.pallas_call(
        _attn_kernel,
        grid=(b, nh, s // tq, nk + 1),
        in_specs=[
            pl.BlockSpec((1, 1, tq, QK_PAD), lambda bi, hi, qi, t: (bi, hi, qi, 0)),
            pl.BlockSpec((1, 1, tk, QK_PAD), lambda bi, hi, qi, t: (bi, hi, jnp.minimum(t, nk - 1), 0)),
            pl.BlockSpec((1, 1, VT_ROWS, rows),
                         lambda bi, hi, qi, t: (bi, hi, 0, jnp.maximum(t * n_sub - 1, 0))),
            pl.BlockSpec((1, 1, VT_ROWS, tk), lambda bi, hi, qi, t: (bi, hi, 0, jnp.minimum(t, nk - 1))),
        ],
        out_specs=pl.BlockSpec((1, tq, MLA_V_DIM), lambda bi, hi, qi, t: (bi, qi, hi)),
        out_shape=jax.ShapeDtypeStruct((b, s, MLA_WIDTH), BF16),
        scratch_shapes=[
            pltpu.VMEM((rows, tq), F32),
            pltpu.VMEM((1, tq), F32),
            pltpu.VMEM((1, tq), F32),
            pltpu.VMEM((VT_ROWS, tq), F32),
        ],
        compiler_params=_params("arbitrary", "arbitrary", "arbitrary", "arbitrary"),
        name="attn",
    )(q, k, vt, vt)


FF_CHUNK = 1024
MLP_TILE = 1024
MLP_SUB = 256


def _mlp_kernel(alpha, x_ref, ret_ref, att_ref, mod_ref, w_o_ref, ln1_w_ref, ln1_b_ref,
                w_up_ref, w_down_ref, ln2_w_ref, ln2_b_ref, o_ref):
    g1 = mod_ref[0, 2:3, :]
    sh2 = mod_ref[0, 3:4, :]
    sc2 = mod_ref[0, 4:5, :]
    g2 = mod_ref[0, 5:6, :]
    n_sub = max(x_ref.shape[1] // MLP_SUB, 1)
    sub = x_ref.shape[1] // n_sub
    rows = [slice(i * sub, (i + 1) * sub) for i in range(n_sub)]
    mixes = [_dot(ret_ref[0, r, :], w_o_ref[0:RET_WIDTH, :]) + _dot(att_ref[0, r, :], w_o_ref[RET_WIDTH:, :])
             for r in rows]
    x1s, hs = [], []
    for r, mix in zip(rows, mixes):
        x1 = _layer_norm(alpha * x_ref[0, r, :] + g1 * mix) * ln1_w_ref[...] + ln1_b_ref[...]
        x1s.append(x1)
        hs.append((_layer_norm(x1) * (1.0 + sc2) + sh2).astype(BF16))
    for r, x1, h in zip(rows, x1s, hs):
        ff = jnp.zeros_like(x1)
        for c0 in range(0, D_FF, FF_CHUNK):
            u = jnp.maximum(_dot(h, w_up_ref[:, c0:c0 + FF_CHUNK]), 0.0)
            ff = ff + _dot((u * u).astype(BF16), w_down_ref[c0:c0 + FF_CHUNK, :])
        o_ref[0, r, :] = _layer_norm(alpha * x1 + g2 * ff) * ln2_w_ref[...] + ln2_b_ref[...]


def _mlp(alpha, x, ret, att, mod, mod_row0, w):
    b, s, _ = x.shape
    tm = _tile(s, MLP_TILE)
    xs = pl.BlockSpec((1, tm, D_MODEL), lambda bi, i: (bi, i, 0))
    half = pl.BlockSpec((1, tm, RET_WIDTH), lambda bi, i: (bi, i, 0))
    vec = lambda: pl.BlockSpec((1, D_MODEL), lambda bi, i: (0, 0))
    return pl.pallas_call(
        functools.partial(_mlp_kernel, alpha),
        grid=(b, s // tm),
        in_specs=[
            xs, half, half,
            pl.BlockSpec((1, N_MOD, D_MODEL), lambda bi, i: (bi + mod_row0, 0, 0)),
            _const_spec(w["w_o"].shape), vec(), vec(),
            _const_spec(w["w_up"].shape), _const_spec(w["w_down"].shape), vec(), vec(),
        ],
        out_specs=xs,
        out_shape=jax.ShapeDtypeStruct((b, s, D_MODEL), F32),
        compiler_params=_params("arbitrary", "arbitrary"),
        name="mlp",
    )(x, ret, att, mod, w["w_o"], w["ln1_w"], w["ln1_b"], w["w_up"], w["w_down"], w["ln2_w"], w["ln2_b"])


def _prep_layer_weights(w_in, ret_decay_f, ret_decay_b, ret_gn_w, q_norm_w, w_uq, kv_norm_w, w_ukv,
                        w_o, ln1_w, ln1_b, w_up, w_down, ln2_w, ln2_b):
    rw = RET_WIDTH
    w_rq, w_rk, w_rest = w_in[:, :rw], w_in[:, rw:2 * rw], w_in[:, 2 * rw:]
    pad = jnp.zeros((D_MODEL, _W_IN_COLS - (w_in.shape[1] - rw)), w_in.dtype)
    w_in_p = jnp.concatenate([w_rq, w_rest, pad], axis=1).astype(BF16)
    uq = w_uq.reshape(Q_LORA, MLA_HEADS, MLA_QK_DIM)
    ukv = w_ukv.reshape(KV_LORA, MLA_HEADS, MLA_NOPE_DIM + MLA_V_DIM)
    lanes = lambda v: jnp.broadcast_to(v.reshape(RET_HEADS, 1), (RET_HEADS, V7X_LANES)).astype(F32)
    return {
        "w_in": w_in_p,
        "w_kt": w_rk.T.astype(BF16),
        "q_norm_w": q_norm_w.reshape(1, Q_LORA),
        "w_uqn": uq[:, :, :MLA_NOPE_DIM].reshape(Q_LORA, MLA_HEADS * MLA_NOPE_DIM).astype(BF16),
        "w_uqr": uq[:, :, MLA_NOPE_DIM:].reshape(Q_LORA, MLA_HEADS * MLA_ROPE_DIM).astype(BF16),
        "kv_norm_w": kv_norm_w.reshape(1, KV_LORA),
        "w_uk": ukv[:, :, :MLA_NOPE_DIM].reshape(KV_LORA, MLA_HEADS * MLA_NOPE_DIM).astype(BF16),
        "w_uvt": ukv[:, :, MLA_NOPE_DIM:].reshape(KV_LORA, MLA_HEADS * MLA_V_DIM).T.astype(BF16),
        "dec_f": lanes(ret_decay_f),
        "dec_b": lanes(ret_decay_b),
        "gn_w": ret_gn_w.reshape(1, RET_WIDTH),
        "w_o": w_o.astype(BF16),
        "ln1_w": ln1_w.reshape(1, D_MODEL), "ln1_b": ln1_b.reshape(1, D_MODEL),
        "w_up": w_up.astype(BF16), "w_down": w_down.astype(BF16),
        "ln2_w": ln2_w.reshape(1, D_MODEL), "ln2_b": ln2_b.reshape(1, D_MODEL),
    }


def _layer(alpha, x, mod, mod_row0, tables, w):
    rq, rkt, rv, rg, q, k, vt = _proj(x, mod, mod_row0, tables, w)
    fwd = _retention(w["dec_f"], rq, rkt, rv, reverse=False)
    ret = _retention(w["dec_b"], rq, rkt, rv, reverse=True, fwd=fwd, rg=rg, gn_w=w["gn_w"])
    att = _attention(q, k, vt)
    return _mlp(alpha, x, ret, att, mod, mod_row0, w)


def _trunks(xs, cs, w_ada, b_ada, layer_weights):
    depth = w_ada.shape[0]
    alpha = float((2 * depth) ** 0.25)
    n_rows = sum(c.shape[0] for c in cs)
    assert n_rows <= ROWS_PAD
    c_pad = jnp.concatenate(list(cs) + [jnp.zeros((ROWS_PAD - n_rows, D_MODEL), F32)], axis=0)
    tables = _rotary_tables(max(x.shape[1] for x in xs))
    outs = list(xs)
    for l in range(depth):
        mod = _ada(c_pad, w_ada[l], b_ada[l]).reshape(ROWS_PAD, N_MOD, D_MODEL)
        w = _prep_layer_weights(*[t[l] for t in layer_weights])
        row0 = 0
        for i, c in enumerate(cs):
            outs[i] = _layer(alpha, outs[i], mod, row0, tables, w)
            row0 += c.shape[0]
    return tuple(outs)


def kernel(x_prompt, x_sample, c_prompt, c_sample, w_ada, b_ada, w_in, ret_decay_f, ret_decay_b, ret_gn_w,
           q_norm_w, w_uq, kv_norm_w, w_ukv, w_o, ln1_w, ln1_b, w_up, w_down, ln2_w, ln2_b):
    layer_weights = (w_in, ret_decay_f, ret_decay_b, ret_gn_w, q_norm_w, w_uq, kv_norm_w, w_ukv,
                     w_o, ln1_w, ln1_b, w_up, w_down, ln2_w, ln2_b)
    return _trunks((x_prompt, x_sample), (c_prompt, c_sample), w_ada, b_ada, layer_weights)
```

```python
import functools
import math

import jax
import jax.numpy as jnp
from jax import lax
from jax.experimental import pallas as pl
from jax.experimental.pallas import tpu as pltpu

F32 = jnp.float32
BF16 = jnp.bfloat16

D_MODEL = 1024
RET_HEADS = 4
RET_HEAD_DIM = 128
RET_WIDTH = RET_HEADS * RET_HEAD_DIM
MLA_HEADS = 4
MLA_NOPE_DIM = 128
MLA_ROPE_DIM = 64
MLA_V_DIM = 128
MLA_WIDTH = MLA_HEADS * MLA_V_DIM
MLA_QK_DIM = MLA_NOPE_DIM + MLA_ROPE_DIM
Q_LORA = 384
KV_LORA = 256
D_FF = 4 * D_MODEL
ROPE_BASE = 10000.0
LN_EPS = 1e-5
RMS_EPS = 1e-6
N_MOD = 6
MLA_SCALE = float(MLA_QK_DIM ** -0.5)
LOG2E = 1.4426950408889634

V7X_LANES = 128
V7X_SUBLANES = 8
V7X_VMEM_BYTES = 64 * 1024 * 1024
VMEM_LIMIT = V7X_VMEM_BYTES - 8 * 1024 * 1024

QK_PAD = 2 * V7X_LANES
VT_ROWS = MLA_V_DIM + V7X_SUBLANES
_OFF_RQ, _OFF_RV, _OFF_RG = 0, RET_WIDTH, 2 * RET_WIDTH
_OFF_CQ = 3 * RET_WIDTH
_OFF_KR = _OFF_CQ + Q_LORA
_OFF_CKV = _OFF_KR + V7X_LANES
_W_IN_COLS = _OFF_CKV + KV_LORA
ROWS_PAD = V7X_SUBLANES
RET_CHUNK = 128


def _tile(n, pref):
    t = min(n, pref)
    assert n % t == 0, (n, t)
    return t


def _const_spec(shape):
    nd = len(shape)
    return pl.BlockSpec(shape, lambda *_: (0,) * nd, pipeline_mode=pl.Buffered(1))


def _params(*sem):
    return pltpu.CompilerParams(dimension_semantics=sem, vmem_limit_bytes=VMEM_LIMIT)


def _nt_dot(a, b):
    return lax.dot_general(a, b, (((1,), (1,)), ((), ())), preferred_element_type=F32)


def _dot(a, b):
    return jnp.dot(a, b, preferred_element_type=F32)


def _ada_kernel(c_ref, w_ref, b_ref, o_ref):
    c = c_ref[...]
    s = c / (1.0 + jnp.exp(-c))
    o_ref[...] = _dot(s.astype(BF16), w_ref[...].astype(BF16)) + b_ref[...]


def _ada(c_pad, w_ada, b_ada):
    n = w_ada.shape[1]
    tn = _tile(n, D_MODEL)
    return pl.pallas_call(
        _ada_kernel,
        grid=(n // tn,),
        in_specs=[
            pl.BlockSpec((ROWS_PAD, D_MODEL), lambda j: (0, 0)),
            pl.BlockSpec((D_MODEL, tn), lambda j: (0, j)),
            pl.BlockSpec((1, tn), lambda j: (0, j)),
        ],
        out_specs=pl.BlockSpec((ROWS_PAD, tn), lambda j: (0, j)),
        out_shape=jax.ShapeDtypeStruct((ROWS_PAD, n), F32),
        compiler_params=_params("arbitrary"),
        name="ada",
    )(c_pad, w_ada, b_ada.reshape(1, n))


def _tables_kernel(inv_ref, sgn_r_ref, sgn_m_ref,
                   cos_r_ref, sin_r_ref, cos_m_ref, sin_m_ref, cos_rt_ref, sin_rt_ref):
    tr = cos_r_ref.shape[0]
    half = V7X_LANES // 2
    pos = (lax.broadcasted_iota(jnp.int32, (tr, V7X_LANES), 0) + pl.program_id(0) * tr).astype(F32)
    ang = pos * inv_ref[...]
    c = jnp.cos(ang)
    s = jnp.sin(ang)
    c_sw = pltpu.roll(c, half, 1)
    s_sw = pltpu.roll(s, half, 1)
    low = lax.broadcasted_iota(jnp.int32, (tr, V7X_LANES), 1) < half
    cos_r = jnp.where(low, c, c_sw)
    sin_r = jnp.where(low, s, s_sw) * sgn_r_ref[...]
    cos_r_ref[...] = cos_r
    sin_r_ref[...] = sin_r
    cos_m_ref[...] = jnp.where(low, c_sw, c)
    sin_m_ref[...] = jnp.where(low, s_sw, s) * sgn_m_ref[...]
    cos_rt_ref[...] = cos_r.T
    sin_rt_ref[...] = sin_r.T


def _rotary_tables(seq):
    half_r = RET_HEAD_DIM // 2
    half_m = MLA_ROPE_DIM // 2
    assert half_r + 2 * half_m == V7X_LANES and RET_HEAD_DIM == V7X_LANES
    inv_r = ROPE_BASE ** (-jnp.arange(0, RET_HEAD_DIM, 2, dtype=F32) / RET_HEAD_DIM)
    inv_m = ROPE_BASE ** (-jnp.arange(0, MLA_ROPE_DIM, 2, dtype=F32) / MLA_ROPE_DIM)
    inv = jnp.concatenate([inv_r, inv_m, inv_m])
    sgn_r = jnp.concatenate([-jnp.ones(half_r, F32), jnp.ones(half_r, F32)])
    sgn_m = jnp.tile(jnp.concatenate([-jnp.ones(half_m, F32), jnp.ones(half_m, F32)]),
                     V7X_LANES // MLA_ROPE_DIM)
    tr = _tile(seq, 1024)
    row = lambda v: v.reshape(1, V7X_LANES)
    tok = jax.ShapeDtypeStruct((seq, V7X_LANES), F32)
    feat = jax.ShapeDtypeStruct((RET_HEAD_DIM, seq), F32)
    vec = pl.BlockSpec((1, V7X_LANES), lambda i: (0, 0))
    tok_spec = pl.BlockSpec((tr, V7X_LANES), lambda i: (i, 0))
    feat_spec = pl.BlockSpec((RET_HEAD_DIM, tr), lambda i: (0, i))
    return pl.pallas_call(
        _tables_kernel,
        grid=(seq // tr,),
        in_specs=[vec, vec, vec],
        out_specs=[tok_spec, tok_spec, tok_spec, tok_spec, feat_spec, feat_spec],
        out_shape=[tok, tok, tok, tok, feat, feat],
        compiler_params=_params("arbitrary"),
        name="tables",
    )(row(inv), row(sgn_r), row(sgn_m))


def _layer_norm(x):
    mu = jnp.mean(x, axis=-1, keepdims=True)
    xc = x - mu
    var = jnp.mean(xc * xc, axis=-1, keepdims=True)
    return xc * lax.rsqrt(var + LN_EPS)


def _rms_norm(x, w):
    return x * lax.rsqrt(jnp.mean(x * x, axis=-1, keepdims=True) + RMS_EPS) * w


def _rope64(x, cos_m, sin_m, first_half):
    swapped = jnp.where(first_half,
                        pltpu.roll(x, V7X_LANES - MLA_ROPE_DIM // 2, 1),
                        pltpu.roll(x, MLA_ROPE_DIM // 2, 1))
    return x * cos_m + swapped * sin_m


def _proj_kernel(x_ref, mod_ref, cos_r_ref, sin_r_ref, cos_m_ref, sin_m_ref, cos_rt_ref, sin_rt_ref,
                 w_in_ref, w_kt_ref, qn_w_ref, w_uqn_ref, w_uqr_ref, kvn_w_ref, w_uk_ref, w_uvt_ref,
                 rq_ref, rkt_ref, rv_ref, rg_ref, q_ref, k_ref, vt_ref):
    sh1 = mod_ref[0, 0:1, :]
    sc1 = mod_ref[0, 1:2, :]
    hd = RET_HEAD_DIM
    k_scale = RET_HEAD_DIM ** -0.5
    q_scale = MLA_SCALE * LOG2E
    n_sub = max(x_ref.shape[1] // PROJ_SUB, 1)
    sub = x_ref.shape[1] // n_sub
    rows = [slice(i * sub, (i + 1) * sub) for i in range(n_sub)]
    lane = lax.broadcasted_iota(jnp.int32, (sub, V7X_LANES), 1)
    first_half = (lane % MLA_ROPE_DIM) < (MLA_ROPE_DIM // 2)
    low_lanes = lane < MLA_ROPE_DIM
    pad_row = lax.broadcasted_iota(jnp.int32, (VT_ROWS - MLA_V_DIM, sub), 0)
    ones_row = jnp.where(pad_row == 0, 1.0, 0.0).astype(BF16)

    hs = [(_layer_norm(x_ref[0, r, :]) * (1.0 + sc1) + sh1).astype(BF16) for r in rows]

    def retention_part(r, h):
        proj = lambda lo, n: _dot(h, w_in_ref[:, lo:lo + n])
        cos_r = cos_r_ref[r, :]
        sin_r = sin_r_ref[r, :]
        rq = proj(_OFF_RQ, RET_WIDTH)
        for hh in range(RET_HEADS):
            t = rq[:, hh * hd:(hh + 1) * hd]
            rq_ref[0, r, hh * hd:(hh + 1) * hd] = (t * cos_r + pltpu.roll(t, hd // 2, 1) * sin_r).astype(BF16)
        cos_rt = cos_rt_ref[:, r]
        sin_rt = sin_rt_ref[:, r]
        rkt = _nt_dot(w_kt_ref[...], h)
        chunk0 = r.start // RET_CHUNK
        for hh in range(RET_HEADS):
            t = rkt[hh * hd:(hh + 1) * hd, :]
            rot = ((t * cos_rt + pltpu.roll(t, hd // 2, 0) * sin_rt) * k_scale).astype(BF16)
            for j in range(sub // RET_CHUNK):
                rkt_ref[0, chunk0 + j, hh * hd:(hh + 1) * hd, :] = rot[:, j * RET_CHUNK:(j + 1) * RET_CHUNK]
        rv_ref[0, r, :] = proj(_OFF_RV, RET_WIDTH).astype(BF16)
        g = proj(_OFF_RG, RET_WIDTH)
        rg_ref[0, r, :] = (g / (1.0 + jnp.exp(-g))).astype(BF16)

    def mla_latents(r, h):
        proj = lambda lo, n: _dot(h, w_in_ref[:, lo:lo + n])
        cq_kr = proj(_OFF_CQ, Q_LORA + V7X_LANES)
        cq = _rms_norm(cq_kr[:, 0:Q_LORA], qn_w_ref[...]).astype(BF16)
        ckv = _rms_norm(proj(_OFF_CKV, KV_LORA), kvn_w_ref[...]).astype(BF16)
        k_pe = _rope64(cq_kr[:, Q_LORA:], cos_m_ref[r, :], sin_m_ref[r, :], first_half)
        return cq, ckv, k_pe

    def mla_heads(r, cq, ckv, k_pe):
        cos_m = cos_m_ref[r, :]
        sin_m = sin_m_ref[r, :]
        q_nope = _dot(cq, w_uqn_ref[...]) * q_scale
        q_rope = _dot(cq, w_uqr_ref[...]) * q_scale
        k_nope = _dot(ckv, w_uk_ref[...])
        v_t = _nt_dot(w_uvt_ref[...], ckv)
        for pair in range(MLA_HEADS // 2):
            rp = _rope64(q_rope[:, pair * V7X_LANES:(pair + 1) * V7X_LANES], cos_m, sin_m, first_half)
            for side in range(2):
                hh = 2 * pair + side
                keep = low_lanes if side == 0 else jnp.logical_not(low_lanes)
                q_ref[0, hh, r, 0:MLA_NOPE_DIM] = q_nope[:, hh * MLA_NOPE_DIM:(hh + 1) * MLA_NOPE_DIM].astype(BF16)
                q_ref[0, hh, r, MLA_NOPE_DIM:QK_PAD] = jnp.where(keep, rp, 0.0).astype(BF16)
        k_pe_hi = pltpu.roll(k_pe, MLA_ROPE_DIM, 1)
        for hh in range(MLA_HEADS):
            k_ref[0, hh, r, 0:MLA_NOPE_DIM] = k_nope[:, hh * MLA_NOPE_DIM:(hh + 1) * MLA_NOPE_DIM].astype(BF16)
            k_ref[0, hh, r, MLA_NOPE_DIM:QK_PAD] = (k_pe if hh % 2 == 0 else k_pe_hi).astype(BF16)
            vt_ref[0, hh, 0:MLA_V_DIM, r] = v_t[hh * MLA_V_DIM:(hh + 1) * MLA_V_DIM, :].astype(BF16)
            vt_ref[0, hh, MLA_V_DIM:VT_ROWS, r] = ones_row

    for r, h in zip(rows, hs):
        retention_part(r, h)
    latents = [mla_latents(r, h) for r, h in zip(rows, hs)]
    for r, lat in zip(rows, latents):
        mla_heads(r, *lat)


PROJ_TILE = 1024
PROJ_SUB = 256


def _proj(x, mod, mod_row0, tables, w):
    b, s, _ = x.shape
    tm = _tile(s, PROJ_TILE)
    cos_r, sin_r, cos_m, sin_m, cos_rt, sin_rt = tables
    tok_tab = pl.BlockSpec((tm, V7X_LANES), lambda bi, i: (i, 0))
    feat_tab = pl.BlockSpec((RET_HEAD_DIM, tm), lambda bi, i: (0, i))
    tok_out = lambda: pl.BlockSpec((1, tm, RET_WIDTH), lambda bi, i: (bi, i, 0))
    out_shape = [
        jax.ShapeDtypeStruct((b, s, RET_WIDTH), BF16),
        jax.ShapeDtypeStruct((b, s // RET_CHUNK, RET_WIDTH, RET_CHUNK), BF16),
        jax.ShapeDtypeStruct((b, s, RET_WIDTH), BF16),
        jax.ShapeDtypeStruct((b, s, RET_WIDTH), BF16),
        jax.ShapeDtypeStruct((b, MLA_HEADS, s, QK_PAD), BF16),
        jax.ShapeDtypeStruct((b, MLA_HEADS, s, QK_PAD), BF16),
        jax.ShapeDtypeStruct((b, MLA_HEADS, VT_ROWS, s), BF16),
    ]
    out_specs = [
        tok_out(),
        pl.BlockSpec((1, tm // RET_CHUNK, RET_WIDTH, RET_CHUNK), lambda bi, i: (bi, i, 0, 0)),
        tok_out(),
        tok_out(),
        pl.BlockSpec((1, MLA_HEADS, tm, QK_PAD), lambda bi, i: (bi, 0, i, 0)),
        pl.BlockSpec((1, MLA_HEADS, tm, QK_PAD), lambda bi, i: (bi, 0, i, 0)),
        pl.BlockSpec((1, MLA_HEADS, VT_ROWS, tm), lambda bi, i: (bi, 0, 0, i)),
    ]
    in_specs = [
        pl.BlockSpec((1, tm, D_MODEL), lambda bi, i: (bi, i, 0)),
        pl.BlockSpec((1, N_MOD, D_MODEL), lambda bi, i: (bi + mod_row0, 0, 0)),
        tok_tab, tok_tab, tok_tab, tok_tab, feat_tab, feat_tab,
        _const_spec(w["w_in"].shape), _const_spec(w["w_kt"].shape),
        _const_spec(w["q_norm_w"].shape), _const_spec(w["w_uqn"].shape), _const_spec(w["w_uqr"].shape),
        _const_spec(w["kv_norm_w"].shape), _const_spec(w["w_uk"].shape), _const_spec(w["w_uvt"].shape),
    ]
    return pl.pallas_call(
        _proj_kernel,
        grid=(b, s // tm),
        in_specs=in_specs,
        out_specs=out_specs,
        out_shape=out_shape,
        compiler_params=_params("arbitrary", "arbitrary"),
        name="proj",
    )(x, mod, cos_r, sin_r, cos_m, sin_m, cos_rt, sin_rt,
      w["w_in"], w["w_kt"], w["q_norm_w"], w["w_uqn"], w["w_uqr"], w["kv_norm_w"], w["w_uk"], w["w_uvt"])


def _log_sigmoid(x):
    return jnp.minimum(x, 0.0) - jnp.log(1.0 + jnp.exp(-jnp.abs(x)))


def _ret_decay_tables(dec_row, reverse, d_in_ref, d_q_ref, d_k_ref, d_c_ref, hh):
    c = RET_CHUNK
    lg = _log_sigmoid(dec_row)[:, 0:1]
    row = lax.broadcasted_iota(jnp.int32, (c, c), 0)
    col = lax.broadcasted_iota(jnp.int32, (c, c), 1)
    rel = (col - row) if reverse else (row - col)
    mask = (rel > 0) if reverse else (rel >= 0)
    relf = jnp.maximum(rel, 0).astype(F32)
    d_in_ref[hh] = jnp.where(mask, jnp.exp(lg * relf), 0.0)
    qi = lax.broadcasted_iota(jnp.int32, (c, V7X_LANES), 0).astype(F32)
    kj = lax.broadcasted_iota(jnp.int32, (RET_HEAD_DIM, c), 1).astype(F32)
    q_steps = (c - qi) if reverse else (qi + 1.0)
    k_steps = kj if reverse else (c - 1.0 - kj)
    d_q_ref[hh] = jnp.exp(lg * q_steps)
    d_k_ref[hh] = jnp.exp(lg * k_steps)
    d_c_ref[hh] = jnp.exp(lg * float(c)) + jnp.zeros((V7X_SUBLANES, V7X_LANES), F32)


def _ret_kernel(reverse, *refs):
    if reverse:
        (dec_ref, rq_ref, rkt_ref, rv_ref, fwd_ref, rg_ref, gnw_ref, out_ref,
         state_ref, d_in_ref, d_q_ref, d_k_ref, d_c_ref) = refs
    else:
        (dec_ref, rq_ref, rkt_ref, rv_ref, out_ref,
         state_ref, d_in_ref, d_q_ref, d_k_ref, d_c_ref) = refs
    c = RET_CHUNK
    hd = RET_HEAD_DIM
    n_chunks = rq_ref.shape[1] // c

    @pl.when(pl.program_id(1) == 0)
    def _():
        state_ref[...] = jnp.zeros_like(state_ref)
        for hh in range(RET_HEADS):
            _ret_decay_tables(dec_ref[hh:hh + 1, :], reverse, d_in_ref, d_q_ref, d_k_ref, d_c_ref, hh)

    def chunk_body(ci, carry):
        cidx = (n_chunks - 1 - ci) if reverse else ci
        t0 = pl.multiple_of(cidx * c, c)
        for hh in range(RET_HEADS):
            cols = slice(hh * hd, (hh + 1) * hd)
            q = rq_ref[0, pl.ds(t0, c), cols]
            kt = rkt_ref[0, cidx, cols, :]
            v = rv_ref[0, pl.ds(t0, c), cols]
            state = state_ref[hh]
            s = _dot(q, kt) * d_in_ref[hh]
            inner = _dot(s.astype(BF16), v)
            cross = _dot(q, state.astype(BF16)) * d_q_ref[hh]
            kt_dec = (kt.astype(F32) * d_k_ref[hh]).astype(BF16)
            state_ref[hh] = state * d_c_ref[hh][0:1, :] + _dot(kt_dec, v)
            y = inner + cross
            if reverse:
                y = y + fwd_ref[0, pl.ds(t0, c), cols].astype(F32)
                mu = jnp.mean(y, axis=-1, keepdims=True)
                yc = y - mu
                var = jnp.mean(yc * yc, axis=-1, keepdims=True)
                yn = yc * lax.rsqrt(var + LN_EPS) * gnw_ref[:, cols]
                gate = rg_ref[0, pl.ds(t0, c), cols].astype(F32)
                out_ref[0, pl.ds(t0, c), cols] = (gate * yn).astype(out_ref.dtype)
            else:
                out_ref[0, pl.ds(t0, c), cols] = y.astype(out_ref.dtype)
        return carry

    lax.fori_loop(0, n_chunks, chunk_body, 0, unroll=True)


def _retention(dec_rows, rq, rkt, rv, reverse, fwd=None, rg=None, gn_w=None):
    b, s, _ = rq.shape
    tb = _tile(s, 2048)
    nb = s // tb
    blk = (lambda i: nb - 1 - i) if reverse else (lambda i: i)
    tok = pl.BlockSpec((1, tb, RET_WIDTH), lambda bi, i: (bi, blk(i), 0))
    feat = pl.BlockSpec((1, tb // RET_CHUNK, RET_WIDTH, RET_CHUNK), lambda bi, i: (bi, blk(i), 0, 0))
    dec_spec = pl.BlockSpec((RET_HEADS, V7X_LANES), lambda bi, i: (0, 0))
    in_specs = [dec_spec, tok, feat, tok]
    args = [dec_rows, rq, rkt, rv]
    if reverse:
        in_specs += [tok, tok, pl.BlockSpec((1, RET_WIDTH), lambda bi, i: (0, 0))]
        args += [fwd, rg, gn_w]
    c = RET_CHUNK
    scratch = [
        pltpu.VMEM((RET_HEADS, RET_HEAD_DIM, RET_HEAD_DIM), F32),
        pltpu.VMEM((RET_HEADS, c, c), F32),
        pltpu.VMEM((RET_HEADS, c, V7X_LANES), F32),
        pltpu.VMEM((RET_HEADS, RET_HEAD_DIM, c), F32),
        pltpu.VMEM((RET_HEADS, V7X_SUBLANES, V7X_LANES), F32),
    ]
    return pl.pallas_call(
        functools.partial(_ret_kernel, reverse),
        grid=(b, nb),
        in_specs=in_specs,
        out_specs=tok,
        out_shape=jax.ShapeDtypeStruct((b, s, RET_WIDTH), BF16),
        scratch_shapes=scratch,
        compiler_params=_params("arbitrary", "arbitrary"),
        name="ret_bwd" if reverse else "ret_fwd",
    )(*args)


NEG_BIG = -1e30


ATTN_GROUP = 2 * V7X_LANES
ATTN_LAG = 3
ATTN_KV_ITEM = 2048
ATTN_KV_STEP = 2048
ATTN_Q_TILE = 4096


def _attn_kernel(q_ref, k_ref, vprev_ref, vcur_ref, o_ref, s_ref, m_old_ref, m_cur_ref, acc_ref):
    t = pl.program_id(3)
    nk = pl.num_programs(3) - 1
    gw = ATTN_GROUP
    ng = q_ref.shape[2] // gw
    rows = s_ref.shape[0]
    n_sub = k_ref.shape[2] // rows
    n_items = n_sub * ng
    lag = min(ATTN_LAG, ng - 1)
    first_carried = ng - lag
    group = lambda g: slice(g * gw, (g + 1) * gw)
    kv = lambda a: slice(a * rows, (a + 1) * rows)

    def softmax_pv(g, vt, m_prev, m_new):
        cols = group(g)
        alpha = jnp.exp2(m_prev - m_new)
        p = jnp.exp2(s_ref[:, cols] - m_new)
        acc_ref[:, cols] = alpha * acc_ref[:, cols] + _dot(vt, p.astype(BF16))

    def scores(a, g, m_prev):
        cols = group(g)
        s = _nt_dot(k_ref[0, 0, kv(a), :], q_ref[0, 0, cols, :])
        s_ref[:, cols] = s
        return jnp.maximum(m_prev, jnp.max(s, axis=0, keepdims=True))

    def carried_pv(g):
        cols = group(g)
        softmax_pv(g, vprev_ref[0, 0], m_old_ref[:, cols], m_cur_ref[:, cols])

    def step(has_prev):
        m_before, m_after = {}, {}
        for p in range(n_items):
            a, g = divmod(p, ng)
            cols = group(g)
            m_before[p] = m_cur_ref[:, cols] if a == 0 else m_after[p - ng]
            m_after[p] = scores(a, g, m_before[p])
            if p >= n_items - lag:
                m_old_ref[:, cols] = m_before[p]
                m_cur_ref[:, cols] = m_after[p]
            d = p - lag
            if d >= 0:
                da, dg = divmod(d, ng)
                softmax_pv(dg, vcur_ref[0, 0, :, kv(da)], m_before[d], m_after[d])
                if da == n_sub - 1:
                    m_cur_ref[:, group(dg)] = m_after[d]
            elif has_prev:
                carried_pv(ng + d)

    @pl.when(t == 0)
    def _():
        m_cur_ref[...] = jnp.full_like(m_cur_ref, NEG_BIG)
        acc_ref[...] = jnp.zeros_like(acc_ref)
        step(False)

    @pl.when(jnp.logical_and(t > 0, t < nk))
    def _():
        step(True)

    @pl.when(t == nk)
    def _():
        for g in range(first_carried, ng):
            carried_pv(g)
        o = acc_ref[0:MLA_V_DIM, :] / acc_ref[MLA_V_DIM:MLA_V_DIM + 1, :]
        o_ref[0] = o.T.astype(o_ref.dtype)


def _attention(q, k, vt):
    b, nh, s, _ = q.shape
    tq = _tile(s, ATTN_Q_TILE)
    tk = _tile(s, ATTN_KV_STEP)
    rows = _tile(tk, ATTN_KV_ITEM)
    n_sub = tk // rows
    nk = s // tk
    return pl.pallas_call(
        _attn_kernel,
        grid=(b, nh, s // tq, nk + 1),
        in_specs=[
            pl.BlockSpec((1, 1, tq, QK_PAD), lambda bi, hi, qi, t: (bi, hi, qi, 0)),
            pl.BlockSpec((1, 1, tk, QK_PAD), lambda bi, hi, qi, t: (bi, hi, jnp.minimum(t, nk - 1), 0)),
            pl.BlockSpec((1, 1, VT_ROWS, rows),
                         lambda bi, hi, qi, t: (bi, hi, 0, jnp.maximum(t * n_sub - 1, 0))),
            pl.BlockSpec((1, 1, VT_ROWS, tk), lambda bi, hi, qi, t: (bi, hi, 0, jnp.minimum(t, nk - 1))),
        ],
        out_specs=pl.BlockSpec((1, tq, MLA_V_DIM), lambda bi, hi, qi, t: (bi, qi, hi)),
        out_shape=jax.ShapeDtypeStruct((b, s, MLA_WIDTH), BF16),
        scratch_shapes=[
            pltpu.VMEM((rows, tq), F32),
            pltpu.VMEM((1, tq), F32),
            pltpu.VMEM((1, tq), F32),
            pltpu.VMEM((VT_ROWS, tq), F32),
        ],
        compiler_params=_params("arbitrary", "arbitrary", "arbitrary", "arbitrary"),
        name="attn",
    )(q, k, vt, vt)


FF_CHUNK = 1024
MLP_TILE = 1024
MLP_SUB = 256


def _mlp_kernel(alpha, x_ref, ret_ref, att_ref, mod_ref, w_o_ref, ln1_w_ref, ln1_b_ref,
                w_up_ref, w_down_ref, ln2_w_ref, ln2_b_ref, o_ref):
    g1 = mod_ref[0, 2:3, :]
    sh2 = mod_ref[0, 3:4, :]
    sc2 = mod_ref[0, 4:5, :]
    g2 = mod_ref[0, 5:6, :]
    n_sub = max(x_ref.shape[1] // MLP_SUB, 1)
    sub = x_ref.shape[1] // n_sub
    rows = [slice(i * sub, (i + 1) * sub) for i in range(n_sub)]
    mixes = [_dot(ret_ref[0, r, :], w_o_ref[0:RET_WIDTH, :]) + _dot(att_ref[0, r, :], w_o_ref[RET_WIDTH:, :])
             for r in rows]
    x1s, hs = [], []
    for r, mix in zip(rows, mixes):
        x1 = _layer_norm(alpha * x_ref[0, r, :] + g1 * mix) * ln1_w_ref[...] + ln1_b_ref[...]
        x1s.append(x1)
        hs.append((_layer_norm(x1) * (1.0 + sc2) + sh2).astype(BF16))
    for r, x1, h in zip(rows, x1s, hs):
        ff = jnp.zeros_like(x1)
        for c0 in range(0, D_FF, FF_CHUNK):
            u = jnp.maximum(_dot(h, w_up_ref[:, c0:c0 + FF_CHUNK]), 0.0)
            ff = ff + _dot((u * u).astype(BF16), w_down_ref[c0:c0 + FF_CHUNK, :])
        o_ref[0, r, :] = _layer_norm(alpha * x1 + g2 * ff) * ln2_w_ref[...] + ln2_b_ref[...]


def _mlp(alpha, x, ret, att, mod, mod_row0, w):
    b, s, _ = x.shape
    tm = _tile(s, MLP_TILE)
    xs = pl.BlockSpec((1, tm, D_MODEL), lambda bi, i: (bi, i, 0))
    half = pl.BlockSpec((1, tm, RET_WIDTH), lambda bi, i: (bi, i, 0))
    vec = lambda: pl.BlockSpec((1, D_MODEL), lambda bi, i: (0, 0))
    return pl.pallas_call(
        functools.partial(_mlp_kernel, alpha),
        grid=(b, s // tm),
        in_specs=[
            xs, half, half,
            pl.BlockSpec((1, N_MOD, D_MODEL), lambda bi, i: (bi + mod_row0, 0, 0)),
            _const_spec(w["w_o"].shape), vec(), vec(),
            _const_spec(w["w_up"].shape), _const_spec(w["w_down"].shape), vec(), vec(),
        ],
        out_specs=xs,
        out_shape=jax.ShapeDtypeStruct((b, s, D_MODEL), F32),
        compiler_params=_params("arbitrary", "arbitrary"),
        name="mlp",
    )(x, ret, att, mod, w["w_o"], w["ln1_w"], w["ln1_b"], w["w_up"], w["w_down"], w["ln2_w"], w["ln2_b"])


def _prep_layer_weights(w_in, ret_decay_f, ret_decay_b, ret_gn_w, q_norm_w, w_uq, kv_norm_w, w_ukv,
                        w_o, ln1_w, ln1_b, w_up, w_down, ln2_w, ln2_b):
    rw = RET_WIDTH
    w_rq, w_rk, w_rv_rg = w_in[:, :rw], w_in[:, rw:2 * rw], w_in[:, 2 * rw:4 * rw]
    w_cq = w_in[:, 4 * rw:4 * rw + Q_LORA]
    w_ckv = w_in[:, 4 * rw + Q_LORA:4 * rw + Q_LORA + KV_LORA]
    w_kr = w_in[:, 4 * rw + Q_LORA + KV_LORA:]
    pad = jnp.zeros((D_MODEL, V7X_LANES - MLA_ROPE_DIM), w_in.dtype)
    w_in_p = jnp.concatenate([w_rq, w_rv_rg, w_cq, w_kr, pad, w_ckv], axis=1).astype(BF16)
    assert w_in_p.shape[1] == _W_IN_COLS
    uq = w_uq.reshape(Q_LORA, MLA_HEADS, MLA_QK_DIM)
    ukv = w_ukv.reshape(KV_LORA, MLA_HEADS, MLA_NOPE_DIM + MLA_V_DIM)
    lanes = lambda v: jnp.broadcast_to(v.reshape(RET_HEADS, 1), (RET_HEADS, V7X_LANES)).astype(F32)
    return {
        "w_in": w_in_p,
        "w_kt": w_rk.T.astype(BF16),
        "q_norm_w": q_norm_w.reshape(1, Q_LORA),
        "w_uqn": uq[:, :, :MLA_NOPE_DIM].reshape(Q_LORA, MLA_HEADS * MLA_NOPE_DIM).astype(BF16),
        "w_uqr": uq[:, :, MLA_NOPE_DIM:].reshape(Q_LORA, MLA_HEADS * MLA_ROPE_DIM).astype(BF16),
        "kv_norm_w": kv_norm_w.reshape(1, KV_LORA),
        "w_uk": ukv[:, :, :MLA_NOPE_DIM].reshape(KV_LORA, MLA_HEADS * MLA_NOPE_DIM).astype(BF16),
        "w_uvt": ukv[:, :, MLA_NOPE_DIM:].reshape(KV_LORA, MLA_HEADS * MLA_V_DIM).T.astype(BF16),
        "dec_f": lanes(ret_decay_f),
        "dec_b": lanes(ret_decay_b),
        "gn_w": ret_gn_w.reshape(1, RET_WIDTH),
        "w_o": w_o.astype(BF16),
        "ln1_w": ln1_w.reshape(1, D_MODEL), "ln1_b": ln1_b.reshape(1, D_MODEL),
        "w_up": w_up.astype(BF16), "w_down": w_down.astype(BF16),
        "ln2_w": ln2_w.reshape(1, D_MODEL), "ln2_b": ln2_b.reshape(1, D_MODEL),
    }


def _layer(alpha, x, mod, mod_row0, tables, w):
    rq, rkt, rv, rg, q, k, vt = _proj(x, mod, mod_row0, tables, w)
    fwd = _retention(w["dec_f"], rq, rkt, rv, reverse=False)
    ret = _retention(w["dec_b"], rq, rkt, rv, reverse=True, fwd=fwd, rg=rg, gn_w=w["gn_w"])
    att = _attention(q, k, vt)
    return _mlp(alpha, x, ret, att, mod, mod_row0, w)


def _trunks(xs, cs, w_ada, b_ada, layer_weights):
    depth = w_ada.shape[0]
    alpha = float((2 * depth) ** 0.25)
    n_rows = sum(c.shape[0] for c in cs)
    assert n_rows <= ROWS_PAD
    c_pad = jnp.concatenate(list(cs) + [jnp.zeros((ROWS_PAD - n_rows, D_MODEL), F32)], axis=0)
    tables = _rotary_tables(max(x.shape[1] for x in xs))
    outs = list(xs)
    for l in range(depth):
        mod = _ada(c_pad, w_ada[l], b_ada[l]).reshape(ROWS_PAD, N_MOD, D_MODEL)
        w = _prep_layer_weights(*[t[l] for t in layer_weights])
        row0 = 0
        for i, c in enumerate(cs):
            outs[i] = _layer(alpha, outs[i], mod, row0, tables, w)
            row0 += c.shape[0]
    return tuple(outs)


def kernel(x_prompt, x_sample, c_prompt, c_sample, w_ada, b_ada, w_in, ret_decay_f, ret_decay_b, ret_gn_w,
           q_norm_w, w_uq, kv_norm_w, w_ukv, w_o, ln1_w, ln1_b, w_up, w_down, ln2_w, ln2_b):
    layer_weights = (w_in, ret_decay_f, ret_decay_b, ret_gn_w, q_norm_w, w_uq, kv_norm_w, w_ukv,
                     w_o, ln1_w, ln1_b, w_up, w_down, ln2_w, ln2_b)
    return _trunks((x_prompt, x_sample), (c_prompt, c_sample), w_ada, b_ada, layer_weights)
```

```python
import functools
import math

import jax
import jax.numpy as jnp
from jax import lax
from jax.experimental import pallas as pl
from jax.experimental.pallas import tpu as pltpu

F32 = jnp.float32
BF16 = jnp.bfloat16

D_MODEL = 1024
RET_HEADS = 4
RET_HEAD_DIM = 128
RET_WIDTH = RET_HEADS * RET_HEAD_DIM
MLA_HEADS = 4
MLA_NOPE_DIM = 128
MLA_ROPE_DIM = 64
MLA_V_DIM = 128
MLA_WIDTH = MLA_HEADS * MLA_V_DIM
MLA_QK_DIM = MLA_NOPE_DIM + MLA_ROPE_DIM
Q_LORA = 384
KV_LORA = 256
D_FF = 4 * D_MODEL
ROPE_BASE = 10000.0
LN_EPS = 1e-5
RMS_EPS = 1e-6
N_MOD = 6
MLA_SCALE = float(MLA_QK_DIM ** -0.5)
LOG2E = 1.4426950408889634

V7X_LANES = 128
V7X_SUBLANES = 8
V7X_VMEM_BYTES = 64 * 1024 * 1024
VMEM_LIMIT = V7X_VMEM_BYTES - 8 * 1024 * 1024

QK_PAD = 2 * V7X_LANES
VT_ROWS = MLA_V_DIM + V7X_SUBLANES
_OFF_RQ, _OFF_RV, _OFF_RG = 0, RET_WIDTH, 2 * RET_WIDTH
_OFF_CQ = 3 * RET_WIDTH
_OFF_KR = _OFF_CQ + Q_LORA
_OFF_CKV = _OFF_KR + V7X_LANES
_W_IN_COLS = _OFF_CKV + KV_LORA
ROWS_PAD = V7X_SUBLANES
RET_CHUNK = 128


def _tile(n, pref):
    t = min(n, pref)
    assert n % t == 0, (n, t)
    return t


def _const_spec(shape):
    nd = len(shape)
    return pl.BlockSpec(shape, lambda *_: (0,) * nd, pipeline_mode=pl.Buffered(1))


def _params(*sem):
    return pltpu.CompilerParams(dimension_semantics=sem, vmem_limit_bytes=VMEM_LIMIT)


def _nt_dot(a, b):
    return lax.dot_general(a, b, (((1,), (1,)), ((), ())), preferred_element_type=F32)


def _dot(a, b):
    return jnp.dot(a, b, preferred_element_type=F32)


def _ada_kernel(c_ref, w_ref, b_ref, o_ref):
    c = c_ref[...]
    s = c / (1.0 + jnp.exp(-c))
    o_ref[...] = _dot(s.astype(BF16), w_ref[...].astype(BF16)) + b_ref[...]


def _ada(c_pad, w_ada, b_ada):
    n = w_ada.shape[1]
    tn = _tile(n, D_MODEL)
    return pl.pallas_call(
        _ada_kernel,
        grid=(n // tn,),
        in_specs=[
            pl.BlockSpec((ROWS_PAD, D_MODEL), lambda j: (0, 0)),
            pl.BlockSpec((D_MODEL, tn), lambda j: (0, j)),
            pl.BlockSpec((1, tn), lambda j: (0, j)),
        ],
        out_specs=pl.BlockSpec((ROWS_PAD, tn), lambda j: (0, j)),
        out_shape=jax.ShapeDtypeStruct((ROWS_PAD, n), F32),
        compiler_params=_params("arbitrary"),
        name="ada",
    )(c_pad, w_ada, b_ada.reshape(1, n))


def _tables_kernel(inv_ref, sgn_r_ref, sgn_m_ref,
                   cos_r_ref, sin_r_ref, cos_m_ref, sin_m_ref, cos_rt_ref, sin_rt_ref):
    tr = cos_r_ref.shape[0]
    seg = TABLE_SEG
    n_seg = tr // seg
    half = V7X_LANES // 2
    inv = inv_ref[...]
    ang_off = lax.broadcasted_iota(jnp.int32, (seg, V7X_LANES), 0).astype(F32) * inv
    c_off = jnp.cos(ang_off)
    s_off = jnp.sin(ang_off)
    seg_rows = -(-n_seg // V7X_SUBLANES) * V7X_SUBLANES
    seg_start = lax.broadcasted_iota(jnp.int32, (seg_rows, V7X_LANES), 0) * seg + pl.program_id(0) * tr
    ang_seg = seg_start.astype(F32) * inv
    c_seg = jnp.cos(ang_seg)
    s_seg = jnp.sin(ang_seg)
    low = lax.broadcasted_iota(jnp.int32, (seg, V7X_LANES), 1) < half
    for j in range(n_seg):
        rows = slice(j * seg, (j + 1) * seg)
        cj = c_seg[j:j + 1, :]
        sj = s_seg[j:j + 1, :]
        c = cj * c_off - sj * s_off
        s = sj * c_off + cj * s_off
        c_sw = pltpu.roll(c, half, 1)
        s_sw = pltpu.roll(s, half, 1)
        cos_r = jnp.where(low, c, c_sw)
        sin_r = jnp.where(low, s, s_sw) * sgn_r_ref[...]
        cos_r_ref[rows, :] = cos_r
        sin_r_ref[rows, :] = sin_r
        cos_m_ref[rows, :] = jnp.where(low, c_sw, c)
        sin_m_ref[rows, :] = jnp.where(low, s_sw, s) * sgn_m_ref[...]
        cos_rt_ref[:, rows] = cos_r.T
        sin_rt_ref[:, rows] = sin_r.T


TABLE_SEG = 128


def _rotary_tables(seq):
    half_r = RET_HEAD_DIM // 2
    half_m = MLA_ROPE_DIM // 2
    assert half_r + 2 * half_m == V7X_LANES and RET_HEAD_DIM == V7X_LANES
    inv_r = ROPE_BASE ** (-jnp.arange(0, RET_HEAD_DIM, 2, dtype=F32) / RET_HEAD_DIM)
    inv_m = ROPE_BASE ** (-jnp.arange(0, MLA_ROPE_DIM, 2, dtype=F32) / MLA_ROPE_DIM)
    inv = jnp.concatenate([inv_r, inv_m, inv_m])
    sgn_r = jnp.concatenate([-jnp.ones(half_r, F32), jnp.ones(half_r, F32)])
    sgn_m = jnp.tile(jnp.concatenate([-jnp.ones(half_m, F32), jnp.ones(half_m, F32)]),
                     V7X_LANES // MLA_ROPE_DIM)
    tr = _tile(seq, 1024)
    row = lambda v: v.reshape(1, V7X_LANES)
    tok = jax.ShapeDtypeStruct((seq, V7X_LANES), F32)
    feat = jax.ShapeDtypeStruct((RET_HEAD_DIM, seq), F32)
    vec = pl.BlockSpec((1, V7X_LANES), lambda i: (0, 0))
    tok_spec = pl.BlockSpec((tr, V7X_LANES), lambda i: (i, 0))
    feat_spec = pl.BlockSpec((RET_HEAD_DIM, tr), lambda i: (0, i))
    return pl.pallas_call(
        _tables_kernel,
        grid=(seq // tr,),
        in_specs=[vec, vec, vec],
        out_specs=[tok_spec, tok_spec, tok_spec, tok_spec, feat_spec, feat_spec],
        out_shape=[tok, tok, tok, tok, feat, feat],
        compiler_params=_params("arbitrary"),
        name="tables",
    )(row(inv), row(sgn_r), row(sgn_m))


def _layer_norm(x):
    mu = jnp.mean(x, axis=-1, keepdims=True)
    xc = x - mu
    var = jnp.mean(xc * xc, axis=-1, keepdims=True)
    return xc * lax.rsqrt(var + LN_EPS)


def _rms_norm(x, w):
    return x * lax.rsqrt(jnp.mean(x * x, axis=-1, keepdims=True) + RMS_EPS) * w


def _rope64(x, cos_m, sin_m, first_half):
    swapped = jnp.where(first_half,
                        pltpu.roll(x, V7X_LANES - MLA_ROPE_DIM // 2, 1),
                        pltpu.roll(x, MLA_ROPE_DIM // 2, 1))
    return x * cos_m + swapped * sin_m


def _proj_kernel(x_ref, mod_ref, cos_r_ref, sin_r_ref, cos_m_ref, sin_m_ref, cos_rt_ref, sin_rt_ref,
                 w_in_ref, w_kt_ref, qn_w_ref, w_uqn_ref, w_uqr_ref, kvn_w_ref, w_uk_ref, w_uvt_ref,
                 rq_ref, rkt_ref, rv_ref, rg_ref, q_ref, k_ref, vt_ref):
    sh1 = mod_ref[0, 0:1, :]
    sc1 = mod_ref[0, 1:2, :]
    hd = RET_HEAD_DIM
    k_scale = RET_HEAD_DIM ** -0.5
    q_scale = MLA_SCALE * LOG2E
    n_sub = max(x_ref.shape[1] // PROJ_SUB, 1)
    sub = x_ref.shape[1] // n_sub
    rows = [slice(i * sub, (i + 1) * sub) for i in range(n_sub)]
    lane = lax.broadcasted_iota(jnp.int32, (sub, V7X_LANES), 1)
    first_half = (lane % MLA_ROPE_DIM) < (MLA_ROPE_DIM // 2)
    low_lanes = lane < MLA_ROPE_DIM
    pad_row = lax.broadcasted_iota(jnp.int32, (VT_ROWS - MLA_V_DIM, sub), 0)
    ones_row = jnp.where(pad_row == 0, 1.0, 0.0).astype(BF16)

    hs = [(_layer_norm(x_ref[0, r, :]) * (1.0 + sc1) + sh1).astype(BF16) for r in rows]

    def retention_part(r, h):
        proj = lambda lo, n: _dot(h, w_in_ref[:, lo:lo + n])
        cos_r = cos_r_ref[r, :]
        sin_r = sin_r_ref[r, :]
        rq = proj(_OFF_RQ, RET_WIDTH)
        for hh in range(RET_HEADS):
            t = rq[:, hh * hd:(hh + 1) * hd]
            rq_ref[0, r, hh * hd:(hh + 1) * hd] = (t * cos_r + pltpu.roll(t, hd // 2, 1) * sin_r).astype(BF16)
        cos_rt = cos_rt_ref[:, r]
        sin_rt = sin_rt_ref[:, r]
        rkt = _nt_dot(w_kt_ref[...], h)
        chunk0 = r.start // RET_CHUNK
        for hh in range(RET_HEADS):
            t = rkt[hh * hd:(hh + 1) * hd, :]
            rot = ((t * cos_rt + pltpu.roll(t, hd // 2, 0) * sin_rt) * k_scale).astype(BF16)
            for j in range(sub // RET_CHUNK):
                rkt_ref[0, chunk0 + j, hh * hd:(hh + 1) * hd, :] = rot[:, j * RET_CHUNK:(j + 1) * RET_CHUNK]
        rv_ref[0, r, :] = proj(_OFF_RV, RET_WIDTH).astype(BF16)
        g = proj(_OFF_RG, RET_WIDTH)
        rg_ref[0, r, :] = (g / (1.0 + jnp.exp(-g))).astype(BF16)

    def mla_latents(r, h):
        proj = lambda lo, n: _dot(h, w_in_ref[:, lo:lo + n])
        cq_kr = proj(_OFF_CQ, Q_LORA + V7X_LANES)
        cq = _rms_norm(cq_kr[:, 0:Q_LORA], qn_w_ref[...]).astype(BF16)
        ckv = _rms_norm(proj(_OFF_CKV, KV_LORA), kvn_w_ref[...]).astype(BF16)
        k_pe = _rope64(cq_kr[:, Q_LORA:], cos_m_ref[r, :], sin_m_ref[r, :], first_half)
        return cq, ckv, k_pe

    def mla_heads(r, cq, ckv, k_pe):
        cos_m = cos_m_ref[r, :]
        sin_m = sin_m_ref[r, :]
        q_nope = _dot(cq, w_uqn_ref[...]) * q_scale
        q_rope = _dot(cq, w_uqr_ref[...]) * q_scale
        k_nope = _dot(ckv, w_uk_ref[...])
        v_t = _nt_dot(w_uvt_ref[...], ckv)
        for pair in range(MLA_HEADS // 2):
            rp = _rope64(q_rope[:, pair * V7X_LANES:(pair + 1) * V7X_LANES], cos_m, sin_m, first_half)
            for side in range(2):
                hh = 2 * pair + side
                keep = low_lanes if side == 0 else jnp.logical_not(low_lanes)
                q_ref[0, hh, r, 0:MLA_NOPE_DIM] = q_nope[:, hh * MLA_NOPE_DIM:(hh + 1) * MLA_NOPE_DIM].astype(BF16)
                q_ref[0, hh, r, MLA_NOPE_DIM:QK_PAD] = jnp.where(keep, rp, 0.0).astype(BF16)
        k_pe_hi = pltpu.roll(k_pe, MLA_ROPE_DIM, 1)
        for hh in range(MLA_HEADS):
            k_ref[0, hh, r, 0:MLA_NOPE_DIM] = k_nope[:, hh * MLA_NOPE_DIM:(hh + 1) * MLA_NOPE_DIM].astype(BF16)
            k_ref[0, hh, r, MLA_NOPE_DIM:QK_PAD] = (k_pe if hh % 2 == 0 else k_pe_hi).astype(BF16)
            vt_ref[0, hh, 0:MLA_V_DIM, r] = v_t[hh * MLA_V_DIM:(hh + 1) * MLA_V_DIM, :].astype(BF16)
            vt_ref[0, hh, MLA_V_DIM:VT_ROWS, r] = ones_row

    for r, h in zip(rows, hs):
        retention_part(r, h)
    latents = [mla_latents(r, h) for r, h in zip(rows, hs)]
    for r, lat in zip(rows, latents):
        mla_heads(r, *lat)


PROJ_TILE = 1024
PROJ_SUB = 256


def _proj(x, mod, mod_row0, tables, w):
    b, s, _ = x.shape
    tm = _tile(s, PROJ_TILE)
    cos_r, sin_r, cos_m, sin_m, cos_rt, sin_rt = tables
    tok_tab = pl.BlockSpec((tm, V7X_LANES), lambda bi, i: (i, 0))
    feat_tab = pl.BlockSpec((RET_HEAD_DIM, tm), lambda bi, i: (0, i))
    tok_out = lambda: pl.BlockSpec((1, tm, RET_WIDTH), lambda bi, i: (bi, i, 0))
    out_shape = [
        jax.ShapeDtypeStruct((b, s, RET_WIDTH), BF16),
        jax.ShapeDtypeStruct((b, s // RET_CHUNK, RET_WIDTH, RET_CHUNK), BF16),
        jax.ShapeDtypeStruct((b, s, RET_WIDTH), BF16),
        jax.ShapeDtypeStruct((b, s, RET_WIDTH), BF16),
        jax.ShapeDtypeStruct((b, MLA_HEADS, s, QK_PAD), BF16),
        jax.ShapeDtypeStruct((b, MLA_HEADS, s, QK_PAD), BF16),
        jax.ShapeDtypeStruct((b, MLA_HEADS, VT_ROWS, s), BF16),
    ]
    out_specs = [
        tok_out(),
        pl.BlockSpec((1, tm // RET_CHUNK, RET_WIDTH, RET_CHUNK), lambda bi, i: (bi, i, 0, 0)),
        tok_out(),
        tok_out(),
        pl.BlockSpec((1, MLA_HEADS, tm, QK_PAD), lambda bi, i: (bi, 0, i, 0)),
        pl.BlockSpec((1, MLA_HEADS, tm, QK_PAD), lambda bi, i: (bi, 0, i, 0)),
        pl.BlockSpec((1, MLA_HEADS, VT_ROWS, tm), lambda bi, i: (bi, 0, 0, i)),
    ]
    in_specs = [
        pl.BlockSpec((1, tm, D_MODEL), lambda bi, i: (bi, i, 0)),
        pl.BlockSpec((1, N_MOD, D_MODEL), lambda bi, i: (bi + mod_row0, 0, 0)),
        tok_tab, tok_tab, tok_tab, tok_tab, feat_tab, feat_tab,
        _const_spec(w["w_in"].shape), _const_spec(w["w_kt"].shape),
        _const_spec(w["q_norm_w"].shape), _const_spec(w["w_uqn"].shape), _const_spec(w["w_uqr"].shape),
        _const_spec(w["kv_norm_w"].shape), _const_spec(w["w_uk"].shape), _const_spec(w["w_uvt"].shape),
    ]
    return pl.pallas_call(
        _proj_kernel,
        grid=(b, s // tm),
        in_specs=in_specs,
        out_specs=out_specs,
        out_shape=out_shape,
        compiler_params=_params("arbitrary", "arbitrary"),
        name="proj",
    )(x, mod, cos_r, sin_r, cos_m, sin_m, cos_rt, sin_rt,
      w["w_in"], w["w_kt"], w["q_norm_w"], w["w_uqn"], w["w_uqr"], w["kv_norm_w"], w["w_uk"], w["w_uvt"])


def _log_sigmoid(x):
    return jnp.minimum(x, 0.0) - jnp.log(1.0 + jnp.exp(-jnp.abs(x)))


def _ret_decay_tables(dec_row, reverse, d_in_ref, d_q_ref, d_k_ref, d_c_ref, hh):
    c = RET_CHUNK
    lg = _log_sigmoid(dec_row)[:, 0:1]
    row = lax.broadcasted_iota(jnp.int32, (c, c), 0)
    col = lax.broadcasted_iota(jnp.int32, (c, c), 1)
    rel = (col - row) if reverse else (row - col)
    mask = (rel > 0) if reverse else (rel >= 0)
    relf = jnp.maximum(rel, 0).astype(F32)
    d_in_ref[hh] = jnp.where(mask, jnp.exp(lg * relf), 0.0)
    qi = lax.broadcasted_iota(jnp.int32, (c, V7X_LANES), 0).astype(F32)
    kj = lax.broadcasted_iota(jnp.int32, (RET_HEAD_DIM, c), 1).astype(F32)
    q_steps = (c - qi) if reverse else (qi + 1.0)
    k_steps = kj if reverse else (c - 1.0 - kj)
    d_q_ref[hh] = jnp.exp(lg * q_steps)
    d_k_ref[hh] = jnp.exp(lg * k_steps)
    d_c_ref[hh] = jnp.exp(lg * float(c)) + jnp.zeros((V7X_SUBLANES, V7X_LANES), F32)


def _ret_kernel(reverse, *refs):
    if reverse:
        (dec_ref, rq_ref, rkt_ref, rv_ref, fwd_ref, rg_ref, gnw_ref, out_ref,
         state_ref, d_in_ref, d_q_ref, d_k_ref, d_c_ref) = refs
    else:
        (dec_ref, rq_ref, rkt_ref, rv_ref, out_ref,
         state_ref, d_in_ref, d_q_ref, d_k_ref, d_c_ref) = refs
    c = RET_CHUNK
    hd = RET_HEAD_DIM
    n_chunks = rq_ref.shape[1] // c

    @pl.when(pl.program_id(1) == 0)
    def _():
        state_ref[...] = jnp.zeros_like(state_ref)
        for hh in range(RET_HEADS):
            _ret_decay_tables(dec_ref[hh:hh + 1, :], reverse, d_in_ref, d_q_ref, d_k_ref, d_c_ref, hh)

    def chunk_body(ci, carry):
        cidx = (n_chunks - 1 - ci) if reverse else ci
        t0 = pl.multiple_of(cidx * c, c)
        for hh in range(RET_HEADS):
            cols = slice(hh * hd, (hh + 1) * hd)
            q = rq_ref[0, pl.ds(t0, c), cols]
            kt = rkt_ref[0, cidx, cols, :]
            v = rv_ref[0, pl.ds(t0, c), cols]
            state = state_ref[hh]
            s = _dot(q, kt) * d_in_ref[hh]
            inner = _dot(s.astype(BF16), v)
            cross = _dot(q, state.astype(BF16)) * d_q_ref[hh]
            kt_dec = (kt.astype(F32) * d_k_ref[hh]).astype(BF16)
            state_ref[hh] = state * d_c_ref[hh][0:1, :] + _dot(kt_dec, v)
            y = inner + cross
            if reverse:
                y = y + fwd_ref[0, pl.ds(t0, c), cols].astype(F32)
                mu = jnp.mean(y, axis=-1, keepdims=True)
                yc = y - mu
                var = jnp.mean(yc * yc, axis=-1, keepdims=True)
                yn = yc * lax.rsqrt(var + LN_EPS) * gnw_ref[:, cols]
                gate = rg_ref[0, pl.ds(t0, c), cols].astype(F32)
                out_ref[0, pl.ds(t0, c), cols] = (gate * yn).astype(out_ref.dtype)
            else:
                out_ref[0, pl.ds(t0, c), cols] = y.astype(out_ref.dtype)
        return carry

    lax.fori_loop(0, n_chunks, chunk_body, 0, unroll=True)


def _retention(dec_rows, rq, rkt, rv, reverse, fwd=None, rg=None, gn_w=None):
    b, s, _ = rq.shape
    tb = _tile(s, 2048)
    nb = s // tb
    blk = (lambda i: nb - 1 - i) if reverse else (lambda i: i)
    tok = pl.BlockSpec((1, tb, RET_WIDTH), lambda bi, i: (bi, blk(i), 0))
    feat = pl.BlockSpec((1, tb // RET_CHUNK, RET_WIDTH, RET_CHUNK), lambda bi, i: (bi, blk(i), 0, 0))
    dec_spec = pl.BlockSpec((RET_HEADS, V7X_LANES), lambda bi, i: (0, 0))
    in_specs = [dec_spec, tok, feat, tok]
    args = [dec_rows, rq, rkt, rv]
    if reverse:
        in_specs += [tok, tok, pl.BlockSpec((1, RET_WIDTH), lambda bi, i: (0, 0))]
        args += [fwd, rg, gn_w]
    c = RET_CHUNK
    scratch = [
        pltpu.VMEM((RET_HEADS, RET_HEAD_DIM, RET_HEAD_DIM), F32),
        pltpu.VMEM((RET_HEADS, c, c), F32),
        pltpu.VMEM((RET_HEADS, c, V7X_LANES), F32),
        pltpu.VMEM((RET_HEADS, RET_HEAD_DIM, c), F32),
        pltpu.VMEM((RET_HEADS, V7X_SUBLANES, V7X_LANES), F32),
    ]
    return pl.pallas_call(
        functools.partial(_ret_kernel, reverse),
        grid=(b, nb),
        in_specs=in_specs,
        out_specs=tok,
        out_shape=jax.ShapeDtypeStruct((b, s, RET_WIDTH), BF16),
        scratch_shapes=scratch,
        compiler_params=_params("arbitrary", "arbitrary"),
        name="ret_bwd" if reverse else "ret_fwd",
    )(*args)


NEG_BIG = -1e30


ATTN_GROUP = 2 * V7X_LANES
ATTN_LAG = 3
ATTN_KV_ITEM = 2048
ATTN_KV_STEP = 2048
ATTN_Q_TILE = 4096


def _attn_kernel(q_ref, k_ref, vprev_ref, vcur_ref, o_ref, s_ref, m_old_ref, m_cur_ref, acc_ref):
    t = pl.program_id(3)
    nk = pl.num_programs(3) - 1
    gw = ATTN_GROUP
    ng = q_ref.shape[2] // gw
    rows = s_ref.shape[0]
    n_sub = k_ref.shape[2] // rows
    n_items = n_sub * ng
    lag = min(ATTN_LAG, ng - 1)
    first_carried = ng - lag
    group = lambda g: slice(g * gw, (g + 1) * gw)
    kv = lambda a: slice(a * rows, (a + 1) * rows)

    def softmax_pv(g, vt, m_prev, m_new):
        cols = group(g)
        alpha = jnp.exp2(m_prev - m_new)
        p = jnp.exp2(s_ref[:, cols] - m_new)
        acc_ref[:, cols] = alpha * acc_ref[:, cols] + _dot(vt, p.astype(BF16))

    def scores(a, g, m_prev):
        cols = group(g)
        s = _nt_dot(k_ref[0, 0, kv(a), :], q_ref[0, 0, cols, :])
        s_ref[:, cols] = s
        return jnp.maximum(m_prev, jnp.max(s, axis=0, keepdims=True))

    def carried_pv(g):
        cols = group(g)
        softmax_pv(g, vprev_ref[0, 0], m_old_ref[:, cols], m_cur_ref[:, cols])

    def step(has_prev):
        m_before, m_after = {}, {}
        for p in range(n_items):
            a, g = divmod(p, ng)
            cols = group(g)
            m_before[p] = m_cur_ref[:, cols] if a == 0 else m_after[p - ng]
            m_after[p] = scores(a, g, m_before[p])
            if p >= n_items - lag:
                m_old_ref[:, cols] = m_before[p]
                m_cur_ref[:, cols] = m_after[p]
            d = p - lag
            if d >= 0:
                da, dg = divmod(d, ng)
                softmax_pv(dg, vcur_ref[0, 0, :, kv(da)], m_before[d], m_after[d])
                if da == n_sub - 1:
                    m_cur_ref[:, group(dg)] = m_after[d]
            elif has_prev:
                carried_pv(ng + d)

    @pl.when(t == 0)
    def _():
        m_cur_ref[...] = jnp.full_like(m_cur_ref, NEG_BIG)
        acc_ref[...] = jnp.zeros_like(acc_ref)
        step(False)

    @pl.when(jnp.logical_and(t > 0, t < nk))
    def _():
        step(True)

    @pl.when(t == nk)
    def _():
        for g in range(first_carried, ng):
            carried_pv(g)
        o = acc_ref[0:MLA_V_DIM, :] / acc_ref[MLA_V_DIM:MLA_V_DIM + 1, :]
        o_ref[0] = o.T.astype(o_ref.dtype)


def _attention(q, k, vt):
    b, nh, s, _ = q.shape
    tq = _tile(s, ATTN_Q_TILE)
    tk = _tile(s, ATTN_KV_STEP)
    rows = _tile(tk, ATTN_KV_ITEM)
    n_sub = tk // rows
    nk = s // tk
    return pl.pallas_call(
        _attn_kernel,
        grid=(b, nh, s // tq, nk + 1),
        in_specs=[
            pl.BlockSpec((1, 1, tq, QK_PAD), lambda bi, hi, qi, t: (bi, hi, qi, 0)),
            pl.BlockSpec((1, 1, tk, QK_PAD), lambda bi, hi, qi, t: (bi, hi, jnp.minimum(t, nk - 1), 0)),
            pl.BlockSpec((1, 1, VT_ROWS, rows),
                         lambda bi, hi, qi, t: (bi, hi, 0, jnp.maximum(t * n_sub - 1, 0))),
            pl.BlockSpec((1, 1, VT_ROWS, tk), lambda bi, hi, qi, t: (bi, hi, 0, jnp.minimum(t, nk - 1))),
        ],
        out_specs=pl.BlockSpec((1, tq, MLA_V_DIM), lambda bi, hi, qi, t: (bi, qi, hi)),
        out_shape=jax.ShapeDtypeStruct((b, s, MLA_WIDTH), BF16),
        scratch_shapes=[
            pltpu.VMEM((rows, tq), F32),
            pltpu.VMEM((1, tq), F32),
            pltpu.VMEM((1, tq), F32),
            pltpu.VMEM((VT_ROWS, tq), F32),
        ],
        compiler_params=_params("arbitrary", "arbitrary", "arbitrary", "arbitrary"),
        name="attn",
    )(q, k, vt, vt)


FF_CHUNK = 1024
MLP_TILE = 1024
MLP_SUB = 256


def _mlp_kernel(alpha, x_ref, ret_ref, att_ref, mod_ref, w_o_ref, ln1_w_ref, ln1_b_ref,
                w_up_ref, w_down_ref, ln2_w_ref, ln2_b_ref, o_ref):
    g1 = mod_ref[0, 2:3, :]
    sh2 = mod_ref[0, 3:4, :]
    sc2 = mod_ref[0, 4:5, :]
    g2 = mod_ref[0, 5:6, :]
    n_sub = max(x_ref.shape[1] // MLP_SUB, 1)
    sub = x_ref.shape[1] // n_sub
    rows = [slice(i * sub, (i + 1) * sub) for i in range(n_sub)]
    mixes = [_dot(ret_ref[0, r, :], w_o_ref[0:RET_WIDTH, :]) + _dot(att_ref[0, r, :], w_o_ref[RET_WIDTH:, :])
             for r in rows]
    x1s, hs = [], []
    for r, mix in zip(rows, mixes):
        x1 = _layer_norm(alpha * x_ref[0, r, :] + g1 * mix) * ln1_w_ref[...] + ln1_b_ref[...]
        x1s.append(x1)
        hs.append((_layer_norm(x1) * (1.0 + sc2) + sh2).astype(BF16))
    for r, x1, h in zip(rows, x1s, hs):
        ff = jnp.zeros_like(x1)
        for c0 in range(0, D_FF, FF_CHUNK):
            u = jnp.maximum(_dot(h, w_up_ref[:, c0:c0 + FF_CHUNK]), 0.0)
            ff = ff + _dot((u * u).astype(BF16), w_down_ref[c0:c0 + FF_CHUNK, :])
        o_ref[0, r, :] = _layer_norm(alpha * x1 + g2 * ff) * ln2_w_ref[...] + ln2_b_ref[...]


def _mlp(alpha, x, ret, att, mod, mod_row0, w):
    b, s, _ = x.shape
    tm = _tile(s, MLP_TILE)
    xs = pl.BlockSpec((1, tm, D_MODEL), lambda bi, i: (bi, i, 0))
    half = pl.BlockSpec((1, tm, RET_WIDTH), lambda bi, i: (bi, i, 0))
    vec = lambda: pl.BlockSpec((1, D_MODEL), lambda bi, i: (0, 0))
    return pl.pallas_call(
        functools.partial(_mlp_kernel, alpha),
        grid=(b, s // tm),
        in_specs=[
            xs, half, half,
            pl.BlockSpec((1, N_MOD, D_MODEL), lambda bi, i: (bi + mod_row0, 0, 0)),
            _const_spec(w["w_o"].shape), vec(), vec(),
            _const_spec(w["w_up"].shape), _const_spec(w["w_down"].shape), vec(), vec(),
        ],
        out_specs=xs,
        out_shape=jax.ShapeDtypeStruct((b, s, D_MODEL), F32),
        compiler_params=_params("arbitrary", "arbitrary"),
        name="mlp",
    )(x, ret, att, mod, w["w_o"], w["ln1_w"], w["ln1_b"], w["w_up"], w["w_down"], w["ln2_w"], w["ln2_b"])


def _prep_layer_weights(w_in, ret_decay_f, ret_decay_b, ret_gn_w, q_norm_w, w_uq, kv_norm_w, w_ukv,
                        w_o, ln1_w, ln1_b, w_up, w_down, ln2_w, ln2_b):
    rw = RET_WIDTH
    w_rq, w_rk, w_rv_rg = w_in[:, :rw], w_in[:, rw:2 * rw], w_in[:, 2 * rw:4 * rw]
    w_cq = w_in[:, 4 * rw:4 * rw + Q_LORA]
    w_ckv = w_in[:, 4 * rw + Q_LORA:4 * rw + Q_LORA + KV_LORA]
    w_kr = w_in[:, 4 * rw + Q_LORA + KV_LORA:]
    pad = jnp.zeros((D_MODEL, V7X_LANES - MLA_ROPE_DIM), w_in.dtype)
    w_in_p = jnp.concatenate([w_rq, w_rv_rg, w_cq, w_kr, pad, w_ckv], axis=1).astype(BF16)
    assert w_in_p.shape[1] == _W_IN_COLS
    uq = w_uq.reshape(Q_LORA, MLA_HEADS, MLA_QK_DIM)
    ukv = w_ukv.reshape(KV_LORA, MLA_HEADS, MLA_NOPE_DIM + MLA_V_DIM)
    lanes = lambda v: jnp.broadcast_to(v.reshape(RET_HEADS, 1), (RET_HEADS, V7X_LANES)).astype(F32)
    return {
        "w_in": w_in_p,
        "w_kt": w_rk.T.astype(BF16),
        "q_norm_w": q_norm_w.reshape(1, Q_LORA),
        "w_uqn": uq[:, :, :MLA_NOPE_DIM].reshape(Q_LORA, MLA_HEADS * MLA_NOPE_DIM).astype(BF16),
        "w_uqr": uq[:, :, MLA_NOPE_DIM:].reshape(Q_LORA, MLA_HEADS * MLA_ROPE_DIM).astype(BF16),
        "kv_norm_w": kv_norm_w.reshape(1, KV_LORA),
        "w_uk": ukv[:, :, :MLA_NOPE_DIM].reshape(KV_LORA, MLA_HEADS * MLA_NOPE_DIM).astype(BF16),
        "w_uvt": ukv[:, :, MLA_NOPE_DIM:].reshape(KV_LORA, MLA_HEADS * MLA_V_DIM).T.astype(BF16),
        "dec_f": lanes(ret_decay_f),
        "dec_b": lanes(ret_decay_b),
        "gn_w": ret_gn_w.reshape(1, RET_WIDTH),
        "w_o": w_o.astype(BF16),
        "ln1_w": ln1_w.reshape(1, D_MODEL), "ln1_b": ln1_b.reshape(1, D_MODEL),
        "w_up": w_up.astype(BF16), "w_down": w_down.astype(BF16),
        "ln2_w": ln2_w.reshape(1, D_MODEL), "ln2_b": ln2_b.reshape(1, D_MODEL),
    }


def _layer(alpha, x, mod, mod_row0, tables, w):
    rq, rkt, rv, rg, q, k, vt = _proj(x, mod, mod_row0, tables, w)
    fwd = _retention(w["dec_f"], rq, rkt, rv, reverse=False)
    ret = _retention(w["dec_b"], rq, rkt, rv, reverse=True, fwd=fwd, rg=rg, gn_w=w["gn_w"])
    att = _attention(q, k, vt)
    return _mlp(alpha, x, ret, att, mod, mod_row0, w)


def _trunks(xs, cs, w_ada, b_ada, layer_weights):
    depth = w_ada.shape[0]
    alpha = float((2 * depth) ** 0.25)
    n_rows = sum(c.shape[0] for c in cs)
    assert n_rows <= ROWS_PAD
    c_pad = jnp.concatenate(list(cs) + [jnp.zeros((ROWS_PAD - n_rows, D_MODEL), F32)], axis=0)
    tables = _rotary_tables(max(x.shape[1] for x in xs))
    outs = list(xs)
    for l in range(depth):
        mod = _ada(c_pad, w_ada[l], b_ada[l]).reshape(ROWS_PAD, N_MOD, D_MODEL)
        w = _prep_layer_weights(*[t[l] for t in layer_weights])
        row0 = 0
        for i, c in enumerate(cs):
            outs[i] = _layer(alpha, outs[i], mod, row0, tables, w)
            row0 += c.shape[0]
    return tuple(outs)


def kernel(x_prompt, x_sample, c_prompt, c_sample, w_ada, b_ada, w_in, ret_decay_f, ret_decay_b, ret_gn_w,
           q_norm_w, w_uq, kv_norm_w, w_ukv, w_o, ln1_w, ln1_b, w_up, w_down, ln2_w, ln2_b):
    layer_weights = (w_in, ret_decay_f, ret_decay_b, ret_gn_w, q_norm_w, w_uq, kv_norm_w, w_ukv,
                     w_o, ln1_w, ln1_b, w_up, w_down, ln2_w, ln2_b)
    return _trunks((x_prompt, x_sample), (c_prompt, c_sample), w_ada, b_ada, layer_weights)
```

```python
import functools

import jax
import jax.numpy as jnp
from jax import lax
from jax.experimental import pallas as pl
from jax.experimental.pallas import tpu as pltpu

F32 = jnp.float32
BF16 = jnp.bfloat16

D_MODEL = 1024
RET_HEADS = 4
RET_HEAD_DIM = 128
RET_WIDTH = RET_HEADS * RET_HEAD_DIM
MLA_HEADS = 4
MLA_NOPE_DIM = 128
MLA_ROPE_DIM = 64
MLA_V_DIM = 128
MLA_WIDTH = MLA_HEADS * MLA_V_DIM
MLA_QK_DIM = MLA_NOPE_DIM + MLA_ROPE_DIM
Q_LORA = 384
KV_LORA = 256
D_FF = 4 * D_MODEL
ROPE_BASE = 10000.0
LN_EPS = 1e-5
RMS_EPS = 1e-6
N_MOD = 6
MLA_SCALE = float(MLA_QK_DIM ** -0.5)
LOG2E = 1.4426950408889634

V7X_LANES = 128
V7X_SUBLANES = 8
V7X_VMEM_BYTES = 64 * 1024 * 1024
VMEM_LIMIT = V7X_VMEM_BYTES - 8 * 1024 * 1024

QK_PAD = 2 * V7X_LANES
VT_ROWS = MLA_V_DIM + V7X_SUBLANES
_OFF_RQ, _OFF_RV, _OFF_RG = 0, RET_WIDTH, 2 * RET_WIDTH
_OFF_CQ = 3 * RET_WIDTH
_OFF_KR = _OFF_CQ + Q_LORA
_OFF_CKV = _OFF_KR + V7X_LANES
_W_IN_COLS = _OFF_CKV + KV_LORA
ROWS_PAD = V7X_SUBLANES
RET_CHUNK = 128


def _tile(n, pref):
    t = min(n, pref)
    assert n % t == 0, (n, t)
    return t


def _const_spec(shape):
    nd = len(shape)
    return pl.BlockSpec(shape, lambda *_: (0,) * nd, pipeline_mode=pl.Buffered(1))


def _params(*sem):
    return pltpu.CompilerParams(dimension_semantics=sem, vmem_limit_bytes=VMEM_LIMIT)


def _nt_dot(a, b):
    return lax.dot_general(a, b, (((1,), (1,)), ((), ())), preferred_element_type=F32)


def _dot(a, b):
    return jnp.dot(a, b, preferred_element_type=F32)


def _ada_kernel(c_ref, w_ref, b_ref, o_ref):
    c = c_ref[...]
    s = c / (1.0 + jnp.exp(-c))
    o_ref[...] = _dot(s.astype(BF16), w_ref[...].astype(BF16)) + b_ref[...]


def _ada(c_pad, w_ada, b_ada):
    n = w_ada.shape[1]
    tn = _tile(n, D_MODEL)
    return pl.pallas_call(
        _ada_kernel,
        grid=(n // tn,),
        in_specs=[
            pl.BlockSpec((ROWS_PAD, D_MODEL), lambda j: (0, 0)),
            pl.BlockSpec((D_MODEL, tn), lambda j: (0, j)),
            pl.BlockSpec((1, tn), lambda j: (0, j)),
        ],
        out_specs=pl.BlockSpec((ROWS_PAD, tn), lambda j: (0, j)),
        out_shape=jax.ShapeDtypeStruct((ROWS_PAD, n), F32),
        compiler_params=_params("arbitrary"),
        name="ada",
    )(c_pad, w_ada, b_ada.reshape(1, n))


def _tables_kernel(inv_ref, sgn_r_ref, sgn_m_ref,
                   cos_r_ref, sin_r_ref, cos_m_ref, sin_m_ref, cos_rt_ref, sin_rt_ref):
    tr = cos_r_ref.shape[0]
    seg = TABLE_SEG
    n_seg = tr // seg
    half = V7X_LANES // 2
    inv = inv_ref[...]
    ang_off = lax.broadcasted_iota(jnp.int32, (seg, V7X_LANES), 0).astype(F32) * inv
    c_off = jnp.cos(ang_off)
    s_off = jnp.sin(ang_off)
    seg_rows = -(-n_seg // V7X_SUBLANES) * V7X_SUBLANES
    seg_start = lax.broadcasted_iota(jnp.int32, (seg_rows, V7X_LANES), 0) * seg + pl.program_id(0) * tr
    ang_seg = seg_start.astype(F32) * inv
    c_seg = jnp.cos(ang_seg)
    s_seg = jnp.sin(ang_seg)
    low = lax.broadcasted_iota(jnp.int32, (seg, V7X_LANES), 1) < half
    for j in range(n_seg):
        rows = slice(j * seg, (j + 1) * seg)
        cj = c_seg[j:j + 1, :]
        sj = s_seg[j:j + 1, :]
        c = cj * c_off - sj * s_off
        s = sj * c_off + cj * s_off
        c_sw = pltpu.roll(c, half, 1)
        s_sw = pltpu.roll(s, half, 1)
        cos_r = jnp.where(low, c, c_sw)
        sin_r = jnp.where(low, s, s_sw) * sgn_r_ref[...]
        cos_r_ref[rows, :] = cos_r
        sin_r_ref[rows, :] = sin_r
        cos_m_ref[rows, :] = jnp.where(low, c_sw, c)
        sin_m_ref[rows, :] = jnp.where(low, s_sw, s) * sgn_m_ref[...]
        cos_rt_ref[:, rows] = cos_r.T
        sin_rt_ref[:, rows] = sin_r.T


TABLE_SEG = 128


def _rotary_tables(seq):
    half_r = RET_HEAD_DIM // 2
    half_m = MLA_ROPE_DIM // 2
    assert half_r + 2 * half_m == V7X_LANES and RET_HEAD_DIM == V7X_LANES
    inv_r = ROPE_BASE ** (-jnp.arange(0, RET_HEAD_DIM, 2, dtype=F32) / RET_HEAD_DIM)
    inv_m = ROPE_BASE ** (-jnp.arange(0, MLA_ROPE_DIM, 2, dtype=F32) / MLA_ROPE_DIM)
    inv = jnp.concatenate([inv_r, inv_m, inv_m])
    sgn_r = jnp.concatenate([-jnp.ones(half_r, F32), jnp.ones(half_r, F32)])
    sgn_m = jnp.tile(jnp.concatenate([-jnp.ones(half_m, F32), jnp.ones(half_m, F32)]),
                     V7X_LANES // MLA_ROPE_DIM)
    tr = _tile(seq, 1024)
    row = lambda v: v.reshape(1, V7X_LANES)
    tok = jax.ShapeDtypeStruct((seq, V7X_LANES), F32)
    feat = jax.ShapeDtypeStruct((RET_HEAD_DIM, seq), F32)
    vec = pl.BlockSpec((1, V7X_LANES), lambda i: (0, 0))
    tok_spec = pl.BlockSpec((tr, V7X_LANES), lambda i: (i, 0))
    feat_spec = pl.BlockSpec((RET_HEAD_DIM, tr), lambda i: (0, i))
    return pl.pallas_call(
        _tables_kernel,
        grid=(seq // tr,),
        in_specs=[vec, vec, vec],
        out_specs=[tok_spec, tok_spec, tok_spec, tok_spec, feat_spec, feat_spec],
        out_shape=[tok, tok, tok, tok, feat, feat],
        compiler_params=_params("arbitrary"),
        name="tables",
    )(row(inv), row(sgn_r), row(sgn_m))


def _layer_norm(x):
    mu = jnp.mean(x, axis=-1, keepdims=True)
    xc = x - mu
    var = jnp.mean(xc * xc, axis=-1, keepdims=True)
    return xc * lax.rsqrt(var + LN_EPS)


def _rms_norm(x, w):
    return x * lax.rsqrt(jnp.mean(x * x, axis=-1, keepdims=True) + RMS_EPS) * w


def _rope64(x, cos_m, sin_m, first_half):
    swapped = jnp.where(first_half,
                        pltpu.roll(x, V7X_LANES - MLA_ROPE_DIM // 2, 1),
                        pltpu.roll(x, MLA_ROPE_DIM // 2, 1))
    return x * cos_m + swapped * sin_m


def _proj_kernel(x_ref, mod_ref, cos_r_ref, sin_r_ref, cos_m_ref, sin_m_ref, cos_rt_ref, sin_rt_ref,
                 w_in_ref, w_kt_ref, qn_w_ref, w_uqn_ref, w_uqr_ref, kvn_w_ref, w_uk_ref, w_uvt_ref,
                 rq_ref, rkt_ref, rv_ref, rg_ref, q_ref, k_ref, vt_ref):
    sh1 = mod_ref[0, 0:1, :]
    sc1 = mod_ref[0, 1:2, :]
    hd = RET_HEAD_DIM
    k_scale = RET_HEAD_DIM ** -0.5
    q_scale = MLA_SCALE * LOG2E
    n_sub = max(x_ref.shape[1] // PROJ_SUB, 1)
    sub = x_ref.shape[1] // n_sub
    rows = [slice(i * sub, (i + 1) * sub) for i in range(n_sub)]
    lane = lax.broadcasted_iota(jnp.int32, (sub, V7X_LANES), 1)
    first_half = (lane % MLA_ROPE_DIM) < (MLA_ROPE_DIM // 2)
    low_lanes = lane < MLA_ROPE_DIM
    pad_row = lax.broadcasted_iota(jnp.int32, (VT_ROWS - MLA_V_DIM, sub), 0)
    ones_row = jnp.where(pad_row == 0, 1.0, 0.0).astype(BF16)

    hs = [(_layer_norm(x_ref[0, r, :]) * (1.0 + sc1) + sh1).astype(BF16) for r in rows]

    def retention_part(r, h):
        proj = lambda lo, n: _dot(h, w_in_ref[:, lo:lo + n])
        cos_r = cos_r_ref[r, :]
        sin_r = sin_r_ref[r, :]
        rq = proj(_OFF_RQ, RET_WIDTH)
        for hh in range(RET_HEADS):
            t = rq[:, hh * hd:(hh + 1) * hd]
            rq_ref[0, r, hh * hd:(hh + 1) * hd] = (t * cos_r + pltpu.roll(t, hd // 2, 1) * sin_r).astype(BF16)
        cos_rt = cos_rt_ref[:, r]
        sin_rt = sin_rt_ref[:, r]
        rkt = _nt_dot(w_kt_ref[...], h)
        chunk0 = r.start // RET_CHUNK
        for hh in range(RET_HEADS):
            t = rkt[hh * hd:(hh + 1) * hd, :]
            rot = ((t * cos_rt + pltpu.roll(t, hd // 2, 0) * sin_rt) * k_scale).astype(BF16)
            for j in range(sub // RET_CHUNK):
                rkt_ref[0, chunk0 + j, hh * hd:(hh + 1) * hd, :] = rot[:, j * RET_CHUNK:(j + 1) * RET_CHUNK]
        rv_ref[0, r, :] = proj(_OFF_RV, RET_WIDTH).astype(BF16)
        g = proj(_OFF_RG, RET_WIDTH)
        rg_ref[0, r, :] = (g / (1.0 + jnp.exp(-g))).astype(BF16)

    def mla_latents(r, h):
        proj = lambda lo, n: _dot(h, w_in_ref[:, lo:lo + n])
        cq_kr = proj(_OFF_CQ, Q_LORA + V7X_LANES)
        cq = _rms_norm(cq_kr[:, 0:Q_LORA], qn_w_ref[...]).astype(BF16)
        ckv = _rms_norm(proj(_OFF_CKV, KV_LORA), kvn_w_ref[...]).astype(BF16)
        k_pe = _rope64(cq_kr[:, Q_LORA:], cos_m_ref[r, :], sin_m_ref[r, :], first_half)
        return cq, ckv, k_pe

    def mla_heads(r, cq, ckv, k_pe):
        cos_m = cos_m_ref[r, :]
        sin_m = sin_m_ref[r, :]
        q_nope = _dot(cq, w_uqn_ref[...]) * q_scale
        q_rope = _dot(cq, w_uqr_ref[...]) * q_scale
        k_nope = _dot(ckv, w_uk_ref[...])
        v_t = _nt_dot(w_uvt_ref[...], ckv)
        for pair in range(MLA_HEADS // 2):
            rp = _rope64(q_rope[:, pair * V7X_LANES:(pair + 1) * V7X_LANES], cos_m, sin_m, first_half)
            for side in range(2):
                hh = 2 * pair + side
                keep = low_lanes if side == 0 else jnp.logical_not(low_lanes)
                q_ref[0, hh, r, 0:MLA_NOPE_DIM] = q_nope[:, hh * MLA_NOPE_DIM:(hh + 1) * MLA_NOPE_DIM].astype(BF16)
                q_ref[0, hh, r, MLA_NOPE_DIM:QK_PAD] = jnp.where(keep, rp, 0.0).astype(BF16)
        k_pe_hi = pltpu.roll(k_pe, MLA_ROPE_DIM, 1)
        for hh in range(MLA_HEADS):
            k_ref[0, hh, r, 0:MLA_NOPE_DIM] = k_nope[:, hh * MLA_NOPE_DIM:(hh + 1) * MLA_NOPE_DIM].astype(BF16)
            k_ref[0, hh, r, MLA_NOPE_DIM:QK_PAD] = (k_pe if hh % 2 == 0 else k_pe_hi).astype(BF16)
            vt_ref[0, hh, 0:MLA_V_DIM, r] = v_t[hh * MLA_V_DIM:(hh + 1) * MLA_V_DIM, :].astype(BF16)
            vt_ref[0, hh, MLA_V_DIM:VT_ROWS, r] = ones_row

    for r, h in zip(rows, hs):
        retention_part(r, h)
    latents = [mla_latents(r, h) for r, h in zip(rows, hs)]
    for r, lat in zip(rows, latents):
        mla_heads(r, *lat)


PROJ_TILE = 1024
PROJ_SUB = 256


def _proj(x, mod, mod_row0, tables, w):
    b, s, _ = x.shape
    tm = _tile(s, PROJ_TILE)
    cos_r, sin_r, cos_m, sin_m, cos_rt, sin_rt = tables
    tok_tab = pl.BlockSpec((tm, V7X_LANES), lambda bi, i: (i, 0))
    feat_tab = pl.BlockSpec((RET_HEAD_DIM, tm), lambda bi, i: (0, i))
    tok_out = lambda: pl.BlockSpec((1, tm, RET_WIDTH), lambda bi, i: (bi, i, 0))
    out_shape = [
        jax.ShapeDtypeStruct((b, s, RET_WIDTH), BF16),
        jax.ShapeDtypeStruct((b, s // RET_CHUNK, RET_WIDTH, RET_CHUNK), BF16),
        jax.ShapeDtypeStruct((b, s, RET_WIDTH), BF16),
        jax.ShapeDtypeStruct((b, s, RET_WIDTH), BF16),
        jax.ShapeDtypeStruct((b, MLA_HEADS, s, QK_PAD), BF16),
        jax.ShapeDtypeStruct((b, MLA_HEADS, s, QK_PAD), BF16),
        jax.ShapeDtypeStruct((b, MLA_HEADS, VT_ROWS, s), BF16),
    ]
    out_specs = [
        tok_out(),
        pl.BlockSpec((1, tm // RET_CHUNK, RET_WIDTH, RET_CHUNK), lambda bi, i: (bi, i, 0, 0)),
        tok_out(),
        tok_out(),
        pl.BlockSpec((1, MLA_HEADS, tm, QK_PAD), lambda bi, i: (bi, 0, i, 0)),
        pl.BlockSpec((1, MLA_HEADS, tm, QK_PAD), lambda bi, i: (bi, 0, i, 0)),
        pl.BlockSpec((1, MLA_HEADS, VT_ROWS, tm), lambda bi, i: (bi, 0, 0, i)),
    ]
    in_specs = [
        pl.BlockSpec((1, tm, D_MODEL), lambda bi, i: (bi, i, 0)),
        pl.BlockSpec((1, N_MOD, D_MODEL), lambda bi, i: (bi + mod_row0, 0, 0)),
        tok_tab, tok_tab, tok_tab, tok_tab, feat_tab, feat_tab,
        _const_spec(w["w_in"].shape), _const_spec(w["w_kt"].shape),
        _const_spec(w["q_norm_w"].shape), _const_spec(w["w_uqn"].shape), _const_spec(w["w_uqr"].shape),
        _const_spec(w["kv_norm_w"].shape), _const_spec(w["w_uk"].shape), _const_spec(w["w_uvt"].shape),
    ]
    return pl.pallas_call(
        _proj_kernel,
        grid=(b, s // tm),
        in_specs=in_specs,
        out_specs=out_specs,
        out_shape=out_shape,
        compiler_params=_params("arbitrary", "arbitrary"),
        name="proj",
    )(x, mod, cos_r, sin_r, cos_m, sin_m, cos_rt, sin_rt,
      w["w_in"], w["w_kt"], w["q_norm_w"], w["w_uqn"], w["w_uqr"], w["kv_norm_w"], w["w_uk"], w["w_uvt"])


def _log_sigmoid(x):
    return jnp.minimum(x, 0.0) - jnp.log(1.0 + jnp.exp(-jnp.abs(x)))


def _ret_decay_tables(dec_row, reverse, d_in_ref, d_q_ref, d_k_ref, d_c_ref, hh):
    c = RET_CHUNK
    lg = _log_sigmoid(dec_row)[:, 0:1]
    row = lax.broadcasted_iota(jnp.int32, (c, c), 0)
    col = lax.broadcasted_iota(jnp.int32, (c, c), 1)
    rel = (col - row) if reverse else (row - col)
    mask = (rel > 0) if reverse else (rel >= 0)
    relf = jnp.maximum(rel, 0).astype(F32)
    d_in_ref[hh] = jnp.where(mask, jnp.exp(lg * relf), 0.0)
    qi = lax.broadcasted_iota(jnp.int32, (c, V7X_LANES), 0).astype(F32)
    kj = lax.broadcasted_iota(jnp.int32, (RET_HEAD_DIM, c), 1).astype(F32)
    q_steps = (c - qi) if reverse else (qi + 1.0)
    k_steps = kj if reverse else (c - 1.0 - kj)
    d_q_ref[hh] = jnp.exp(lg * q_steps)
    d_k_ref[hh] = jnp.exp(lg * k_steps)
    d_c_ref[hh] = jnp.exp(lg * float(c)) + jnp.zeros((V7X_SUBLANES, V7X_LANES), F32)


def _ret_kernel(reverse, *refs):
    if reverse:
        (dec_ref, rq_ref, rkt_ref, rv_ref, fwd_ref, rg_ref, gnw_ref, out_ref,
         state_ref, d_in_ref, d_q_ref, d_k_ref, d_c_ref) = refs
    else:
        (dec_ref, rq_ref, rkt_ref, rv_ref, out_ref,
         state_ref, d_in_ref, d_q_ref, d_k_ref, d_c_ref) = refs
    c = RET_CHUNK
    hd = RET_HEAD_DIM
    n_chunks = rq_ref.shape[1] // c

    @pl.when(pl.program_id(1) == 0)
    def _():
        state_ref[...] = jnp.zeros_like(state_ref)
        for hh in range(RET_HEADS):
            _ret_decay_tables(dec_ref[hh:hh + 1, :], reverse, d_in_ref, d_q_ref, d_k_ref, d_c_ref, hh)

    def chunk_body(ci, carry):
        cidx = (n_chunks - 1 - ci) if reverse else ci
        t0 = pl.multiple_of(cidx * c, c)
        for hh in range(RET_HEADS):
            cols = slice(hh * hd, (hh + 1) * hd)
            q = rq_ref[0, pl.ds(t0, c), cols]
            kt = rkt_ref[0, cidx, cols, :]
            v = rv_ref[0, pl.ds(t0, c), cols]
            state = state_ref[hh]
            s = _dot(q, kt) * d_in_ref[hh]
            inner = _dot(s.astype(BF16), v)
            cross = _dot(q, state.astype(BF16)) * d_q_ref[hh]
            kt_dec = (kt.astype(F32) * d_k_ref[hh]).astype(BF16)
            state_ref[hh] = state * d_c_ref[hh][0:1, :] + _dot(kt_dec, v)
            y = inner + cross
            if reverse:
                y = y + fwd_ref[0, pl.ds(t0, c), cols].astype(F32)
                mu = jnp.mean(y, axis=-1, keepdims=True)
                yc = y - mu
                var = jnp.mean(yc * yc, axis=-1, keepdims=True)
                yn = yc * lax.rsqrt(var + LN_EPS) * gnw_ref[:, cols]
                gate = rg_ref[0, pl.ds(t0, c), cols].astype(F32)
                out_ref[0, pl.ds(t0, c), cols] = (gate * yn).astype(out_ref.dtype)
            else:
                out_ref[0, pl.ds(t0, c), cols] = y.astype(out_ref.dtype)
        return carry

    lax.fori_loop(0, n_chunks, chunk_body, 0, unroll=True)


def _retention(dec_rows, rq, rkt, rv, reverse, fwd=None, rg=None, gn_w=None):
    b, s, _ = rq.shape
    tb = _tile(s, 2048)
    nb = s // tb
    blk = (lambda i: nb - 1 - i) if reverse else (lambda i: i)
    tok = pl.BlockSpec((1, tb, RET_WIDTH), lambda bi, i: (bi, blk(i), 0))
    feat = pl.BlockSpec((1, tb // RET_CHUNK, RET_WIDTH, RET_CHUNK), lambda bi, i: (bi, blk(i), 0, 0))
    dec_spec = pl.BlockSpec((RET_HEADS, V7X_LANES), lambda bi, i: (0, 0))
    in_specs = [dec_spec, tok, feat, tok]
    args = [dec_rows, rq, rkt, rv]
    if reverse:
        in_specs += [tok, tok, pl.BlockSpec((1, RET_WIDTH), lambda bi, i: (0, 0))]
        args += [fwd, rg, gn_w]
    c = RET_CHUNK
    scratch = [
        pltpu.VMEM((RET_HEADS, RET_HEAD_DIM, RET_HEAD_DIM), F32),
        pltpu.VMEM((RET_HEADS, c, c), F32),
        pltpu.VMEM((RET_HEADS, c, V7X_LANES), F32),
        pltpu.VMEM((RET_HEADS, RET_HEAD_DIM, c), F32),
        pltpu.VMEM((RET_HEADS, V7X_SUBLANES, V7X_LANES), F32),
    ]
    return pl.pallas_call(
        functools.partial(_ret_kernel, reverse),
        grid=(b, nb),
        in_specs=in_specs,
        out_specs=tok,
        out_shape=jax.ShapeDtypeStruct((b, s, RET_WIDTH), BF16),
        scratch_shapes=scratch,
        compiler_params=_params("arbitrary", "arbitrary"),
        name="ret_bwd" if reverse else "ret_fwd",
    )(*args)


NEG_BIG = -1e30


ATTN_GROUP = 2 * V7X_LANES
ATTN_LAG = 3
ATTN_KV_ITEM = 2048
ATTN_KV_STEP = 2048
ATTN_Q_TILE = 4096


def _attn_kernel(q_ref, k_ref, vprev_ref, vcur_ref, o_ref, s_ref, m_old_ref, m_cur_ref, acc_ref):
    t = pl.program_id(3)
    nk = pl.num_programs(3) - 1
    gw = ATTN_GROUP
    ng = q_ref.shape[2] // gw
    rows = s_ref.shape[0]
    n_sub = k_ref.shape[2] // rows
    n_items = n_sub * ng
    lag = min(ATTN_LAG, ng - 1)
    first_carried = ng - lag
    group = lambda g: slice(g * gw, (g + 1) * gw)
    kv = lambda a: slice(a * rows, (a + 1) * rows)

    def softmax_pv(g, vt, m_prev, m_new):
        cols = group(g)
        alpha = jnp.exp2(m_prev - m_new)
        p = jnp.exp2(s_ref[:, cols] - m_new)
        acc_ref[:, cols] = alpha * acc_ref[:, cols] + _dot(vt, p.astype(BF16))

    def scores(a, g, m_prev):
        cols = group(g)
        s = _nt_dot(k_ref[0, 0, kv(a), :], q_ref[0, 0, cols, :])
        s_ref[:, cols] = s
        return jnp.maximum(m_prev, jnp.max(s, axis=0, keepdims=True))

    def carried_pv(g):
        cols = group(g)
        softmax_pv(g, vprev_ref[0, 0], m_old_ref[:, cols], m_cur_ref[:, cols])

    def step(has_prev):
        m_before, m_after = {}, {}
        for p in range(n_items):
            a, g = divmod(p, ng)
            cols = group(g)
            m_before[p] = m_cur_ref[:, cols] if a == 0 else m_after[p - ng]
            m_after[p] = scores(a, g, m_before[p])
            if p >= n_items - lag:
                m_old_ref[:, cols] = m_before[p]
                m_cur_ref[:, cols] = m_after[p]
            d = p - lag
            if d >= 0:
                da, dg = divmod(d, ng)
                softmax_pv(dg, vcur_ref[0, 0, :, kv(da)], m_before[d], m_after[d])
                if da == n_sub - 1:
                    m_cur_ref[:, group(dg)] = m_after[d]
            elif has_prev:
                carried_pv(ng + d)

    @pl.when(t == 0)
    def _():
        m_cur_ref[...] = jnp.full_like(m_cur_ref, NEG_BIG)
        acc_ref[...] = jnp.zeros_like(acc_ref)
        step(False)

    @pl.when(jnp.logical_and(t > 0, t < nk))
    def _():
        step(True)

    @pl.when(t == nk)
    def _():
        for g in range(first_carried, ng):
            carried_pv(g)
        o = acc_ref[0:MLA_V_DIM, :] / acc_ref[MLA_V_DIM:MLA_V_DIM + 1, :]
        o_ref[0] = o.T.astype(o_ref.dtype)


def _attention(q, k, vt):
    b, nh, s, _ = q.shape
    tq = _tile(s, ATTN_Q_TILE)
    tk = _tile(s, ATTN_KV_STEP)
    rows = _tile(tk, ATTN_KV_ITEM)
    n_sub = tk // rows
    nk = s // tk
    return pl.pallas_call(
        _attn_kernel,
        grid=(b, nh, s // tq, nk + 1),
        in_specs=[
            pl.BlockSpec((1, 1, tq, QK_PAD), lambda bi, hi, qi, t: (bi, hi, qi, 0)),
            pl.BlockSpec((1, 1, tk, QK_PAD), lambda bi, hi, qi, t: (bi, hi, jnp.minimum(t, nk - 1), 0)),
            pl.BlockSpec((1, 1, VT_ROWS, rows),
                         lambda bi, hi, qi, t: (bi, hi, 0, jnp.maximum(t * n_sub - 1, 0))),
            pl.BlockSpec((1, 1, VT_ROWS, tk), lambda bi, hi, qi, t: (bi, hi, 0, jnp.minimum(t, nk - 1))),
        ],
        out_specs=pl.BlockSpec((1, tq, MLA_V_DIM), lambda bi, hi, qi, t: (bi, qi, hi)),
        out_shape=jax.ShapeDtypeStruct((b, s, MLA_WIDTH), BF16),
        scratch_shapes=[
            pltpu.VMEM((rows, tq), F32),
            pltpu.VMEM((1, tq), F32),
            pltpu.VMEM((1, tq), F32),
            pltpu.VMEM((VT_ROWS, tq), F32),
        ],
        compiler_params=_params("arbitrary", "arbitrary", "arbitrary", "arbitrary"),
        name="attn",
    )(q, k, vt, vt)


FF_CHUNK = 1024
MLP_TILE = 1024
MLP_SUB = 256


def _mlp_kernel(alpha, x_ref, ret_ref, att_ref, mod_ref, w_o_ref, ln1_w_ref, ln1_b_ref,
                w_up_ref, w_down_ref, ln2_w_ref, ln2_b_ref, o_ref):
    g1 = mod_ref[0, 2:3, :]
    sh2 = mod_ref[0, 3:4, :]
    sc2 = mod_ref[0, 4:5, :]
    g2 = mod_ref[0, 5:6, :]
    n_sub = max(x_ref.shape[1] // MLP_SUB, 1)
    sub = x_ref.shape[1] // n_sub
    rows = [slice(i * sub, (i + 1) * sub) for i in range(n_sub)]
    mixes = [_dot(ret_ref[0, r, :], w_o_ref[0:RET_WIDTH, :]) + _dot(att_ref[0, r, :], w_o_ref[RET_WIDTH:, :])
             for r in rows]
    x1s, hs = [], []
    for r, mix in zip(rows, mixes):
        x1 = _layer_norm(alpha * x_ref[0, r, :] + g1 * mix) * ln1_w_ref[...] + ln1_b_ref[...]
        x1s.append(x1)
        hs.append((_layer_norm(x1) * (1.0 + sc2) + sh2).astype(BF16))
    for r, x1, h in zip(rows, x1s, hs):
        ff = jnp.zeros_like(x1)
        for c0 in range(0, D_FF, FF_CHUNK):
            u = jnp.maximum(_dot(h, w_up_ref[:, c0:c0 + FF_CHUNK]), 0.0)
            ff = ff + _dot((u * u).astype(BF16), w_down_ref[c0:c0 + FF_CHUNK, :])
        o_ref[0, r, :] = _layer_norm(alpha * x1 + g2 * ff) * ln2_w_ref[...] + ln2_b_ref[...]


def _mlp(alpha, x, ret, att, mod, mod_row0, w):
    b, s, _ = x.shape
    tm = _tile(s, MLP_TILE)
    xs = pl.BlockSpec((1, tm, D_MODEL), lambda bi, i: (bi, i, 0))
    half = pl.BlockSpec((1, tm, RET_WIDTH), lambda bi, i: (bi, i, 0))
    vec = lambda: pl.BlockSpec((1, D_MODEL), lambda bi, i: (0, 0))
    return pl.pallas_call(
        functools.partial(_mlp_kernel, alpha),
        grid=(b, s // tm),
        in_specs=[
            xs, half, half,
            pl.BlockSpec((1, N_MOD, D_MODEL), lambda bi, i: (bi + mod_row0, 0, 0)),
            _const_spec(w["w_o"].shape), vec(), vec(),
            _const_spec(w["w_up"].shape), _const_spec(w["w_down"].shape), vec(), vec(),
        ],
        out_specs=xs,
        out_shape=jax.ShapeDtypeStruct((b, s, D_MODEL), F32),
        compiler_params=_params("arbitrary", "arbitrary"),
        name="mlp",
    )(x, ret, att, mod, w["w_o"], w["ln1_w"], w["ln1_b"], w["w_up"], w["w_down"], w["ln2_w"], w["ln2_b"])


def _prep_layer_weights(w_in, ret_decay_f, ret_decay_b, ret_gn_w, q_norm_w, w_uq, kv_norm_w, w_ukv,
                        w_o, ln1_w, ln1_b, w_up, w_down, ln2_w, ln2_b):
    rw = RET_WIDTH
    w_rq, w_rk, w_rv_rg = w_in[:, :rw], w_in[:, rw:2 * rw], w_in[:, 2 * rw:4 * rw]
    w_cq = w_in[:, 4 * rw:4 * rw + Q_LORA]
    w_ckv = w_in[:, 4 * rw + Q_LORA:4 * rw + Q_LORA + KV_LORA]
    w_kr = w_in[:, 4 * rw + Q_LORA + KV_LORA:]
    pad = jnp.zeros((D_MODEL, V7X_LANES - MLA_ROPE_DIM), w_in.dtype)
    w_in_p = jnp.concatenate([w_rq, w_rv_rg, w_cq, w_kr, pad, w_ckv], axis=1).astype(BF16)
    assert w_in_p.shape[1] == _W_IN_COLS
    uq = w_uq.reshape(Q_LORA, MLA_HEADS, MLA_QK_DIM)
    ukv = w_ukv.reshape(KV_LORA, MLA_HEADS, MLA_NOPE_DIM + MLA_V_DIM)
    lanes = lambda v: jnp.broadcast_to(v.reshape(RET_HEADS, 1), (RET_HEADS, V7X_LANES)).astype(F32)
    return {
        "w_in": w_in_p,
        "w_kt": w_rk.T.astype(BF16),
        "q_norm_w": q_norm_w.reshape(1, Q_LORA),
        "w_uqn": uq[:, :, :MLA_NOPE_DIM].reshape(Q_LORA, MLA_HEADS * MLA_NOPE_DIM).astype(BF16),
        "w_uqr": uq[:, :, MLA_NOPE_DIM:].reshape(Q_LORA, MLA_HEADS * MLA_ROPE_DIM).astype(BF16),
        "kv_norm_w": kv_norm_w.reshape(1, KV_LORA),
        "w_uk": ukv[:, :, :MLA_NOPE_DIM].reshape(KV_LORA, MLA_HEADS * MLA_NOPE_DIM).astype(BF16),
        "w_uvt": ukv[:, :, MLA_NOPE_DIM:].reshape(KV_LORA, MLA_HEADS * MLA_V_DIM).T.astype(BF16),
        "dec_f": lanes(ret_decay_f),
        "dec_b": lanes(ret_decay_b),
        "gn_w": ret_gn_w.reshape(1, RET_WIDTH),
        "w_o": w_o.astype(BF16),
        "ln1_w": ln1_w.reshape(1, D_MODEL), "ln1_b": ln1_b.reshape(1, D_MODEL),
        "w_up": w_up.astype(BF16), "w_down": w_down.astype(BF16),
        "ln2_w": ln2_w.reshape(1, D_MODEL), "ln2_b": ln2_b.reshape(1, D_MODEL),
    }


def _layer(alpha, x, mod, mod_row0, tables, w):
    rq, rkt, rv, rg, q, k, vt = _proj(x, mod, mod_row0, tables, w)
    fwd = _retention(w["dec_f"], rq, rkt, rv, reverse=False)
    ret = _retention(w["dec_b"], rq, rkt, rv, reverse=True, fwd=fwd, rg=rg, gn_w=w["gn_w"])
    att = _attention(q, k, vt)
    return _mlp(alpha, x, ret, att, mod, mod_row0, w)


def _trunks(xs, cs, w_ada, b_ada, layer_weights):
    depth = w_ada.shape[0]
    alpha = float((2 * depth) ** 0.25)
    n_rows = sum(c.shape[0] for c in cs)
    assert n_rows <= ROWS_PAD
    c_pad = jnp.concatenate(list(cs) + [jnp.zeros((ROWS_PAD - n_rows, D_MODEL), F32)], axis=0)
    tables = _rotary_tables(max(x.shape[1] for x in xs))
    outs = list(xs)
    for l in range(depth):
        mod = _ada(c_pad, w_ada[l], b_ada[l]).reshape(ROWS_PAD, N_MOD, D_MODEL)
        w = _prep_layer_weights(*[t[l] for t in layer_weights])
        row0 = 0
        for i, c in enumerate(cs):
            outs[i] = _layer(alpha, outs[i], mod, row0, tables, w)
            row0 += c.shape[0]
    return tuple(outs)


def kernel(x_prompt, x_sample, c_prompt, c_sample, w_ada, b_ada, w_in, ret_decay_f, ret_decay_b, ret_gn_w,
           q_norm_w, w_uq, kv_norm_w, w_ukv, w_o, ln1_w, ln1_b, w_up, w_down, ln2_w, ln2_b):
    layer_weights = (w_in, ret_decay_f, ret_decay_b, ret_gn_w, q_norm_w, w_uq, kv_norm_w, w_ukv,
                     w_o, ln1_w, ln1_b, w_up, w_down, ln2_w, ln2_b)
    return _trunks((x_prompt, x_sample), (c_prompt, c_sample), w_ada, b_ada, layer_weights)
```

```python
import functools

import jax
import jax.numpy as jnp
from jax import lax
from jax.experimental import pallas as pl
from jax.experimental.pallas import tpu as pltpu

F32 = jnp.float32
BF16 = jnp.bfloat16

D_MODEL = 1024
RET_HEADS = 4
RET_HEAD_DIM = 128
RET_WIDTH = RET_HEADS * RET_HEAD_DIM
MLA_HEADS = 4
MLA_NOPE_DIM = 128
MLA_ROPE_DIM = 64
MLA_V_DIM = 128
MLA_WIDTH = MLA_HEADS * MLA_V_DIM
MLA_QK_DIM = MLA_NOPE_DIM + MLA_ROPE_DIM
Q_LORA = 384
KV_LORA = 256
D_FF = 4 * D_MODEL
ROPE_BASE = 10000.0
LN_EPS = 1e-5
RMS_EPS = 1e-6
N_MOD = 6
MLA_SCALE = float(MLA_QK_DIM ** -0.5)
LOG2E = 1.4426950408889634

V7X_LANES = 128
V7X_SUBLANES = 8
V7X_VMEM_BYTES = 64 * 1024 * 1024
VMEM_LIMIT = V7X_VMEM_BYTES - 8 * 1024 * 1024

QK_PAD = 2 * V7X_LANES
VT_ROWS = MLA_V_DIM + V7X_SUBLANES
_OFF_RQ, _OFF_RV, _OFF_RG = 0, RET_WIDTH, 2 * RET_WIDTH
_OFF_CQ = 3 * RET_WIDTH
_OFF_KR = _OFF_CQ + Q_LORA
_OFF_CKV = _OFF_KR + V7X_LANES
_W_IN_COLS = _OFF_CKV + KV_LORA
ROWS_PAD = V7X_SUBLANES
RET_CHUNK = 128


def _tile(n, pref):
    t = min(n, pref)
    assert n % t == 0, (n, t)
    return t


def _const_spec(shape):
    nd = len(shape)
    return pl.BlockSpec(shape, lambda *_: (0,) * nd, pipeline_mode=pl.Buffered(1))


def _params(*sem):
    return pltpu.CompilerParams(dimension_semantics=sem, vmem_limit_bytes=VMEM_LIMIT)


def _nt_dot(a, b):
    return lax.dot_general(a, b, (((1,), (1,)), ((), ())), preferred_element_type=F32)


def _dot(a, b):
    return jnp.dot(a, b, preferred_element_type=F32)


def _ada_kernel(c_ref, w_ref, b_ref, o_ref):
    c = c_ref[...]
    s = c / (1.0 + jnp.exp(-c))
    o_ref[...] = _dot(s.astype(BF16), w_ref[...].astype(BF16)) + b_ref[...]


def _ada(c_pad, w_ada, b_ada):
    n = w_ada.shape[1]
    tn = _tile(n, D_MODEL)
    return pl.pallas_call(
        _ada_kernel,
        grid=(n // tn,),
        in_specs=[
            pl.BlockSpec((ROWS_PAD, D_MODEL), lambda j: (0, 0)),
            pl.BlockSpec((D_MODEL, tn), lambda j: (0, j)),
            pl.BlockSpec((1, tn), lambda j: (0, j)),
        ],
        out_specs=pl.BlockSpec((ROWS_PAD, tn), lambda j: (0, j)),
        out_shape=jax.ShapeDtypeStruct((ROWS_PAD, n), F32),
        compiler_params=_params("arbitrary"),
        name="ada",
    )(c_pad, w_ada, b_ada.reshape(1, n))


def _tables_kernel(inv_ref, sgn_r_ref, sgn_m_ref,
                   cos_r_ref, sin_r_ref, cos_m_ref, sin_m_ref, cos_rt_ref, sin_rt_ref):
    tr = cos_r_ref.shape[0]
    seg = TABLE_SEG
    n_seg = tr // seg
    half = V7X_LANES // 2
    inv = inv_ref[...]
    ang_off = lax.broadcasted_iota(jnp.int32, (seg, V7X_LANES), 0).astype(F32) * inv
    c_off = jnp.cos(ang_off)
    s_off = jnp.sin(ang_off)
    seg_rows = -(-n_seg // V7X_SUBLANES) * V7X_SUBLANES
    seg_start = lax.broadcasted_iota(jnp.int32, (seg_rows, V7X_LANES), 0) * seg + pl.program_id(0) * tr
    ang_seg = seg_start.astype(F32) * inv
    c_seg = jnp.cos(ang_seg)
    s_seg = jnp.sin(ang_seg)
    low = lax.broadcasted_iota(jnp.int32, (seg, V7X_LANES), 1) < half
    for j in range(n_seg):
        rows = slice(j * seg, (j + 1) * seg)
        cj = c_seg[j:j + 1, :]
        sj = s_seg[j:j + 1, :]
        c = cj * c_off - sj * s_off
        s = sj * c_off + cj * s_off
        c_sw = pltpu.roll(c, half, 1)
        s_sw = pltpu.roll(s, half, 1)
        cos_r = jnp.where(low, c, c_sw)
        sin_r = jnp.where(low, s, s_sw) * sgn_r_ref[...]
        cos_r_ref[rows, :] = cos_r
        sin_r_ref[rows, :] = sin_r
        cos_m_ref[rows, :] = jnp.where(low, c_sw, c)
        sin_m_ref[rows, :] = jnp.where(low, s_sw, s) * sgn_m_ref[...]
        cos_rt_ref[:, rows] = cos_r.T
        sin_rt_ref[:, rows] = sin_r.T


TABLE_SEG = 128


def _rotary_tables(seq):
    half_r = RET_HEAD_DIM // 2
    half_m = MLA_ROPE_DIM // 2
    assert half_r + 2 * half_m == V7X_LANES and RET_HEAD_DIM == V7X_LANES
    inv_r = ROPE_BASE ** (-jnp.arange(0, RET_HEAD_DIM, 2, dtype=F32) / RET_HEAD_DIM)
    inv_m = ROPE_BASE ** (-jnp.arange(0, MLA_ROPE_DIM, 2, dtype=F32) / MLA_ROPE_DIM)
    inv = jnp.concatenate([inv_r, inv_m, inv_m])
    sgn_r = jnp.concatenate([-jnp.ones(half_r, F32), jnp.ones(half_r, F32)])
    sgn_m = jnp.tile(jnp.concatenate([-jnp.ones(half_m, F32), jnp.ones(half_m, F32)]),
                     V7X_LANES // MLA_ROPE_DIM)
    tr = _tile(seq, 1024)
    row = lambda v: v.reshape(1, V7X_LANES)
    tok = jax.ShapeDtypeStruct((seq, V7X_LANES), F32)
    feat = jax.ShapeDtypeStruct((RET_HEAD_DIM, seq), F32)
    vec = pl.BlockSpec((1, V7X_LANES), lambda i: (0, 0))
    tok_spec = pl.BlockSpec((tr, V7X_LANES), lambda i: (i, 0))
    feat_spec = pl.BlockSpec((RET_HEAD_DIM, tr), lambda i: (0, i))
    return pl.pallas_call(
        _tables_kernel,
        grid=(seq // tr,),
        in_specs=[vec, vec, vec],
        out_specs=[tok_spec, tok_spec, tok_spec, tok_spec, feat_spec, feat_spec],
        out_shape=[tok, tok, tok, tok, feat, feat],
        compiler_params=_params("arbitrary"),
        name="tables",
    )(row(inv), row(sgn_r), row(sgn_m))


def _layer_norm(x):
    mu = jnp.mean(x, axis=-1, keepdims=True)
    xc = x - mu
    var = jnp.mean(xc * xc, axis=-1, keepdims=True)
    return xc * lax.rsqrt(var + LN_EPS)


def _rms_norm(x, w):
    return x * lax.rsqrt(jnp.mean(x * x, axis=-1, keepdims=True) + RMS_EPS) * w


def _rope64(x, cos_m, sin_m, first_half):
    swapped = jnp.where(first_half,
                        pltpu.roll(x, V7X_LANES - MLA_ROPE_DIM // 2, 1),
                        pltpu.roll(x, MLA_ROPE_DIM // 2, 1))
    return x * cos_m + swapped * sin_m


def _proj_kernel(x_ref, mod_ref, cos_r_ref, sin_r_ref, cos_m_ref, sin_m_ref, cos_rt_ref, sin_rt_ref,
                 w_in_ref, w_kt_ref, qn_w_ref, w_uqn_ref, w_uqr_ref, kvn_w_ref, w_uk_ref, w_uvt_ref,
                 rq_ref, rkt_ref, rv_ref, rg_ref, q_ref, k_ref, vt_ref):
    sh1 = mod_ref[0, 0:1, :]
    sc1 = mod_ref[0, 1:2, :]
    hd = RET_HEAD_DIM
    k_scale = RET_HEAD_DIM ** -0.5
    q_scale = MLA_SCALE * LOG2E
    n_sub = max(x_ref.shape[1] // PROJ_SUB, 1)
    sub = x_ref.shape[1] // n_sub
    rows = [slice(i * sub, (i + 1) * sub) for i in range(n_sub)]
    lane = lax.broadcasted_iota(jnp.int32, (sub, V7X_LANES), 1)
    first_half = (lane % MLA_ROPE_DIM) < (MLA_ROPE_DIM // 2)
    low_lanes = lane < MLA_ROPE_DIM
    pad_row = lax.broadcasted_iota(jnp.int32, (VT_ROWS - MLA_V_DIM, sub), 0)
    ones_row = jnp.where(pad_row == 0, 1.0, 0.0).astype(BF16)

    hs = [(_layer_norm(x_ref[0, r, :]) * (1.0 + sc1) + sh1).astype(BF16) for r in rows]

    def retention_part(r, h):
        proj = lambda lo, n: _dot(h, w_in_ref[:, lo:lo + n])
        cos_r = cos_r_ref[r, :]
        sin_r = sin_r_ref[r, :]
        rq = proj(_OFF_RQ, RET_WIDTH)
        for hh in range(RET_HEADS):
            t = rq[:, hh * hd:(hh + 1) * hd]
            rq_ref[0, r, hh * hd:(hh + 1) * hd] = (t * cos_r + pltpu.roll(t, hd // 2, 1) * sin_r).astype(BF16)
        cos_rt = cos_rt_ref[:, r]
        sin_rt = sin_rt_ref[:, r]
        rkt = _nt_dot(w_kt_ref[...], h)
        chunk0 = r.start // RET_CHUNK
        for hh in range(RET_HEADS):
            t = rkt[hh * hd:(hh + 1) * hd, :]
            rot = ((t * cos_rt + pltpu.roll(t, hd // 2, 0) * sin_rt) * k_scale).astype(BF16)
            for j in range(sub // RET_CHUNK):
                rkt_ref[0, chunk0 + j, hh * hd:(hh + 1) * hd, :] = rot[:, j * RET_CHUNK:(j + 1) * RET_CHUNK]
        rv_ref[0, r, :] = proj(_OFF_RV, RET_WIDTH).astype(BF16)
        g = proj(_OFF_RG, RET_WIDTH)
        rg_ref[0, r, :] = (g / (1.0 + jnp.exp(-g))).astype(BF16)

    def mla_latents(r, h):
        proj = lambda lo, n: _dot(h, w_in_ref[:, lo:lo + n])
        cq_kr = proj(_OFF_CQ, Q_LORA + V7X_LANES)
        cq = _rms_norm(cq_kr[:, 0:Q_LORA], qn_w_ref[...]).astype(BF16)
        ckv = _rms_norm(proj(_OFF_CKV, KV_LORA), kvn_w_ref[...]).astype(BF16)
        k_pe = _rope64(cq_kr[:, Q_LORA:], cos_m_ref[r, :], sin_m_ref[r, :], first_half)
        return cq, ckv, k_pe

    def mla_heads(r, cq, ckv, k_pe):
        cos_m = cos_m_ref[r, :]
        sin_m = sin_m_ref[r, :]
        q_nope = _dot(cq, w_uqn_ref[...]) * q_scale
        q_rope = _dot(cq, w_uqr_ref[...]) * q_scale
        k_nope = _dot(ckv, w_uk_ref[...])
        v_t = _nt_dot(w_uvt_ref[...], ckv)
        for pair in range(MLA_HEADS // 2):
            rp = _rope64(q_rope[:, pair * V7X_LANES:(pair + 1) * V7X_LANES], cos_m, sin_m, first_half)
            for side in range(2):
                hh = 2 * pair + side
                keep = low_lanes if side == 0 else jnp.logical_not(low_lanes)
                q_ref[0, hh, r, 0:MLA_NOPE_DIM] = q_nope[:, hh * MLA_NOPE_DIM:(hh + 1) * MLA_NOPE_DIM].astype(BF16)
                q_ref[0, hh, r, MLA_NOPE_DIM:QK_PAD] = jnp.where(keep, rp, 0.0).astype(BF16)
        k_pe_hi = pltpu.roll(k_pe, MLA_ROPE_DIM, 1)
        for hh in range(MLA_HEADS):
            k_ref[0, hh, r, 0:MLA_NOPE_DIM] = k_nope[:, hh * MLA_NOPE_DIM:(hh + 1) * MLA_NOPE_DIM].astype(BF16)
            k_ref[0, hh, r, MLA_NOPE_DIM:QK_PAD] = (k_pe if hh % 2 == 0 else k_pe_hi).astype(BF16)
            vt_ref[0, hh, 0:MLA_V_DIM, r] = v_t[hh * MLA_V_DIM:(hh + 1) * MLA_V_DIM, :].astype(BF16)
            vt_ref[0, hh, MLA_V_DIM:VT_ROWS, r] = ones_row

    for r, h in zip(rows, hs):
        retention_part(r, h)
    latents = [mla_latents(r, h) for r, h in zip(rows, hs)]
    for r, lat in zip(rows, latents):
        mla_heads(r, *lat)


PROJ_TILE = 1024
PROJ_SUB = 256


def _proj(x, mod, mod_row0, tables, w):
    b, s, _ = x.shape
    tm = _tile(s, PROJ_TILE)
    cos_r, sin_r, cos_m, sin_m, cos_rt, sin_rt = tables
    tok_tab = pl.BlockSpec((tm, V7X_LANES), lambda bi, i: (i, 0))
    feat_tab = pl.BlockSpec((RET_HEAD_DIM, tm), lambda bi, i: (0, i))
    tok_out = lambda: pl.BlockSpec((1, tm, RET_WIDTH), lambda bi, i: (bi, i, 0))
    out_shape = [
        jax.ShapeDtypeStruct((b, s, RET_WIDTH), BF16),
        jax.ShapeDtypeStruct((b, s // RET_CHUNK, RET_WIDTH, RET_CHUNK), BF16),
        jax.ShapeDtypeStruct((b, s, RET_WIDTH), BF16),
        jax.ShapeDtypeStruct((b, s, RET_WIDTH), BF16),
        jax.ShapeDtypeStruct((b, MLA_HEADS, s, QK_PAD), BF16),
        jax.ShapeDtypeStruct((b, MLA_HEADS, s, QK_PAD), BF16),
        jax.ShapeDtypeStruct((b, MLA_HEADS, VT_ROWS, s), BF16),
    ]
    out_specs = [
        tok_out(),
        pl.BlockSpec((1, tm // RET_CHUNK, RET_WIDTH, RET_CHUNK), lambda bi, i: (bi, i, 0, 0)),
        tok_out(),
        tok_out(),
        pl.BlockSpec((1, MLA_HEADS, tm, QK_PAD), lambda bi, i: (bi, 0, i, 0)),
        pl.BlockSpec((1, MLA_HEADS, tm, QK_PAD), lambda bi, i: (bi, 0, i, 0)),
        pl.BlockSpec((1, MLA_HEADS, VT_ROWS, tm), lambda bi, i: (bi, 0, 0, i)),
    ]
    in_specs = [
        pl.BlockSpec((1, tm, D_MODEL), lambda bi, i: (bi, i, 0)),
        pl.BlockSpec((1, N_MOD, D_MODEL), lambda bi, i: (bi + mod_row0, 0, 0)),
        tok_tab, tok_tab, tok_tab, tok_tab, feat_tab, feat_tab,
        _const_spec(w["w_in"].shape), _const_spec(w["w_kt"].shape),
        _const_spec(w["q_norm_w"].shape), _const_spec(w["w_uqn"].shape), _const_spec(w["w_uqr"].shape),
        _const_spec(w["kv_norm_w"].shape), _const_spec(w["w_uk"].shape), _const_spec(w["w_uvt"].shape),
    ]
    return pl.pallas_call(
        _proj_kernel,
        grid=(b, s // tm),
        in_specs=in_specs,
        out_specs=out_specs,
        out_shape=out_shape,
        compiler_params=_params("arbitrary", "arbitrary"),
        name=f"proj_b{b}_s{s}",
    )(x, mod, cos_r, sin_r, cos_m, sin_m, cos_rt, sin_rt,
      w["w_in"], w["w_kt"], w["q_norm_w"], w["w_uqn"], w["w_uqr"], w["kv_norm_w"], w["w_uk"], w["w_uvt"])


def _log_sigmoid(x):
    return jnp.minimum(x, 0.0) - jnp.log(1.0 + jnp.exp(-jnp.abs(x)))


def _ret_decay_tables(dec_row, reverse, d_in_ref, d_q_ref, d_k_ref, d_c_ref, hh):
    c = RET_CHUNK
    lg = _log_sigmoid(dec_row)[:, 0:1]
    row = lax.broadcasted_iota(jnp.int32, (c, c), 0)
    col = lax.broadcasted_iota(jnp.int32, (c, c), 1)
    rel = (col - row) if reverse else (row - col)
    mask = (rel > 0) if reverse else (rel >= 0)
    relf = jnp.maximum(rel, 0).astype(F32)
    d_in_ref[hh] = jnp.where(mask, jnp.exp(lg * relf), 0.0)
    qi = lax.broadcasted_iota(jnp.int32, (c, V7X_LANES), 0).astype(F32)
    kj = lax.broadcasted_iota(jnp.int32, (RET_HEAD_DIM, c), 1).astype(F32)
    q_steps = (c - qi) if reverse else (qi + 1.0)
    k_steps = kj if reverse else (c - 1.0 - kj)
    d_q_ref[hh] = jnp.exp(lg * q_steps)
    d_k_ref[hh] = jnp.exp(lg * k_steps)
    d_c_ref[hh] = jnp.exp(lg * float(c)) + jnp.zeros((V7X_SUBLANES, V7X_LANES), F32)


def _ret_kernel(reverse, *refs):
    if reverse:
        (dec_ref, rq_ref, rkt_ref, rv_ref, fwd_ref, rg_ref, gnw_ref, out_ref,
         state_ref, d_in_ref, d_q_ref, d_k_ref, d_c_ref) = refs
    else:
        (dec_ref, rq_ref, rkt_ref, rv_ref, out_ref,
         state_ref, d_in_ref, d_q_ref, d_k_ref, d_c_ref) = refs
    c = RET_CHUNK
    hd = RET_HEAD_DIM
    n_chunks = rq_ref.shape[1] // c

    @pl.when(pl.program_id(1) == 0)
    def _():
        state_ref[...] = jnp.zeros_like(state_ref)
        for hh in range(RET_HEADS):
            _ret_decay_tables(dec_ref[hh:hh + 1, :], reverse, d_in_ref, d_q_ref, d_k_ref, d_c_ref, hh)

    def chunk_body(ci, carry):
        cidx = (n_chunks - 1 - ci) if reverse else ci
        t0 = pl.multiple_of(cidx * c, c)
        for hh in range(RET_HEADS):
            cols = slice(hh * hd, (hh + 1) * hd)
            q = rq_ref[0, pl.ds(t0, c), cols]
            kt = rkt_ref[0, cidx, cols, :]
            v = rv_ref[0, pl.ds(t0, c), cols]
            state = state_ref[hh]
            s = _dot(q, kt) * d_in_ref[hh]
            inner = _dot(s.astype(BF16), v)
            cross = _dot(q, state.astype(BF16)) * d_q_ref[hh]
            kt_dec = (kt.astype(F32) * d_k_ref[hh]).astype(BF16)
            state_ref[hh] = state * d_c_ref[hh][0:1, :] + _dot(kt_dec, v)
            y = inner + cross
            if reverse:
                y = y + fwd_ref[0, pl.ds(t0, c), cols].astype(F32)
                mu = jnp.mean(y, axis=-1, keepdims=True)
                yc = y - mu
                var = jnp.mean(yc * yc, axis=-1, keepdims=True)
                yn = yc * lax.rsqrt(var + LN_EPS) * gnw_ref[:, cols]
                gate = rg_ref[0, pl.ds(t0, c), cols].astype(F32)
                out_ref[0, pl.ds(t0, c), cols] = (gate * yn).astype(out_ref.dtype)
            else:
                out_ref[0, pl.ds(t0, c), cols] = y.astype(out_ref.dtype)
        return carry

    lax.fori_loop(0, n_chunks, chunk_body, 0, unroll=True)


def _retention(dec_rows, rq, rkt, rv, reverse, fwd=None, rg=None, gn_w=None):
    b, s, _ = rq.shape
    tb = _tile(s, 2048)
    nb = s // tb
    blk = (lambda i: nb - 1 - i) if reverse else (lambda i: i)
    tok = pl.BlockSpec((1, tb, RET_WIDTH), lambda bi, i: (bi, blk(i), 0))
    feat = pl.BlockSpec((1, tb // RET_CHUNK, RET_WIDTH, RET_CHUNK), lambda bi, i: (bi, blk(i), 0, 0))
    dec_spec = pl.BlockSpec((RET_HEADS, V7X_LANES), lambda bi, i: (0, 0))
    in_specs = [dec_spec, tok, feat, tok]
    args = [dec_rows, rq, rkt, rv]
    if reverse:
        in_specs += [tok, tok, pl.BlockSpec((1, RET_WIDTH), lambda bi, i: (0, 0))]
        args += [fwd, rg, gn_w]
    c = RET_CHUNK
    scratch = [
        pltpu.VMEM((RET_HEADS, RET_HEAD_DIM, RET_HEAD_DIM), F32),
        pltpu.VMEM((RET_HEADS, c, c), F32),
        pltpu.VMEM((RET_HEADS, c, V7X_LANES), F32),
        pltpu.VMEM((RET_HEADS, RET_HEAD_DIM, c), F32),
        pltpu.VMEM((RET_HEADS, V7X_SUBLANES, V7X_LANES), F32),
    ]
    return pl.pallas_call(
        functools.partial(_ret_kernel, reverse),
        grid=(b, nb),
        in_specs=in_specs,
        out_specs=tok,
        out_shape=jax.ShapeDtypeStruct((b, s, RET_WIDTH), BF16),
        scratch_shapes=scratch,
        compiler_params=_params("arbitrary", "arbitrary"),
        name=("ret_bwd" if reverse else "ret_fwd") + f"_b{b}_s{s}",
    )(*args)


NEG_BIG = -1e30


ATTN_GROUP = 2 * V7X_LANES
ATTN_LAG = 3
ATTN_KV_ITEM = 2048
ATTN_KV_STEP = 2048
ATTN_Q_TILE = 4096


def _attn_kernel(q_ref, k_ref, vprev_ref, vcur_ref, o_ref, s_ref, m_old_ref, m_cur_ref, acc_ref):
    t = pl.program_id(3)
    nk = pl.num_programs(3) - 1
    gw = ATTN_GROUP
    ng = q_ref.shape[2] // gw
    rows = s_ref.shape[0]
    n_sub = k_ref.shape[2] // rows
    n_items = n_sub * ng
    lag = min(ATTN_LAG, ng - 1)
    first_carried = ng - lag
    group = lambda g: slice(g * gw, (g + 1) * gw)
    kv = lambda a: slice(a * rows, (a + 1) * rows)

    def softmax_pv(g, vt, m_prev, m_new):
        cols = group(g)
        alpha = jnp.exp2(m_prev - m_new)
        p = jnp.exp2(s_ref[:, cols] - m_new)
        acc_ref[:, cols] = alpha * acc_ref[:, cols] + _dot(vt, p.astype(BF16))

    def scores(a, g, m_prev):
        cols = group(g)
        s = _nt_dot(k_ref[0, 0, kv(a), :], q_ref[0, 0, cols, :])
        s_ref[:, cols] = s
        return jnp.maximum(m_prev, jnp.max(s, axis=0, keepdims=True))

    def carried_pv(g):
        cols = group(g)
        softmax_pv(g, vprev_ref[0, 0], m_old_ref[:, cols], m_cur_ref[:, cols])

    def step(has_prev):
        m_before, m_after = {}, {}
        for p in range(n_items):
            a, g = divmod(p, ng)
            cols = group(g)
            m_before[p] = m_cur_ref[:, cols] if a == 0 else m_after[p - ng]
            m_after[p] = scores(a, g, m_before[p])
            if p >= n_items - lag:
                m_old_ref[:, cols] = m_before[p]
                m_cur_ref[:, cols] = m_after[p]
            d = p - lag
            if d >= 0:
                da, dg = divmod(d, ng)
                softmax_pv(dg, vcur_ref[0, 0, :, kv(da)], m_before[d], m_after[d])
                if da == n_sub - 1:
                    m_cur_ref[:, group(dg)] = m_after[d]
            elif has_prev:
                carried_pv(ng + d)

    @pl.when(t == 0)
    def _():
        m_cur_ref[...] = jnp.full_like(m_cur_ref, NEG_BIG)
        acc_ref[...] = jnp.zeros_like(acc_ref)
        step(False)

    @pl.when(jnp.logical_and(t > 0, t < nk))
    def _():
        step(True)

    @pl.when(t == nk)
    def _():
        for g in range(first_carried, ng):
            carried_pv(g)
        o = acc_ref[0:MLA_V_DIM, :] / acc_ref[MLA_V_DIM:MLA_V_DIM + 1, :]
        o_ref[0] = o.T.astype(o_ref.dtype)


def _attention(q, k, vt):
    b, nh, s, _ = q.shape
    tq = _tile(s, ATTN_Q_TILE)
    tk = _tile(s, ATTN_KV_STEP)
    rows = _tile(tk, ATTN_KV_ITEM)
    n_sub = tk // rows
    nk = s // tk
    return pl.pallas_call(
        _attn_kernel,
        grid=(b, nh, s // tq, nk + 1),
        in_specs=[
            pl.BlockSpec((1, 1, tq, QK_PAD), lambda bi, hi, qi, t: (bi, hi, qi, 0)),
            pl.BlockSpec((1, 1, tk, QK_PAD), lambda bi, hi, qi, t: (bi, hi, jnp.minimum(t, nk - 1), 0)),
            pl.BlockSpec((1, 1, VT_ROWS, rows),
                         lambda bi, hi, qi, t: (bi, hi, 0, jnp.maximum(t * n_sub - 1, 0))),
            pl.BlockSpec((1, 1, VT_ROWS, tk), lambda bi, hi, qi, t: (bi, hi, 0, jnp.minimum(t, nk - 1))),
        ],
        out_specs=pl.BlockSpec((1, tq, MLA_V_DIM), lambda bi, hi, qi, t: (bi, qi, hi)),
        out_shape=jax.ShapeDtypeStruct((b, s, MLA_WIDTH), BF16),
        scratch_shapes=[
            pltpu.VMEM((rows, tq), F32),
            pltpu.VMEM((1, tq), F32),
            pltpu.VMEM((1, tq), F32),
            pltpu.VMEM((VT_ROWS, tq), F32),
        ],
        compiler_params=_params("arbitrary", "arbitrary", "arbitrary", "arbitrary"),
        name=f"attn_b{b}_s{s}",
    )(q, k, vt, vt)


FF_CHUNK = 1024
MLP_TILE = 1024
MLP_SUB = 256


def _mlp_kernel(alpha, x_ref, ret_ref, att_ref, mod_ref, w_o_ref, ln1_w_ref, ln1_b_ref,
                w_up_ref, w_down_ref, ln2_w_ref, ln2_b_ref, o_ref):
    g1 = mod_ref[0, 2:3, :]
    sh2 = mod_ref[0, 3:4, :]
    sc2 = mod_ref[0, 4:5, :]
    g2 = mod_ref[0, 5:6, :]
    n_sub = max(x_ref.shape[1] // MLP_SUB, 1)
    sub = x_ref.shape[1] // n_sub
    rows = [slice(i * sub, (i + 1) * sub) for i in range(n_sub)]
    mixes = [_dot(ret_ref[0, r, :], w_o_ref[0:RET_WIDTH, :]) + _dot(att_ref[0, r, :], w_o_ref[RET_WIDTH:, :])
             for r in rows]
    x1s, hs = [], []
    for r, mix in zip(rows, mixes):
        x1 = _layer_norm(alpha * x_ref[0, r, :] + g1 * mix) * ln1_w_ref[...] + ln1_b_ref[...]
        x1s.append(x1)
        hs.append((_layer_norm(x1) * (1.0 + sc2) + sh2).astype(BF16))
    for r, x1, h in zip(rows, x1s, hs):
        ff = jnp.zeros_like(x1)
        for c0 in range(0, D_FF, FF_CHUNK):
            u = jnp.maximum(_dot(h, w_up_ref[:, c0:c0 + FF_CHUNK]), 0.0)
            ff = ff + _dot((u * u).astype(BF16), w_down_ref[c0:c0 + FF_CHUNK, :])
        o_ref[0, r, :] = _layer_norm(alpha * x1 + g2 * ff) * ln2_w_ref[...] + ln2_b_ref[...]


def _mlp(alpha, x, ret, att, mod, mod_row0, w):
    b, s, _ = x.shape
    tm = _tile(s, MLP_TILE)
    xs = pl.BlockSpec((1, tm, D_MODEL), lambda bi, i: (bi, i, 0))
    half = pl.BlockSpec((1, tm, RET_WIDTH), lambda bi, i: (bi, i, 0))
    vec = lambda: pl.BlockSpec((1, D_MODEL), lambda bi, i: (0, 0))
    return pl.pallas_call(
        functools.partial(_mlp_kernel, alpha),
        grid=(b, s // tm),
        in_specs=[
            xs, half, half,
            pl.BlockSpec((1, N_MOD, D_MODEL), lambda bi, i: (bi + mod_row0, 0, 0)),
            _const_spec(w["w_o"].shape), vec(), vec(),
            _const_spec(w["w_up"].shape), _const_spec(w["w_down"].shape), vec(), vec(),
        ],
        out_specs=xs,
        out_shape=jax.ShapeDtypeStruct((b, s, D_MODEL), F32),
        compiler_params=_params("arbitrary", "arbitrary"),
        name=f"mlp_b{b}_s{s}",
    )(x, ret, att, mod, w["w_o"], w["ln1_w"], w["ln1_b"], w["w_up"], w["w_down"], w["ln2_w"], w["ln2_b"])


def _prep_layer_weights(w_in, ret_decay_f, ret_decay_b, ret_gn_w, q_norm_w, w_uq, kv_norm_w, w_ukv,
                        w_o, ln1_w, ln1_b, w_up, w_down, ln2_w, ln2_b):
    rw = RET_WIDTH
    w_rq, w_rk, w_rv_rg = w_in[:, :rw], w_in[:, rw:2 * rw], w_in[:, 2 * rw:4 * rw]
    w_cq = w_in[:, 4 * rw:4 * rw + Q_LORA]
    w_ckv = w_in[:, 4 * rw + Q_LORA:4 * rw + Q_LORA + KV_LORA]
    w_kr = w_in[:, 4 * rw + Q_LORA + KV_LORA:]
    pad = jnp.zeros((D_MODEL, V7X_LANES - MLA_ROPE_DIM), w_in.dtype)
    w_in_p = jnp.concatenate([w_rq, w_rv_rg, w_cq, w_kr, pad, w_ckv], axis=1).astype(BF16)
    assert w_in_p.shape[1] == _W_IN_COLS
    uq = w_uq.reshape(Q_LORA, MLA_HEADS, MLA_QK_DIM)
    ukv = w_ukv.reshape(KV_LORA, MLA_HEADS, MLA_NOPE_DIM + MLA_V_DIM)
    lanes = lambda v: jnp.broadcast_to(v.reshape(RET_HEADS, 1), (RET_HEADS, V7X_LANES)).astype(F32)
    return {
        "w_in": w_in_p,
        "w_kt": w_rk.T.astype(BF16),
        "q_norm_w": q_norm_w.reshape(1, Q_LORA),
        "w_uqn": uq[:, :, :MLA_NOPE_DIM].reshape(Q_LORA, MLA_HEADS * MLA_NOPE_DIM).astype(BF16),
        "w_uqr": uq[:, :, MLA_NOPE_DIM:].reshape(Q_LORA, MLA_HEADS * MLA_ROPE_DIM).astype(BF16),
        "kv_norm_w": kv_norm_w.reshape(1, KV_LORA),
        "w_uk": ukv[:, :, :MLA_NOPE_DIM].reshape(KV_LORA, MLA_HEADS * MLA_NOPE_DIM).astype(BF16),
        "w_uvt": ukv[:, :, MLA_NOPE_DIM:].reshape(KV_LORA, MLA_HEADS * MLA_V_DIM).T.astype(BF16),
        "dec_f": lanes(ret_decay_f),
        "dec_b": lanes(ret_decay_b),
        "gn_w": ret_gn_w.reshape(1, RET_WIDTH),
        "w_o": w_o.astype(BF16),
        "ln1_w": ln1_w.reshape(1, D_MODEL), "ln1_b": ln1_b.reshape(1, D_MODEL),
        "w_up": w_up.astype(BF16), "w_down": w_down.astype(BF16),
        "ln2_w": ln2_w.reshape(1, D_MODEL), "ln2_b": ln2_b.reshape(1, D_MODEL),
    }


def _layer(alpha, x, mod, mod_row0, tables, w):
    rq, rkt, rv, rg, q, k, vt = _proj(x, mod, mod_row0, tables, w)
    fwd = _retention(w["dec_f"], rq, rkt, rv, reverse=False)
    ret = _retention(w["dec_b"], rq, rkt, rv, reverse=True, fwd=fwd, rg=rg, gn_w=w["gn_w"])
    att = _attention(q, k, vt)
    return _mlp(alpha, x, ret, att, mod, mod_row0, w)


def _trunks(xs, cs, w_ada, b_ada, layer_weights):
    depth = w_ada.shape[0]
    alpha = float((2 * depth) ** 0.25)
    n_rows = sum(c.shape[0] for c in cs)
    assert n_rows <= ROWS_PAD
    c_pad = jnp.concatenate(list(cs) + [jnp.zeros((ROWS_PAD - n_rows, D_MODEL), F32)], axis=0)
    tables = _rotary_tables(max(x.shape[1] for x in xs))
    outs = list(xs)
    for l in range(depth):
        mod = _ada(c_pad, w_ada[l], b_ada[l]).reshape(ROWS_PAD, N_MOD, D_MODEL)
        w = _prep_layer_weights(*[t[l] for t in layer_weights])
        row0 = 0
        for i, c in enumerate(cs):
            outs[i] = _layer(alpha, outs[i], mod, row0, tables, w)
            row0 += c.shape[0]
    return tuple(outs)


def kernel(x_prompt, x_sample, c_prompt, c_sample, w_ada, b_ada, w_in, ret_decay_f, ret_decay_b, ret_gn_w,
           q_norm_w, w_uq, kv_norm_w, w_ukv, w_o, ln1_w, ln1_b, w_up, w_down, ln2_w, ln2_b):
    layer_weights = (w_in, ret_decay_f, ret_decay_b, ret_gn_w, q_norm_w, w_uq, kv_norm_w, w_ukv,
                     w_o, ln1_w, ln1_b, w_up, w_down, ln2_w, ln2_b)
    return _trunks((x_prompt, x_sample), (c_prompt, c_sample), w_ada, b_ada, layer_weights)
```

```python
import functools

import jax
import jax.numpy as jnp
from jax import lax
from jax.experimental import pallas as pl
from jax.experimental.pallas import tpu as pltpu

F32 = jnp.float32
BF16 = jnp.bfloat16

D_MODEL = 1024
RET_HEADS = 4
RET_HEAD_DIM = 128
RET_WIDTH = RET_HEADS * RET_HEAD_DIM
MLA_HEADS = 4
MLA_NOPE_DIM = 128
MLA_ROPE_DIM = 64
MLA_V_DIM = 128
MLA_WIDTH = MLA_HEADS * MLA_V_DIM
MLA_QK_DIM = MLA_NOPE_DIM + MLA_ROPE_DIM
Q_LORA = 384
KV_LORA = 256
D_FF = 4 * D_MODEL
ROPE_BASE = 10000.0
LN_EPS = 1e-5
RMS_EPS = 1e-6
N_MOD = 6
MLA_SCALE = float(MLA_QK_DIM ** -0.5)
LOG2E = 1.4426950408889634

V7X_LANES = 128
V7X_SUBLANES = 8
V7X_VMEM_BYTES = 64 * 1024 * 1024
VMEM_LIMIT = V7X_VMEM_BYTES - 8 * 1024 * 1024

QK_PAD = 2 * V7X_LANES
VT_ROWS = MLA_V_DIM + V7X_SUBLANES
_OFF_RQ, _OFF_RV, _OFF_RG = 0, RET_WIDTH, 2 * RET_WIDTH
_OFF_CQ = 3 * RET_WIDTH
_OFF_KR = _OFF_CQ + Q_LORA
_OFF_CKV = _OFF_KR + V7X_LANES
_W_IN_COLS = _OFF_CKV + KV_LORA
ROWS_PAD = V7X_SUBLANES
RET_CHUNK = 128


def _tile(n, pref):
    t = min(n, pref)
    assert n % t == 0, (n, t)
    return t


def _const_spec(shape):
    nd = len(shape)
    return pl.BlockSpec(shape, lambda *_: (0,) * nd, pipeline_mode=pl.Buffered(1))


def _params(*sem):
    return pltpu.CompilerParams(dimension_semantics=sem, vmem_limit_bytes=VMEM_LIMIT)


def _nt_dot(a, b):
    return lax.dot_general(a, b, (((1,), (1,)), ((), ())), preferred_element_type=F32)


def _dot(a, b):
    return jnp.dot(a, b, preferred_element_type=F32)


def _ada_kernel(c_ref, w_ref, b_ref, o_ref):
    c = c_ref[...]
    s = c / (1.0 + jnp.exp(-c))
    o_ref[...] = _dot(s.astype(BF16), w_ref[...].astype(BF16)) + b_ref[...]


def _ada(c_pad, w_ada, b_ada):
    n = w_ada.shape[1]
    tn = _tile(n, D_MODEL)
    return pl.pallas_call(
        _ada_kernel,
        grid=(n // tn,),
        in_specs=[
            pl.BlockSpec((ROWS_PAD, D_MODEL), lambda j: (0, 0)),
            pl.BlockSpec((D_MODEL, tn), lambda j: (0, j)),
            pl.BlockSpec((1, tn), lambda j: (0, j)),
        ],
        out_specs=pl.BlockSpec((ROWS_PAD, tn), lambda j: (0, j)),
        out_shape=jax.ShapeDtypeStruct((ROWS_PAD, n), F32),
        compiler_params=_params("arbitrary"),
        name="ada",
    )(c_pad, w_ada, b_ada.reshape(1, n))


def _tables_kernel(inv_ref, sgn_r_ref, sgn_m_ref,
                   cos_r_ref, sin_r_ref, cos_m_ref, sin_m_ref, cos_rt_ref, sin_rt_ref):
    tr = cos_r_ref.shape[0]
    seg = TABLE_SEG
    n_seg = tr // seg
    half = V7X_LANES // 2
    inv = inv_ref[...]
    ang_off = lax.broadcasted_iota(jnp.int32, (seg, V7X_LANES), 0).astype(F32) * inv
    c_off = jnp.cos(ang_off)
    s_off = jnp.sin(ang_off)
    seg_rows = -(-n_seg // V7X_SUBLANES) * V7X_SUBLANES
    seg_start = lax.broadcasted_iota(jnp.int32, (seg_rows, V7X_LANES), 0) * seg + pl.program_id(0) * tr
    ang_seg = seg_start.astype(F32) * inv
    c_seg = jnp.cos(ang_seg)
    s_seg = jnp.sin(ang_seg)
    low = lax.broadcasted_iota(jnp.int32, (seg, V7X_LANES), 1) < half
    for j in range(n_seg):
        rows = slice(j * seg, (j + 1) * seg)
        cj = c_seg[j:j + 1, :]
        sj = s_seg[j:j + 1, :]
        c = cj * c_off - sj * s_off
        s = sj * c_off + cj * s_off
        c_sw = pltpu.roll(c, half, 1)
        s_sw = pltpu.roll(s, half, 1)
        cos_r = jnp.where(low, c, c_sw)
        sin_r = jnp.where(low, s, s_sw) * sgn_r_ref[...]
        cos_r_ref[rows, :] = cos_r
        sin_r_ref[rows, :] = sin_r
        cos_m_ref[rows, :] = jnp.where(low, c_sw, c)
        sin_m_ref[rows, :] = jnp.where(low, s_sw, s) * sgn_m_ref[...]
        cos_rt_ref[:, rows] = cos_r.T
        sin_rt_ref[:, rows] = sin_r.T


TABLE_SEG = 128


def _rotary_tables(seq):
    half_r = RET_HEAD_DIM // 2
    half_m = MLA_ROPE_DIM // 2
    assert half_r + 2 * half_m == V7X_LANES and RET_HEAD_DIM == V7X_LANES
    inv_r = ROPE_BASE ** (-jnp.arange(0, RET_HEAD_DIM, 2, dtype=F32) / RET_HEAD_DIM)
    inv_m = ROPE_BASE ** (-jnp.arange(0, MLA_ROPE_DIM, 2, dtype=F32) / MLA_ROPE_DIM)
    inv = jnp.concatenate([inv_r, inv_m, inv_m])
    sgn_r = jnp.concatenate([-jnp.ones(half_r, F32), jnp.ones(half_r, F32)])
    sgn_m = jnp.tile(jnp.concatenate([-jnp.ones(half_m, F32), jnp.ones(half_m, F32)]),
                     V7X_LANES // MLA_ROPE_DIM)
    tr = _tile(seq, 1024)
    row = lambda v: v.reshape(1, V7X_LANES)
    tok = jax.ShapeDtypeStruct((seq, V7X_LANES), F32)
    feat = jax.ShapeDtypeStruct((RET_HEAD_DIM, seq), F32)
    vec = pl.BlockSpec((1, V7X_LANES), lambda i: (0, 0))
    tok_spec = pl.BlockSpec((tr, V7X_LANES), lambda i: (i, 0))
    feat_spec = pl.BlockSpec((RET_HEAD_DIM, tr), lambda i: (0, i))
    return pl.pallas_call(
        _tables_kernel,
        grid=(seq // tr,),
        in_specs=[vec, vec, vec],
        out_specs=[tok_spec, tok_spec, tok_spec, tok_spec, feat_spec, feat_spec],
        out_shape=[tok, tok, tok, tok, feat, feat],
        compiler_params=_params("arbitrary"),
        name="tables",
    )(row(inv), row(sgn_r), row(sgn_m))


def _layer_norm(x):
    mu = jnp.mean(x, axis=-1, keepdims=True)
    xc = x - mu
    var = jnp.mean(xc * xc, axis=-1, keepdims=True)
    return xc * lax.rsqrt(var + LN_EPS)


def _rms_norm(x, w):
    return x * lax.rsqrt(jnp.mean(x * x, axis=-1, keepdims=True) + RMS_EPS) * w


def _rope64(x, cos_m, sin_m, first_half):
    swapped = jnp.where(first_half,
                        pltpu.roll(x, V7X_LANES - MLA_ROPE_DIM // 2, 1),
                        pltpu.roll(x, MLA_ROPE_DIM // 2, 1))
    return x * cos_m + swapped * sin_m


def _proj_kernel(x_ref, mod_ref, cos_r_ref, sin_r_ref, cos_m_ref, sin_m_ref, cos_rt_ref, sin_rt_ref,
                 w_in_ref, w_kt_ref, qn_w_ref, w_uqn_ref, w_uqr_ref, kvn_w_ref, w_uk_ref, w_uvt_ref,
                 rq_ref, rkt_ref, rv_ref, rg_ref, q_ref, k_ref, vt_ref):
    sh1 = mod_ref[0, 0:1, :]
    sc1 = mod_ref[0, 1:2, :]
    hd = RET_HEAD_DIM
    k_scale = RET_HEAD_DIM ** -0.5
    q_scale = MLA_SCALE * LOG2E
    n_sub = max(x_ref.shape[1] // PROJ_SUB, 1)
    sub = x_ref.shape[1] // n_sub
    rows = [slice(i * sub, (i + 1) * sub) for i in range(n_sub)]
    lane = lax.broadcasted_iota(jnp.int32, (sub, V7X_LANES), 1)
    first_half = (lane % MLA_ROPE_DIM) < (MLA_ROPE_DIM // 2)
    low_lanes = lane < MLA_ROPE_DIM
    pad_row = lax.broadcasted_iota(jnp.int32, (VT_ROWS - MLA_V_DIM, sub), 0)
    ones_row = jnp.where(pad_row == 0, 1.0, 0.0).astype(BF16)

    hs = [(_layer_norm(x_ref[0, r, :]) * (1.0 + sc1) + sh1).astype(BF16) for r in rows]

    def retention_part(r, h):
        proj = lambda lo, n: _dot(h, w_in_ref[:, lo:lo + n])
        cos_r = cos_r_ref[r, :]
        sin_r = sin_r_ref[r, :]
        rq = proj(_OFF_RQ, RET_WIDTH)
        for hh in range(RET_HEADS):
            t = rq[:, hh * hd:(hh + 1) * hd]
            rq_ref[0, r, hh * hd:(hh + 1) * hd] = (t * cos_r + pltpu.roll(t, hd // 2, 1) * sin_r).astype(BF16)
        cos_rt = cos_rt_ref[:, r]
        sin_rt = sin_rt_ref[:, r]
        rkt = _nt_dot(w_kt_ref[...], h)
        chunk0 = r.start // RET_CHUNK
        for hh in range(RET_HEADS):
            t = rkt[hh * hd:(hh + 1) * hd, :]
            rot = ((t * cos_rt + pltpu.roll(t, hd // 2, 0) * sin_rt) * k_scale).astype(BF16)
            for j in range(sub // RET_CHUNK):
                rkt_ref[0, chunk0 + j, hh * hd:(hh + 1) * hd, :] = rot[:, j * RET_CHUNK:(j + 1) * RET_CHUNK]
        rv_ref[0, r, :] = proj(_OFF_RV, RET_WIDTH).astype(BF16)
        g = proj(_OFF_RG, RET_WIDTH)
        rg_ref[0, r, :] = (g / (1.0 + jnp.exp(-g))).astype(BF16)

    def mla_latents(r, h):
        proj = lambda lo, n: _dot(h, w_in_ref[:, lo:lo + n])
        cq_kr = proj(_OFF_CQ, Q_LORA + V7X_LANES)
        cq = _rms_norm(cq_kr[:, 0:Q_LORA], qn_w_ref[...]).astype(BF16)
        ckv = _rms_norm(proj(_OFF_CKV, KV_LORA), kvn_w_ref[...]).astype(BF16)
        k_pe = _rope64(cq_kr[:, Q_LORA:], cos_m_ref[r, :], sin_m_ref[r, :], first_half)
        return cq, ckv, k_pe

    def mla_heads(r, cq, ckv, k_pe):
        cos_m = cos_m_ref[r, :]
        sin_m = sin_m_ref[r, :]
        q_nope = _dot(cq, w_uqn_ref[...]) * q_scale
        q_rope = _dot(cq, w_uqr_ref[...]) * q_scale
        k_nope = _dot(ckv, w_uk_ref[...])
        v_t = _nt_dot(w_uvt_ref[...], ckv)
        for pair in range(MLA_HEADS // 2):
            rp = _rope64(q_rope[:, pair * V7X_LANES:(pair + 1) * V7X_LANES], cos_m, sin_m, first_half)
            for side in range(2):
                hh = 2 * pair + side
                keep = low_lanes if side == 0 else jnp.logical_not(low_lanes)
                q_ref[0, hh, r, 0:MLA_NOPE_DIM] = q_nope[:, hh * MLA_NOPE_DIM:(hh + 1) * MLA_NOPE_DIM].astype(BF16)
                q_ref[0, hh, r, MLA_NOPE_DIM:QK_PAD] = jnp.where(keep, rp, 0.0).astype(BF16)
        k_pe_hi = pltpu.roll(k_pe, MLA_ROPE_DIM, 1)
        for hh in range(MLA_HEADS):
            k_ref[0, hh, r, 0:MLA_NOPE_DIM] = k_nope[:, hh * MLA_NOPE_DIM:(hh + 1) * MLA_NOPE_DIM].astype(BF16)
            k_ref[0, hh, r, MLA_NOPE_DIM:QK_PAD] = (k_pe if hh % 2 == 0 else k_pe_hi).astype(BF16)
            vt_ref[0, hh, 0:MLA_V_DIM, r] = v_t[hh * MLA_V_DIM:(hh + 1) * MLA_V_DIM, :].astype(BF16)
            vt_ref[0, hh, MLA_V_DIM:VT_ROWS, r] = ones_row

    for r, h in zip(rows, hs):
        retention_part(r, h)
    latents = [mla_latents(r, h) for r, h in zip(rows, hs)]
    for r, lat in zip(rows, latents):
        mla_heads(r, *lat)


PROJ_TILE = 1024
PROJ_SUB = 256


def _proj(x, mod, mod_row0, tables, w):
    b, s, _ = x.shape
    tm = _tile(s, PROJ_TILE)
    cos_r, sin_r, cos_m, sin_m, cos_rt, sin_rt = tables
    tok_tab = pl.BlockSpec((tm, V7X_LANES), lambda bi, i: (i, 0))
    feat_tab = pl.BlockSpec((RET_HEAD_DIM, tm), lambda bi, i: (0, i))
    tok_out = lambda: pl.BlockSpec((1, tm, RET_WIDTH), lambda bi, i: (bi, i, 0))
    out_shape = [
        jax.ShapeDtypeStruct((b, s, RET_WIDTH), BF16),
        jax.ShapeDtypeStruct((b, s // RET_CHUNK, RET_WIDTH, RET_CHUNK), BF16),
        jax.ShapeDtypeStruct((b, s, RET_WIDTH), BF16),
        jax.ShapeDtypeStruct((b, s, RET_WIDTH), BF16),
        jax.ShapeDtypeStruct((b, MLA_HEADS, s, QK_PAD), BF16),
        jax.ShapeDtypeStruct((b, MLA_HEADS, s, QK_PAD), BF16),
        jax.ShapeDtypeStruct((b, MLA_HEADS, VT_ROWS, s), BF16),
    ]
    out_specs = [
        tok_out(),
        pl.BlockSpec((1, tm // RET_CHUNK, RET_WIDTH, RET_CHUNK), lambda bi, i: (bi, i, 0, 0)),
        tok_out(),
        tok_out(),
        pl.BlockSpec((1, MLA_HEADS, tm, QK_PAD), lambda bi, i: (bi, 0, i, 0)),
        pl.BlockSpec((1, MLA_HEADS, tm, QK_PAD), lambda bi, i: (bi, 0, i, 0)),
        pl.BlockSpec((1, MLA_HEADS, VT_ROWS, tm), lambda bi, i: (bi, 0, 0, i)),
    ]
    in_specs = [
        pl.BlockSpec((1, tm, D_MODEL), lambda bi, i: (bi, i, 0)),
        pl.BlockSpec((1, N_MOD, D_MODEL), lambda bi, i: (bi + mod_row0, 0, 0)),
        tok_tab, tok_tab, tok_tab, tok_tab, feat_tab, feat_tab,
        _const_spec(w["w_in"].shape), _const_spec(w["w_kt"].shape),
        _const_spec(w["q_norm_w"].shape), _const_spec(w["w_uqn"].shape), _const_spec(w["w_uqr"].shape),
        _const_spec(w["kv_norm_w"].shape), _const_spec(w["w_uk"].shape), _const_spec(w["w_uvt"].shape),
    ]
    return pl.pallas_call(
        _proj_kernel,
        grid=(b, s // tm),
        in_specs=in_specs,
        out_specs=out_specs,
        out_shape=out_shape,
        compiler_params=_params("arbitrary", "arbitrary"),
        name=f"proj_b{b}_s{s}",
    )(x, mod, cos_r, sin_r, cos_m, sin_m, cos_rt, sin_rt,
      w["w_in"], w["w_kt"], w["q_norm_w"], w["w_uqn"], w["w_uqr"], w["kv_norm_w"], w["w_uk"], w["w_uvt"])


def _log_sigmoid(x):
    return jnp.minimum(x, 0.0) - jnp.log(1.0 + jnp.exp(-jnp.abs(x)))


def _ret_decay_tables(dec_row, reverse, d_in_ref, d_q_ref, d_k_ref, d_c_ref, hh):
    c = RET_CHUNK
    lg = _log_sigmoid(dec_row)[:, 0:1]
    row = lax.broadcasted_iota(jnp.int32, (c, c), 0)
    col = lax.broadcasted_iota(jnp.int32, (c, c), 1)
    rel = (col - row) if reverse else (row - col)
    mask = (rel > 0) if reverse else (rel >= 0)
    relf = jnp.maximum(rel, 0).astype(F32)
    d_in_ref[hh] = jnp.where(mask, jnp.exp(lg * relf), 0.0)
    qi = lax.broadcasted_iota(jnp.int32, (c, V7X_LANES), 0).astype(F32)
    kj = lax.broadcasted_iota(jnp.int32, (RET_HEAD_DIM, c), 1).astype(F32)
    q_steps = (c - qi) if reverse else (qi + 1.0)
    k_steps = kj if reverse else (c - 1.0 - kj)
    d_q_ref[hh] = jnp.exp(lg * q_steps)
    d_k_ref[hh] = jnp.exp(lg * k_steps)
    d_c_ref[hh] = jnp.exp(lg * float(c)) + jnp.zeros((V7X_SUBLANES, V7X_LANES), F32)


def _ret_kernel(reverse, *refs):
    if reverse:
        (dec_ref, rq_ref, rkt_ref, rv_ref, fwd_ref, rg_ref, gnw_ref, out_ref,
         state_ref, d_in_ref, d_q_ref, d_k_ref, d_c_ref) = refs
    else:
        (dec_ref, rq_ref, rkt_ref, rv_ref, out_ref,
         state_ref, d_in_ref, d_q_ref, d_k_ref, d_c_ref) = refs
    c = RET_CHUNK
    hd = RET_HEAD_DIM
    n_chunks = rq_ref.shape[1] // c

    @pl.when(pl.program_id(1) == 0)
    def _():
        state_ref[...] = jnp.zeros_like(state_ref)
        for hh in range(RET_HEADS):
            _ret_decay_tables(dec_ref[hh:hh + 1, :], reverse, d_in_ref, d_q_ref, d_k_ref, d_c_ref, hh)

    def chunk_body(ci, carry):
        cidx = (n_chunks - 1 - ci) if reverse else ci
        t0 = pl.multiple_of(cidx * c, c)
        for hh in range(RET_HEADS):
            cols = slice(hh * hd, (hh + 1) * hd)
            q = rq_ref[0, pl.ds(t0, c), cols]
            kt = rkt_ref[0, cidx, cols, :]
            v = rv_ref[0, pl.ds(t0, c), cols]
            state = state_ref[hh]
            s = _dot(q, kt) * d_in_ref[hh]
            inner = _dot(s.astype(BF16), v)
            cross = _dot(q, state.astype(BF16)) * d_q_ref[hh]
            kt_dec = (kt.astype(F32) * d_k_ref[hh]).astype(BF16)
            state_ref[hh] = state * d_c_ref[hh][0:1, :] + _dot(kt_dec, v)
            y = inner + cross
            if reverse:
                y = y + fwd_ref[0, pl.ds(t0, c), cols].astype(F32)
                mu = jnp.mean(y, axis=-1, keepdims=True)
                yc = y - mu
                var = jnp.mean(yc * yc, axis=-1, keepdims=True)
                yn = yc * lax.rsqrt(var + LN_EPS) * gnw_ref[:, cols]
                gate = rg_ref[0, pl.ds(t0, c), cols].astype(F32)
                out_ref[0, pl.ds(t0, c), cols] = (gate * yn).astype(out_ref.dtype)
            else:
                out_ref[0, pl.ds(t0, c), cols] = y.astype(out_ref.dtype)
        return carry

    lax.fori_loop(0, n_chunks, chunk_body, 0, unroll=True)


def _retention(dec_rows, rq, rkt, rv, reverse, fwd=None, rg=None, gn_w=None):
    b, s, _ = rq.shape
    tb = _tile(s, 2048)
    nb = s // tb
    blk = (lambda i: nb - 1 - i) if reverse else (lambda i: i)
    tok = pl.BlockSpec((1, tb, RET_WIDTH), lambda bi, i: (bi, blk(i), 0))
    feat = pl.BlockSpec((1, tb // RET_CHUNK, RET_WIDTH, RET_CHUNK), lambda bi, i: (bi, blk(i), 0, 0))
    dec_spec = pl.BlockSpec((RET_HEADS, V7X_LANES), lambda bi, i: (0, 0))
    in_specs = [dec_spec, tok, feat, tok]
    args = [dec_rows, rq, rkt, rv]
    if reverse:
        in_specs += [tok, tok, pl.BlockSpec((1, RET_WIDTH), lambda bi, i: (0, 0))]
        args += [fwd, rg, gn_w]
    c = RET_CHUNK
    scratch = [
        pltpu.VMEM((RET_HEADS, RET_HEAD_DIM, RET_HEAD_DIM), F32),
        pltpu.VMEM((RET_HEADS, c, c), F32),
        pltpu.VMEM((RET_HEADS, c, V7X_LANES), F32),
        pltpu.VMEM((RET_HEADS, RET_HEAD_DIM, c), F32),
        pltpu.VMEM((RET_HEADS, V7X_SUBLANES, V7X_LANES), F32),
    ]
    return pl.pallas_call(
        functools.partial(_ret_kernel, reverse),
        grid=(b, nb),
        in_specs=in_specs,
        out_specs=tok,
        out_shape=jax.ShapeDtypeStruct((b, s, RET_WIDTH), BF16),
        scratch_shapes=scratch,
        compiler_params=_params("arbitrary", "arbitrary"),
        name=("ret_bwd" if reverse else "ret_fwd") + f"_b{b}_s{s}",
    )(*args)


NEG_BIG = -1e30


ATTN_GROUP = 2 * V7X_LANES
ATTN_LAG = 3
ATTN_KV_ITEM = 2048
ATTN_KV_STEP = 2048
ATTN_MIN_KV_STEPS = 4
ATTN_Q_TILE = 4096


def _attn_kernel(q_ref, k_ref, vprev_ref, vcur_ref, o_ref, s_ref, m_old_ref, m_cur_ref, acc_ref):
    t = pl.program_id(3)
    nk = pl.num_programs(3) - 1
    gw = ATTN_GROUP
    ng = q_ref.shape[2] // gw
    rows = s_ref.shape[0]
    n_sub = k_ref.shape[2] // rows
    n_items = n_sub * ng
    lag = min(ATTN_LAG, ng - 1)
    first_carried = ng - lag
    group = lambda g: slice(g * gw, (g + 1) * gw)
    kv = lambda a: slice(a * rows, (a + 1) * rows)

    def softmax_pv(g, vt, m_prev, m_new):
        cols = group(g)
        alpha = jnp.exp2(m_prev - m_new)
        p = jnp.exp2(s_ref[:, cols] - m_new)
        acc_ref[:, cols] = alpha * acc_ref[:, cols] + _dot(vt, p.astype(BF16))

    def scores(a, g, m_prev):
        cols = group(g)
        s = _nt_dot(k_ref[0, 0, kv(a), :], q_ref[0, 0, cols, :])
        s_ref[:, cols] = s
        return jnp.maximum(m_prev, jnp.max(s, axis=0, keepdims=True))

    def carried_pv(g):
        cols = group(g)
        softmax_pv(g, vprev_ref[0, 0], m_old_ref[:, cols], m_cur_ref[:, cols])

    def step(has_prev):
        m_before, m_after = {}, {}
        for p in range(n_items):
            a, g = divmod(p, ng)
            cols = group(g)
            m_before[p] = m_cur_ref[:, cols] if a == 0 else m_after[p - ng]
            m_after[p] = scores(a, g, m_before[p])
            if p >= n_items - lag:
                m_old_ref[:, cols] = m_before[p]
                m_cur_ref[:, cols] = m_after[p]
            d = p - lag
            if d >= 0:
                da, dg = divmod(d, ng)
                softmax_pv(dg, vcur_ref[0, 0, :, kv(da)], m_before[d], m_after[d])
                if da == n_sub - 1:
                    m_cur_ref[:, group(dg)] = m_after[d]
            elif has_prev:
                carried_pv(ng + d)

    @pl.when(t == 0)
    def _():
        m_cur_ref[...] = jnp.full_like(m_cur_ref, NEG_BIG)
        acc_ref[...] = jnp.zeros_like(acc_ref)
        step(False)

    @pl.when(jnp.logical_and(t > 0, t < nk))
    def _():
        step(True)

    @pl.when(t == nk)
    def _():
        for g in range(first_carried, ng):
            carried_pv(g)
        o = acc_ref[0:MLA_V_DIM, :] / acc_ref[MLA_V_DIM:MLA_V_DIM + 1, :]
        o_ref[0] = o.T.astype(o_ref.dtype)


def _attention(q, k, vt):
    b, nh, s, _ = q.shape
    tq = _tile(s, ATTN_Q_TILE)
    tk = _tile(s, max(ATTN_KV_STEP, min(2 * ATTN_KV_STEP, s // ATTN_MIN_KV_STEPS)))
    rows = _tile(tk, ATTN_KV_ITEM)
    n_sub = tk // rows
    nk = s // tk
    return pl.pallas_call(
        _attn_kernel,
        grid=(b, nh, s // tq, nk + 1),
        in_specs=[
            pl.BlockSpec((1, 1, tq, QK_PAD), lambda bi, hi, qi, t: (bi, hi, qi, 0)),
            pl.BlockSpec((1, 1, tk, QK_PAD), lambda bi, hi, qi, t: (bi, hi, jnp.minimum(t, nk - 1), 0)),
            pl.BlockSpec((1, 1, VT_ROWS, rows),
                         lambda bi, hi, qi, t: (bi, hi, 0, jnp.maximum(t * n_sub - 1, 0))),
            pl.BlockSpec((1, 1, VT_ROWS, tk), lambda bi, hi, qi, t: (bi, hi, 0, jnp.minimum(t, nk - 1))),
        ],
        out_specs=pl.BlockSpec((1, tq, MLA_V_DIM), lambda bi, hi, qi, t: (bi, qi, hi)),
        out_shape=jax.ShapeDtypeStruct((b, s, MLA_WIDTH), BF16),
        scratch_shapes=[
            pltpu.VMEM((rows, tq), F32),
            pltpu.VMEM((1, tq), F32),
            pltpu.VMEM((1, tq), F32),
            pltpu.VMEM((VT_ROWS, tq), F32),
        ],
        compiler_params=_params("arbitrary", "arbitrary", "arbitrary", "arbitrary"),
        name=f"attn_b{b}_s{s}",
    )(q, k, vt, vt)


FF_CHUNK = 1024
MLP_TILE = 1024
MLP_SUB = 256


def _mlp_kernel(alpha, x_ref, ret_ref, att_ref, mod_ref, w_o_ref, ln1_w_ref, ln1_b_ref,
                w_up_ref, w_down_ref, ln2_w_ref, ln2_b_ref, o_ref):
    g1 = mod_ref[0, 2:3, :]
    sh2 = mod_ref[0, 3:4, :]
    sc2 = mod_ref[0, 4:5, :]
    g2 = mod_ref[0, 5:6, :]
    n_sub = max(x_ref.shape[1] // MLP_SUB, 1)
    sub = x_ref.shape[1] // n_sub
    rows = [slice(i * sub, (i + 1) * sub) for i in range(n_sub)]
    mixes = [_dot(ret_ref[0, r, :], w_o_ref[0:RET_WIDTH, :]) + _dot(att_ref[0, r, :], w_o_ref[RET_WIDTH:, :])
             for r in rows]
    x1s, hs = [], []
    for r, mix in zip(rows, mixes):
        x1 = _layer_norm(alpha * x_ref[0, r, :] + g1 * mix) * ln1_w_ref[...] + ln1_b_ref[...]
        x1s.append(x1)
        hs.append((_layer_norm(x1) * (1.0 + sc2) + sh2).astype(BF16))
    for r, x1, h in zip(rows, x1s, hs):
        ff = jnp.zeros_like(x1)
        for c0 in range(0, D_FF, FF_CHUNK):
            u = jnp.maximum(_dot(h, w_up_ref[:, c0:c0 + FF_CHUNK]), 0.0)
            ff = ff + _dot((u * u).astype(BF16), w_down_ref[c0:c0 + FF_CHUNK, :])
        o_ref[0, r, :] = _layer_norm(alpha * x1 + g2 * ff) * ln2_w_ref[...] + ln2_b_ref[...]


def _mlp(alpha, x, ret, att, mod, mod_row0, w):
    b, s, _ = x.shape
    tm = _tile(s, MLP_TILE)
    xs = pl.BlockSpec((1, tm, D_MODEL), lambda bi, i: (bi, i, 0))
    half = pl.BlockSpec((1, tm, RET_WIDTH), lambda bi, i: (bi, i, 0))
    vec = lambda: pl.BlockSpec((1, D_MODEL), lambda bi, i: (0, 0))
    return pl.pallas_call(
        functools.partial(_mlp_kernel, alpha),
        grid=(b, s // tm),
        in_specs=[
            xs, half, half,
            pl.BlockSpec((1, N_MOD, D_MODEL), lambda bi, i: (bi + mod_row0, 0, 0)),
            _const_spec(w["w_o"].shape), vec(), vec(),
            _const_spec(w["w_up"].shape), _const_spec(w["w_down"].shape), vec(), vec(),
        ],
        out_specs=xs,
        out_shape=jax.ShapeDtypeStruct((b, s, D_MODEL), F32),
        compiler_params=_params("arbitrary", "arbitrary"),
        name=f"mlp_b{b}_s{s}",
    )(x, ret, att, mod, w["w_o"], w["ln1_w"], w["ln1_b"], w["w_up"], w["w_down"], w["ln2_w"], w["ln2_b"])


def _prep_layer_weights(w_in, ret_decay_f, ret_decay_b, ret_gn_w, q_norm_w, w_uq, kv_norm_w, w_ukv,
                        w_o, ln1_w, ln1_b, w_up, w_down, ln2_w, ln2_b):
    rw = RET_WIDTH
    w_rq, w_rk, w_rv_rg = w_in[:, :rw], w_in[:, rw:2 * rw], w_in[:, 2 * rw:4 * rw]
    w_cq = w_in[:, 4 * rw:4 * rw + Q_LORA]
    w_ckv = w_in[:, 4 * rw + Q_LORA:4 * rw + Q_LORA + KV_LORA]
    w_kr = w_in[:, 4 * rw + Q_LORA + KV_LORA:]
    pad = jnp.zeros((D_MODEL, V7X_LANES - MLA_ROPE_DIM), w_in.dtype)
    w_in_p = jnp.concatenate([w_rq, w_rv_rg, w_cq, w_kr, pad, w_ckv], axis=1).astype(BF16)
    assert w_in_p.shape[1] == _W_IN_COLS
    uq = w_uq.reshape(Q_LORA, MLA_HEADS, MLA_QK_DIM)
    ukv = w_ukv.reshape(KV_LORA, MLA_HEADS, MLA_NOPE_DIM + MLA_V_DIM)
    lanes = lambda v: jnp.broadcast_to(v.reshape(RET_HEADS, 1), (RET_HEADS, V7X_LANES)).astype(F32)
    return {
        "w_in": w_in_p,
        "w_kt": w_rk.T.astype(BF16),
        "q_norm_w": q_norm_w.reshape(1, Q_LORA),
        "w_uqn": uq[:, :, :MLA_NOPE_DIM].reshape(Q_LORA, MLA_HEADS * MLA_NOPE_DIM).astype(BF16),
        "w_uqr": uq[:, :, MLA_NOPE_DIM:].reshape(Q_LORA, MLA_HEADS * MLA_ROPE_DIM).astype(BF16),
        "kv_norm_w": kv_norm_w.reshape(1, KV_LORA),
        "w_uk": ukv[:, :, :MLA_NOPE_DIM].reshape(KV_LORA, MLA_HEADS * MLA_NOPE_DIM).astype(BF16),
        "w_uvt": ukv[:, :, MLA_NOPE_DIM:].reshape(KV_LORA, MLA_HEADS * MLA_V_DIM).T.astype(BF16),
        "dec_f": lanes(ret_decay_f),
        "dec_b": lanes(ret_decay_b),
        "gn_w": ret_gn_w.reshape(1, RET_WIDTH),
        "w_o": w_o.astype(BF16),
        "ln1_w": ln1_w.reshape(1, D_MODEL), "ln1_b": ln1_b.reshape(1, D_MODEL),
        "w_up": w_up.astype(BF16), "w_down": w_down.astype(BF16),
        "ln2_w": ln2_w.reshape(1, D_MODEL), "ln2_b": ln2_b.reshape(1, D_MODEL),
    }


def _layer(alpha, x, mod, mod_row0, tables, w):
    rq, rkt, rv, rg, q, k, vt = _proj(x, mod, mod_row0, tables, w)
    fwd = _retention(w["dec_f"], rq, rkt, rv, reverse=False)
    ret = _retention(w["dec_b"], rq, rkt, rv, reverse=True, fwd=fwd, rg=rg, gn_w=w["gn_w"])
    att = _attention(q, k, vt)
    return _mlp(alpha, x, ret, att, mod, mod_row0, w)


def _trunks(xs, cs, w_ada, b_ada, layer_weights):
    depth = w_ada.shape[0]
    alpha = float((2 * depth) ** 0.25)
    n_rows = sum(c.shape[0] for c in cs)
    assert n_rows <= ROWS_PAD
    c_pad = jnp.concatenate(list(cs) + [jnp.zeros((ROWS_PAD - n_rows, D_MODEL), F32)], axis=0)
    tables = _rotary_tables(max(x.shape[1] for x in xs))
    outs = list(xs)
    for l in range(depth):
        mod = _ada(c_pad, w_ada[l], b_ada[l]).reshape(ROWS_PAD, N_MOD, D_MODEL)
        w = _prep_layer_weights(*[t[l] for t in layer_weights])
        row0 = 0
        for i, c in enumerate(cs):
            outs[i] = _layer(alpha, outs[i], mod, row0, tables, w)
            row0 += c.shape[0]
    return tuple(outs)


def kernel(x_prompt, x_sample, c_prompt, c_sample, w_ada, b_ada, w_in, ret_decay_f, ret_decay_b, ret_gn_w,
           q_norm_w, w_uq, kv_norm_w, w_ukv, w_o, ln1_w, ln1_b, w_up, w_down, ln2_w, ln2_b):
    layer_weights = (w_in, ret_decay_f, ret_decay_b, ret_gn_w, q_norm_w, w_uq, kv_norm_w, w_ukv,
                     w_o, ln1_w, ln1_b, w_up, w_down, ln2_w, ln2_b)
    return _trunks((x_prompt, x_sample), (c_prompt, c_sample), w_ada, b_ada, layer_weights)
```
